```python
import jax, jax.numpy as jnp
from jax import lax
import numpy as np

D_MODEL = 2048
BATCH = 16
SEQ = 2048
DEPTH = 1

GRID_W = 64
CTX_LEN = 256
NA_HEADS = 8
NA_HEAD_DIM = 128
NA_WIN_H_MAX = 8
NA_WIN_W = 16
RET_HEADS = 8
RET_QK_DIM = 64
RET_V_DIM = 128
RET_CHUNK = 128
NA_WIDTH = NA_HEADS * NA_HEAD_DIM
RET_QK_WIDTH = RET_HEADS * RET_QK_DIM
RET_V_WIDTH = RET_HEADS * RET_V_DIM
MIX_WIDTH = NA_WIDTH + RET_V_WIDTH
IN_WIDTH = 3 * NA_WIDTH + 2 * RET_QK_WIDTH + 2 * RET_V_WIDTH
D_FF = -(-8 * D_MODEL // (3 * 256)) * 256
ROPE_BASE = 10000.0
EPS = 1e-6

kernel_name = 'hybrid_na_retention_dit_block'


def rms_norm(x, g):
    xf = x.astype(jnp.float32)
    y = xf * lax.rsqrt(jnp.mean(xf * xf, axis=-1, keepdims=True) + EPS)
    return (y * g.astype(jnp.float32)).astype(x.dtype)


def adaln(cvec, ada_w, ada_b):
    m = jax.nn.silu(cvec) @ ada_w + ada_b
    return jnp.split(m, 6, axis=-1)


def modulate(x, g, shift, scale):
    return rms_norm(x, g) * (1 + scale) + shift


def split_projection(p):
    B, L, _ = p.shape
    sizes = [NA_WIDTH, NA_WIDTH, NA_WIDTH, RET_QK_WIDTH, RET_QK_WIDTH, RET_V_WIDTH]
    offs = np.cumsum(sizes).tolist()
    na_q, na_k, na_v, r_q, r_k, r_v, r_g = jnp.split(p, offs, axis=-1)
    heads = lambda t, h: t.reshape(B, L, h, -1).transpose(0, 2, 1, 3)
    return (heads(na_q, NA_HEADS), heads(na_k, NA_HEADS), heads(na_v, NA_HEADS),
            heads(r_q, RET_HEADS), heads(r_k, RET_HEADS), heads(r_v, RET_HEADS), r_g)


def axial_rope(x, pos_row, pos_col):
    d = x.shape[-1]
    half = d // 2
    quarter = half // 2
    inv_freq = ROPE_BASE ** (-jnp.arange(quarter, dtype=jnp.float32) / quarter)

    def rot(xa, pos):
        ang = pos.astype(jnp.float32)[:, None] * inv_freq
        cos, sin = jnp.cos(ang), jnp.sin(ang)
        x1 = xa[..., :quarter].astype(jnp.float32)
        x2 = xa[..., quarter:].astype(jnp.float32)
        return jnp.concatenate([x1 * cos - x2 * sin, x1 * sin + x2 * cos], axis=-1)

    out = jnp.concatenate([rot(x[..., :half], pos_row), rot(x[..., half:], pos_col)], axis=-1)
    return out.astype(x.dtype)


def neighbourhood_attention(q, k, v, k_ctx, v_ctx, rpb):
    B, H, rows, W, d = q.shape
    win_h = min(NA_WIN_H_MAX, rows)
    n_loc = win_h * NA_WIN_W
    scale = d ** -0.5
    cols = jnp.arange(W)
    col_start = jnp.clip(cols - NA_WIN_W // 2, 0, W - NA_WIN_W)
    col_idx = col_start[:, None] + jnp.arange(NA_WIN_W)[None, :]
    dc = col_idx - cols[:, None]

    def row_block(r):
        rs = jnp.clip(r - win_h // 2, 0, rows - win_h)
        q_r = lax.dynamic_index_in_dim(q, r, axis=2, keepdims=False)
        k_band = lax.dynamic_slice_in_dim(k, rs, win_h, axis=2)
        v_band = lax.dynamic_slice_in_dim(v, rs, win_h, axis=2)
        k_win = k_band[:, :, :, col_idx]
        v_win = v_band[:, :, :, col_idx]
        dr = rs + jnp.arange(win_h) - r
        bias = rpb[:, dr[None, :, None] + NA_WIN_H_MAX - 1, dc[:, None, :] + NA_WIN_W - 1]
        s_loc = jnp.einsum('bhqd,bhiqjd->bhqij', q_r, k_win).astype(jnp.float32) * scale
        s_loc = (s_loc + bias.astype(jnp.float32)[None]).reshape(B, H, W, n_loc)
        s_ctx = jnp.einsum('bhqd,bhkd->bhqk', q_r, k_ctx).astype(jnp.float32) * scale
        p = jax.nn.softmax(jnp.concatenate([s_loc, s_ctx], axis=-1), axis=-1).astype(v.dtype)
        p_loc = p[..., :n_loc].reshape(B, H, W, win_h, NA_WIN_W)
        p_ctx = p[..., n_loc:]
        return (jnp.einsum('bhqij,bhiqjd->bhqd', p_loc, v_win)
                + jnp.einsum('bhqk,bhkd->bhqd', p_ctx, v_ctx))

    out = lax.map(row_block, jnp.arange(rows))
    return jnp.moveaxis(out, 0, 2).reshape(B, H, rows * W, d)


def context_attention(q, k, v):
    s = jnp.einsum('bhqd,bhkd->bhqk', q, k).astype(jnp.float32) * q.shape[-1] ** -0.5
    p = jax.nn.softmax(s, axis=-1).astype(v.dtype)
    return jnp.einsum('bhqk,bhkd->bhqd', p, v)


def retention_chunkwise(q, k, v, log_gamma, s0):
    B, H, L, dk = q.shape
    dv = v.shape[-1]
    C = RET_CHUNK
    n = L // C
    lg = log_gamma.astype(jnp.float32)
    idx = jnp.arange(C, dtype=jnp.float32)
    diff = idx[:, None] - idx[None, :]
    decay_in = jnp.where(diff >= 0, jnp.exp(lg[:, None, None] * jnp.maximum(diff, 0.0)), 0.0)
    decay_q = jnp.exp(lg[:, None] * (idx + 1.0))[..., None]
    decay_k = jnp.exp(lg[:, None] * (C - 1.0 - idx))[..., None]
    decay_chunk = jnp.exp(lg * C)[:, None, None]

    def chunks(t):
        return jnp.moveaxis(t.astype(jnp.float32).reshape(B, H, n, C, t.shape[-1]), 2, 0)

    qs, ks, vs = chunks(q * dk ** -0.5), chunks(k), chunks(v)

    def step(s, inp):
        qc, kc, vc = inp
        inner = jnp.einsum('bhid,bhjd->bhij', qc, kc) * decay_in
        o = (jnp.einsum('bhij,bhjv->bhiv', inner, vc)
             + jnp.einsum('bhid,bhdv->bhiv', qc * decay_q, s))
        s_new = s * decay_chunk + jnp.einsum('bhjd,bhjv->bhdv', kc * decay_k, vc)
        return s_new, o

    s_fin, o = lax.scan(step, s0, (qs, ks, vs))
    o = jnp.moveaxis(o, 0, 2).reshape(B, H, L, dv).astype(v.dtype)
    return o, s_fin


def merge_mixers(o_na, o_ret, r_g, w_out):
    B, _, L, _ = o_na.shape
    of = o_ret.astype(jnp.float32)
    mu = jnp.mean(of, axis=-1, keepdims=True)
    var = jnp.mean(jnp.square(of - mu), axis=-1, keepdims=True)
    o_ret_n = ((of - mu) * lax.rsqrt(var + EPS)).astype(o_ret.dtype)
    na = o_na.transpose(0, 2, 1, 3).reshape(B, L, NA_WIDTH)
    ret = o_ret_n.transpose(0, 2, 1, 3).reshape(B, L, RET_V_WIDTH) * jax.nn.silu(r_g)
    return jnp.concatenate([na, ret], axis=-1) @ w_out


def swiglu(h, w_gate, w_up, w_down):
    return (jax.nn.silu(h @ w_gate) * (h @ w_up)) @ w_down


def setup_inputs(seed: int = 0) -> dict:
    key = jax.random.key(seed)
    ks = jax.random.split(key, 18)
    f32 = jnp.float32

    def nrm(k, shape, scale):
        return jax.random.normal(k, shape, f32) * scale

    base_lg = jnp.log1p(-jnp.exp2(-5.0 - jnp.arange(RET_HEADS, dtype=f32)))
    return {
        'x': nrm(ks[0], (BATCH, SEQ, D_MODEL), 1.0),
        'c': nrm(ks[1], (BATCH, D_MODEL), 1.0),
        'ctx': nrm(ks[2], (BATCH, CTX_LEN, D_MODEL), 1.0),
        'c_ctx': nrm(ks[3], (D_MODEL,), 1.0),
        'ada_w': nrm(ks[4], (DEPTH, D_MODEL, 6 * D_MODEL), 0.5 * D_MODEL ** -0.5),
        'ada_b': nrm(ks[5], (DEPTH, 6 * D_MODEL), 0.02),
        'norm_pre_mix': 1.0 + nrm(ks[6], (DEPTH, D_MODEL), 0.05),
        'norm_post_mix': 1.0 + nrm(ks[7], (DEPTH, D_MODEL), 0.05),
        'norm_pre_ffn': 1.0 + nrm(ks[8], (DEPTH, D_MODEL), 0.05),
        'norm_post_ffn': 1.0 + nrm(ks[9], (DEPTH, D_MODEL), 0.05),
        'w_in': nrm(ks[10], (DEPTH, D_MODEL, IN_WIDTH), D_MODEL ** -0.5),
        'na_rpb': nrm(ks[11], (DEPTH, NA_HEADS, 2 * NA_WIN_H_MAX - 1, 2 * NA_WIN_W - 1), 0.1),
        'ret_log_gamma_fwd': base_lg * (1.0 + nrm(ks[12], (DEPTH, RET_HEADS), 0.05)),
        'ret_log_gamma_bwd': base_lg * (1.0 + nrm(ks[13], (DEPTH, RET_HEADS), 0.05)),
        'w_out': nrm(ks[14], (DEPTH, MIX_WIDTH, D_MODEL), MIX_WIDTH ** -0.5),
        'w_gate': nrm(ks[15], (DEPTH, D_MODEL, D_FF), D_MODEL ** -0.5),
        'w_up': nrm(ks[16], (DEPTH, D_MODEL, D_FF), D_MODEL ** -0.5),
        'w_down': nrm(ks[17], (DEPTH, D_FF, D_MODEL), D_FF ** -0.5),
    }


def reference(x, c, ctx, c_ctx, ada_w, ada_b, norm_pre_mix, norm_post_mix, norm_pre_ffn,
              norm_post_ffn, w_in, na_rpb, ret_log_gamma_fwd, ret_log_gamma_bwd, w_out,
              w_gate, w_up, w_down):
    B, L, _ = x.shape
    rows = L // GRID_W
    t = jnp.arange(L)
    pos_row, pos_col = t // GRID_W, t % GRID_W
    flip = lambda a: jnp.flip(a, axis=2)
    grid = lambda a: a.reshape(B, NA_HEADS, rows, GRID_W, NA_HEAD_DIM)
    zero_state = jnp.zeros((B, RET_HEADS, RET_QK_DIM, RET_V_DIM), jnp.float32)

    for li in range(DEPTH):
        sh1, sc1, g1, sh2, sc2, g2 = [m[:, None, :] for m in adaln(c, ada_w[li], ada_b[li])]
        csh1, csc1, cg1, csh2, csc2, cg2 = adaln(c_ctx, ada_w[li], ada_b[li])

        h = modulate(x, norm_pre_mix[li], sh1, sc1)
        hc = modulate(ctx, norm_pre_mix[li], csh1, csc1)
        na_q, na_k, na_v, r_q, r_k, r_v, r_g = split_projection(h @ w_in[li])
        cna_q, cna_k, cna_v, cr_q, cr_k, cr_v, cr_g = split_projection(hc @ w_in[li])

        o_na = neighbourhood_attention(grid(na_q), grid(na_k), grid(na_v), cna_k, cna_v, na_rpb[li])

        o_cf, s_cf = retention_chunkwise(cr_q, cr_k, cr_v, ret_log_gamma_fwd[li], zero_state)
        o_cb, s_cb = retention_chunkwise(flip(cr_q), flip(cr_k), flip(cr_v), ret_log_gamma_bwd[li], zero_state)
        rq = axial_rope(r_q, pos_row, pos_col)
        rk = axial_rope(r_k, pos_row, pos_col)
        o_f, _ = retention_chunkwise(rq, rk, r_v, ret_log_gamma_fwd[li], s_cf)
        o_b, _ = retention_chunkwise(flip(rq), flip(rk), flip(r_v), ret_log_gamma_bwd[li], s_cb)
        o_ret = o_f + flip(o_b)

        mix = merge_mixers(o_na, o_ret, r_g, w_out[li])
        x_new = x + g1 * rms_norm(mix, norm_post_mix[li])
        h2 = modulate(x_new, norm_pre_ffn[li], sh2, sc2)
        x_new = x_new + g2 * rms_norm(swiglu(h2, w_gate[li], w_up[li], w_down[li]), norm_post_ffn[li])

        if li + 1 < DEPTH:
            o_na_c = context_attention(cna_q, cna_k, cna_v)
            o_ret_c = o_cf + flip(o_cb)
            mix_c = merge_mixers(o_na_c, o_ret_c, cr_g, w_out[li])
            ctx = ctx + cg1 * rms_norm(mix_c, norm_post_mix[li])
            hc2 = modulate(ctx, norm_pre_ffn[li], csh2, csc2)
            ctx = ctx + cg2 * rms_norm(swiglu(hc2, w_gate[li], w_up[li], w_down[li]), norm_post_ffn[li])
        x = x_new
    return x
```

```python
import functools

import numpy as np
import jax
import jax.numpy as jnp
from jax import lax
from jax.experimental import pallas as pl
from jax.experimental.pallas import tpu as pltpu

F32 = jnp.float32
BF16 = jnp.bfloat16

D_MODEL = 2048
GRID_W = 64
NA_HEADS = 8
NA_HEAD_DIM = 128
NA_WIN_H = 8
NA_WIN_W = 16
RET_HEADS = 8
RET_QK_DIM = 64
RET_V_DIM = 128
NA_WIDTH = NA_HEADS * NA_HEAD_DIM
RET_QK_WIDTH = RET_HEADS * RET_QK_DIM
RET_V_WIDTH = RET_HEADS * RET_V_DIM
IN_WIDTH = 3 * NA_WIDTH + 2 * RET_QK_WIDTH + 2 * RET_V_WIDTH
ROPE_BASE = 10000.0
EPS = 1e-6
MASK_VALUE = -1e30

OFF_NA_Q = 0
OFF_NA_K = NA_WIDTH
OFF_NA_V = 2 * NA_WIDTH
OFF_R_Q = 3 * NA_WIDTH
OFF_R_K = OFF_R_Q + RET_QK_WIDTH
OFF_R_V = OFF_R_K + RET_QK_WIDTH
OFF_R_G = OFF_R_V + RET_V_WIDTH

VMEM_LIMIT_BYTES = 56 * 1024 * 1024

RET_CHUNK = 256


def _params(*semantics):
    return pltpu.CompilerParams(dimension_semantics=semantics, vmem_limit_bytes=VMEM_LIMIT_BYTES)


def _dot(a, b):
    return jnp.dot(a, b, preferred_element_type=F32)


def _dot_nt(a, b):
    return lax.dot_general(a, b, (((1,), (1,)), ((), ())), preferred_element_type=F32)


def _silu(x):
    return x * jax.nn.sigmoid(x)


def _adaln_kernel(c_ref, w_ref, b_ref, o_ref):
    s = _silu(c_ref[...]).astype(BF16)
    o_ref[...] = _dot(s, w_ref[...].astype(BF16)) + b_ref[...]


def _adaln(cc, ada_w, ada_b, tn=1024):
    rows, d = cc.shape
    n = ada_w.shape[1]
    return pl.pallas_call(
        _adaln_kernel,
        grid=(n // tn,),
        in_specs=[
            pl.BlockSpec((rows, d), lambda j: (0, 0)),
            pl.BlockSpec((d, tn), lambda j: (0, j)),
            pl.BlockSpec((1, tn), lambda j: (0, j)),
        ],
        out_specs=pl.BlockSpec((rows, tn), lambda j: (0, j)),
        out_shape=jax.ShapeDtypeStruct((rows, n), F32),
        compiler_params=_params("arbitrary"),
        name="adaln",
    )(cc, ada_w, ada_b.reshape(1, n))


def _inproj_kernel(x_ref, g_ref, sh_ref, sc_ref, w_ref, o_ref, h_ref):
    @pl.when(pl.program_id(1) == 0)
    def _():
        x = x_ref[...]
        ms = jnp.mean(x * x, axis=-1, keepdims=True)
        y = x * lax.rsqrt(ms + EPS) * g_ref[...]
        h_ref[...] = (y * (1.0 + sc_ref[...]) + sh_ref[...]).astype(BF16)

    o_ref[...] = _dot(h_ref[...], w_ref[...]).astype(BF16)


def _inproj(x2, gain, mod3, w_bf16, *, rows_per_mod, mod_row0, col_tiles, tm, tn):
    m, d = x2.shape
    n_out = len(col_tiles) * tn
    first = col_tiles[0]
    gap_at = next((i for i in range(1, len(col_tiles)) if col_tiles[i] != col_tiles[i - 1] + 1), None)
    if gap_at is None:
        w_map = lambda i, j: (0, j + first)
    else:
        gap = col_tiles[gap_at] - col_tiles[gap_at - 1] - 1
        assert all(col_tiles[i] == first + i + (gap if i >= gap_at else 0) for i in range(len(col_tiles)))
        w_map = lambda i, j: (0, j + first + jnp.where(j >= gap_at, gap, 0))
    mod_row = lambda i: mod_row0 + (i * tm) // rows_per_mod
    return pl.pallas_call(
        _inproj_kernel,
        grid=(m // tm, len(col_tiles)),
        in_specs=[
            pl.BlockSpec((tm, d), lambda i, j: (i, 0)),
            pl.BlockSpec((1, d), lambda i, j: (0, 0)),
            pl.BlockSpec((None, 1, d), lambda i, j: (mod_row(i), 0, 0)),
            pl.BlockSpec((None, 1, d), lambda i, j: (mod_row(i), 0, 1)),
            pl.BlockSpec((d, tn), w_map),
        ],
        out_specs=pl.BlockSpec((tm, tn), lambda i, j: (i, j)),
        out_shape=jax.ShapeDtypeStruct((m, n_out), BF16),
        scratch_shapes=[pltpu.VMEM((tm, d), BF16)],
        compiler_params=_params("parallel", "arbitrary"),
        name="inproj",
    )(x2, gain, mod3, mod3, w_bf16)


def _na_kernel(q_ref, k_ref, v_ref, kc_ref, vc_ref, bias_ref, o_ref, qs_ref, *, rows):
    scale = NA_HEAD_DIM ** -0.5
    qs_ref[...] = (q_ref[...].astype(F32) * scale).astype(BF16)
    kc = kc_ref[...]
    vc = vc_ref[...]
    band = NA_WIN_H * GRID_W

    def row_body(r, carry):
        rs = jnp.minimum(jnp.maximum(r - NA_WIN_H // 2, 0), rows - NA_WIN_H)
        q0 = pl.multiple_of(r * GRID_W, GRID_W)
        k0 = pl.multiple_of(rs * GRID_W, GRID_W)
        q = qs_ref[pl.ds(q0, GRID_W), :]
        kb = k_ref[pl.ds(k0, band), :]
        vb = v_ref[pl.ds(k0, band), :]
        s_loc = _dot_nt(q, kb) + bias_ref[rs - r + NA_WIN_H - 1]
        s_ctx = _dot_nt(q, kc)
        m = jnp.maximum(jnp.max(s_loc, axis=-1, keepdims=True), jnp.max(s_ctx, axis=-1, keepdims=True))
        p_loc = jnp.exp(s_loc - m)
        p_ctx = jnp.exp(s_ctx - m)
        l = jnp.sum(p_loc, axis=-1, keepdims=True) + jnp.sum(p_ctx, axis=-1, keepdims=True)
        o = _dot(p_loc.astype(BF16), vb) + _dot(p_ctx.astype(BF16), vc)
        o_ref[pl.ds(q0, GRID_W), :] = (o / l).astype(BF16)
        return carry

    lax.fori_loop(0, rows, row_body, 0)


def _na_bias_table(rpb):
    cols = np.arange(GRID_W)
    col_start = np.clip(cols - NA_WIN_W // 2, 0, GRID_W - NA_WIN_W)
    kc = np.arange(GRID_W)
    valid = (kc[None, :] >= col_start[:, None]) & (kc[None, :] < col_start[:, None] + NA_WIN_W)
    idx = np.clip(kc[None, :] - cols[:, None] + NA_WIN_W - 1, 0, 2 * NA_WIN_W - 2)
    t = jnp.where(valid[None, None], rpb.astype(F32)[:, :, idx], MASK_VALUE)
    b8 = jnp.stack([t[:, s:s + NA_WIN_H] for s in range(NA_WIN_H)], axis=1)
    return b8.transpose(0, 1, 3, 2, 4).reshape(NA_HEADS, NA_WIN_H, GRID_W, NA_WIN_H * GRID_W)


def _neighbourhood_attention(p, pc, pc_off_k, pc_off_v, bias8):
    b, l, _ = p.shape
    lc = pc.shape[1]
    rows = l // GRID_W
    hd = NA_HEAD_DIM
    blk = lambda off: (lambda bi, h: (bi, 0, off // hd + h))
    return pl.pallas_call(
        functools.partial(_na_kernel, rows=rows),
        grid=(b, NA_HEADS),
        in_specs=[
            pl.BlockSpec((None, l, hd), blk(OFF_NA_Q)),
            pl.BlockSpec((None, l, hd), blk(OFF_NA_K)),
            pl.BlockSpec((None, l, hd), blk(OFF_NA_V)),
            pl.BlockSpec((None, lc, hd), blk(pc_off_k)),
            pl.BlockSpec((None, lc, hd), blk(pc_off_v)),
            pl.BlockSpec((None, NA_WIN_H, GRID_W, NA_WIN_H * GRID_W), lambda bi, h: (h, 0, 0, 0)),
        ],
        out_specs=pl.BlockSpec((None, l, hd), lambda bi, h: (bi, 0, h)),
        out_shape=jax.ShapeDtypeStruct((b, l, NA_WIDTH), BF16),
        scratch_shapes=[pltpu.VMEM((l, hd), BF16)],
        compiler_params=_params("parallel", "parallel"),
        name="na",
    )(p, p, p, pc, pc, bias8)


def _rope_tables(l):
    t = np.arange(l)
    pos_row, pos_col = t // GRID_W, t % GRID_W
    quarter = RET_QK_DIM // 4
    inv_freq = ROPE_BASE ** (-np.arange(quarter, dtype=np.float64) / quarter)
    ang_r = pos_row[:, None] * inv_freq
    ang_c = pos_col[:, None] * inv_freq
    cos = np.concatenate([np.cos(ang_r), np.cos(ang_r), np.cos(ang_c), np.cos(ang_c)], axis=-1)
    sin = np.concatenate([-np.sin(ang_r), np.sin(ang_r), -np.sin(ang_c), np.sin(ang_c)], axis=-1)
    reps = 128 // RET_QK_DIM
    return (jnp.asarray(np.tile(cos, (1, reps)), F32), jnp.asarray(np.tile(sin, (1, reps)), F32))


def _ret_kernel(lgf_ref, lgb_ref, q_ref, k_ref, v_ref, g_ref, kc_ref, vc_ref, cos_ref, sin_ref,
                o_ref, qr_ref, kr_ref, sf_ref, sb_ref, *, n_chunks):
    C = RET_CHUNK
    hp = pl.program_id(1)
    lane = lax.broadcasted_iota(jnp.int32, (1, 128), 1)
    first_half = (lane % (RET_QK_DIM // 2)) < (RET_QK_DIM // 4)

    def rope(x):
        partner = jnp.where(first_half, pltpu.roll(x, 128 - RET_QK_DIM // 4, 1), pltpu.roll(x, RET_QK_DIM // 4, 1))
        return x * cos_ref[...] + partner * sin_ref[...]

    qr_ref[...] = rope(q_ref[...].astype(F32)) * (RET_QK_DIM ** -0.5)
    kr_ref[...] = rope(k_ref[...].astype(F32))

    row = lax.broadcasted_iota(jnp.int32, (C, 128), 0).astype(F32)
    ri = lax.broadcasted_iota(jnp.int32, (C, C), 0)
    ci = lax.broadcasted_iota(jnp.int32, (C, C), 1)
    diff = (ri - ci).astype(F32)

    for e in range(2):
        lgf = lgf_ref[2 * hp + e]
        lgb = lgb_ref[2 * hp + e]
        head_lanes = (lane // RET_QK_DIM) == e
        vsl = slice(e * RET_V_DIM, (e + 1) * RET_V_DIM)

        dkf = jnp.exp(lgf * (C - 1.0 - row))
        dkb = jnp.exp(lgb * row)
        dqf = jnp.exp(lgf * (row + 1.0))
        dqb = jnp.exp(lgb * (C - row))
        cdf = jnp.exp(jnp.full((1, RET_V_DIM), lgf * C, F32))
        cdb = jnp.exp(jnp.full((1, RET_V_DIM), lgb * C, F32))
        d_bi = (jnp.where(diff >= 0, jnp.exp(lgf * jnp.maximum(diff, 0.0)), 0.0)
                + jnp.where(diff <= 0, jnp.exp(lgb * jnp.maximum(-diff, 0.0)), 0.0))

        def local_state(kf32, v_bf16, dk):
            return _dot((kf32 * dk).T.astype(BF16), v_bf16)

        kc = kc_ref[...].astype(F32)
        vc = vc_ref[:, vsl]
        s_f = local_state(kc, vc, dkf)
        s_b = local_state(kc, vc, dkb)

        for i in range(n_chunks):
            sf_ref[i] = s_f
            rows = pl.ds(i * C, C)
            s_f = s_f * cdf + local_state(kr_ref[rows, :], v_ref[rows, vsl], dkf)
        for i in reversed(range(n_chunks)):
            sb_ref[i] = s_b
            rows = pl.ds(i * C, C)
            s_b = s_b * cdb + local_state(kr_ref[rows, :], v_ref[rows, vsl], dkb)

        def chunk_body(i, carry):
            r0 = pl.multiple_of(i * C, C)
            rows = pl.ds(r0, C)
            qm = jnp.where(head_lanes, qr_ref[rows, :], 0.0)
            kk = kr_ref[rows, :].astype(BF16)
            vv = v_ref[rows, vsl]
            inner = _dot_nt(qm.astype(BF16), kk) * d_bi
            o = (_dot(inner.astype(BF16), vv)
                 + _dot((qm * dqf).astype(BF16), sf_ref[i].astype(BF16))
                 + _dot((qm * dqb).astype(BF16), sb_ref[i].astype(BF16)))
            mu = jnp.mean(o, axis=-1, keepdims=True)
            oc = o - mu
            var = jnp.mean(oc * oc, axis=-1, keepdims=True)
            gate = _silu(g_ref[rows, vsl].astype(F32))
            o_ref[rows, vsl] = (oc * lax.rsqrt(var + EPS) * gate).astype(BF16)
            return carry

        lax.fori_loop(0, n_chunks, chunk_body, 0)


def _retention(p, pc, pc_off_k, pc_off_v, lg_f, lg_b):
    b, l, _ = p.shape
    lc = pc.shape[1]
    assert lc == RET_CHUNK and l % RET_CHUNK == 0
    n_chunks = l // RET_CHUNK
    cos, sin = _rope_tables(l)
    smem = pl.BlockSpec(memory_space=pltpu.SMEM)
    qk_blk = lambda off: (lambda bi, hp: (bi, 0, off // 128 + hp))
    v_blk = lambda off: (lambda bi, hp: (bi, 0, off // 256 + hp))
    return pl.pallas_call(
        functools.partial(_ret_kernel, n_chunks=n_chunks),
        grid=(b, RET_HEADS // 2),
        in_specs=[
            smem, smem,
            pl.BlockSpec((None, l, 128), qk_blk(OFF_R_Q)),
            pl.BlockSpec((None, l, 128), qk_blk(OFF_R_K)),
            pl.BlockSpec((None, l, 256), v_blk(OFF_R_V)),
            pl.BlockSpec((None, l, 256), v_blk(OFF_R_G)),
            pl.BlockSpec((None, lc, 128), qk_blk(pc_off_k)),
            pl.BlockSpec((None, lc, 256), v_blk(pc_off_v)),
            pl.BlockSpec((l, 128), lambda bi, hp: (0, 0)),
            pl.BlockSpec((l, 128), lambda bi, hp: (0, 0)),
        ],
        out_specs=pl.BlockSpec((None, l, 256), lambda bi, hp: (bi, 0, hp)),
        out_shape=jax.ShapeDtypeStruct((b, l, RET_V_WIDTH), BF16),
        scratch_shapes=[
            pltpu.VMEM((l, 128), F32),
            pltpu.VMEM((l, 128), F32),
            pltpu.VMEM((n_chunks, 128, RET_V_DIM), F32),
            pltpu.VMEM((n_chunks, 128, RET_V_DIM), F32),
        ],
        compiler_params=_params("parallel", "parallel"),
        name="ret",
    )(lg_f, lg_b, p, p, p, p, pc, pc, cos, sin)


def _outproj_kernel(na_ref, ret_ref, wt_ref, wb_ref, x_ref, g1_ref, sh2_ref, sc2_ref, npost_ref, npre_ref,
                    xn_ref, h2_ref):
    mix = _dot(na_ref[...], wt_ref[...]) + _dot(ret_ref[...], wb_ref[...])
    ms = jnp.mean(mix * mix, axis=-1, keepdims=True)
    xn = x_ref[...] + g1_ref[...] * (mix * lax.rsqrt(ms + EPS) * npost_ref[...])
    xn_ref[...] = xn
    ms2 = jnp.mean(xn * xn, axis=-1, keepdims=True)
    y = xn * lax.rsqrt(ms2 + EPS) * npre_ref[...]
    h2_ref[...] = (y * (1.0 + sc2_ref[...]) + sh2_ref[...]).astype(BF16)


def _outproj(na, ret, w_out_bf16, x2, mod3, norm_post, norm_pre, *, rows_per_mod, tm):
    m, d = x2.shape
    kh = na.shape[1]
    mod_row = lambda i: (i * tm) // rows_per_mod
    mod_blk = lambda k: pl.BlockSpec((None, 1, d), lambda i: (mod_row(i), 0, k))
    return pl.pallas_call(
        _outproj_kernel,
        grid=(m // tm,),
        in_specs=[
            pl.BlockSpec((tm, kh), lambda i: (i, 0)),
            pl.BlockSpec((tm, kh), lambda i: (i, 0)),
            pl.BlockSpec((kh, d), lambda i: (0, 0)),
            pl.BlockSpec((kh, d), lambda i: (1, 0)),
            pl.BlockSpec((tm, d), lambda i: (i, 0)),
            mod_blk(2), mod_blk(3), mod_blk(4),
            pl.BlockSpec((1, d), lambda i: (0, 0)),
            pl.BlockSpec((1, d), lambda i: (0, 0)),
        ],
        out_specs=[pl.BlockSpec((tm, d), lambda i: (i, 0)), pl.BlockSpec((tm, d), lambda i: (i, 0))],
        out_shape=[jax.ShapeDtypeStruct((m, d), F32), jax.ShapeDtypeStruct((m, d), BF16)],
        compiler_params=_params("parallel"),
        name="outproj",
    )(na, ret, w_out_bf16, w_out_bf16, x2, mod3, mod3, mod3, norm_post, norm_pre)


def _ffn_kernel(h_ref, wg_ref, wu_ref, wd_ref, xn_ref, g2_ref, npost_ref, o_ref, acc_ref):
    f = pl.program_id(1)
    h = h_ref[...]
    a = _dot(h, wg_ref[...])
    u = _dot(h, wu_ref[...])
    t = (_silu(a) * u).astype(BF16)
    part = _dot(t, wd_ref[...])

    @pl.when(f == 0)
    def _():
        acc_ref[...] = part

    @pl.when(f > 0)
    def _():
        acc_ref[...] += part

    @pl.when(f == pl.num_programs(1) - 1)
    def _():
        y = acc_ref[...]
        ms = jnp.mean(y * y, axis=-1, keepdims=True)
        o_ref[...] = xn_ref[...] + g2_ref[...] * (y * lax.rsqrt(ms + EPS) * npost_ref[...])


def _ffn(h2, wg, wu, wd, xn, mod3, norm_post, *, rows_per_mod, tm, tf):
    m, d = h2.shape
    dff = wg.shape[1]
    mod_row = lambda i: (i * tm) // rows_per_mod
    return pl.pallas_call(
        _ffn_kernel,
        grid=(m // tm, dff // tf),
        in_specs=[
            pl.BlockSpec((tm, d), lambda i, f: (i, 0)),
            pl.BlockSpec((d, tf), lambda i, f: (0, f)),
            pl.BlockSpec((d, tf), lambda i, f: (0, f)),
            pl.BlockSpec((tf, d), lambda i, f: (f, 0)),
            pl.BlockSpec((tm, d), lambda i, f: (i, 0)),
            pl.BlockSpec((None, 1, d), lambda i, f: (mod_row(i), 0, 5)),
            pl.BlockSpec((1, d), lambda i, f: (0, 0)),
        ],
        out_specs=pl.BlockSpec((tm, d), lambda i, f: (i, 0)),
        out_shape=jax.ShapeDtypeStruct((m, d), F32),
        scratch_shapes=[pltpu.VMEM((tm, d), F32)],
        compiler_params=_params("parallel", "arbitrary"),
        name="ffn",
    )(h2, wg, wu, wd, xn, mod3, norm_post)


def kernel(x, c, ctx, c_ctx, ada_w, ada_b, norm_pre_mix, norm_post_mix, norm_pre_ffn, norm_post_ffn, w_in,
           na_rpb, ret_log_gamma_fwd, ret_log_gamma_bwd, w_out, w_gate, w_up, w_down):
    b, l, d = x.shape
    lc = ctx.shape[1]
    depth = ada_w.shape[0]
    assert depth == 1, "the context-stream update between layers is not implemented"
    li = 0

    mod_rows = -(-(b + 1) // 8) * 8
    cc = jnp.concatenate([c, c_ctx[None, :], jnp.zeros((mod_rows - b - 1, d), F32)], axis=0)
    mod3 = _adaln(cc, ada_w[li], ada_b[li]).reshape(mod_rows, 1, 6 * d)

    w_in_b = w_in[li].astype(BF16)
    x2 = x.reshape(b * l, d)
    ctx2 = ctx.reshape(b * lc, d)
    gain_pre = norm_pre_mix[li].reshape(1, d)

    tn = 512
    p = _inproj(x2, gain_pre, mod3, w_in_b, rows_per_mod=l, mod_row0=0,
                col_tiles=tuple(range(IN_WIDTH // tn)), tm=1024, tn=tn)
    ctx_tiles = tuple(range(OFF_NA_K // tn, OFF_R_Q // tn)) + tuple(range(OFF_R_K // tn, OFF_R_G // tn))
    pc = _inproj(ctx2, gain_pre, mod3, w_in_b, rows_per_mod=b * lc, mod_row0=b,
                 col_tiles=ctx_tiles, tm=1024, tn=tn)
    p = p.reshape(b, l, IN_WIDTH)
    pc = pc.reshape(b, lc, len(ctx_tiles) * tn)
    pc_na_k, pc_na_v = 0, NA_WIDTH
    pc_r_k, pc_r_v = 2 * NA_WIDTH, 2 * NA_WIDTH + RET_QK_WIDTH

    o_na = _neighbourhood_attention(p, pc, pc_na_k, pc_na_v, _na_bias_table(na_rpb[li]))
    o_ret = _retention(p, pc, pc_r_k, pc_r_v, ret_log_gamma_fwd[li].astype(F32), ret_log_gamma_bwd[li].astype(F32))

    x_new, h2 = _outproj(o_na.reshape(b * l, NA_WIDTH), o_ret.reshape(b * l, RET_V_WIDTH), w_out[li].astype(BF16),
                         x2, mod3, norm_post_mix[li].reshape(1, d), norm_pre_ffn[li].reshape(1, d),
                         rows_per_mod=l, tm=512)
    out = _ffn(h2, w_gate[li].astype(BF16), w_up[li].astype(BF16), w_down[li].astype(BF16), x_new, mod3,
               norm_post_ffn[li].reshape(1, d), rows_per_mod=l, tm=512, tf=512)
    return out.reshape(b, l, d)
```

```python
import functools

import numpy as np
import jax
import jax.numpy as jnp
from jax import lax
from jax.experimental import pallas as pl
from jax.experimental.pallas import tpu as pltpu

F32 = jnp.float32
BF16 = jnp.bfloat16

D_MODEL = 2048
GRID_W = 64
NA_HEADS = 8
NA_HEAD_DIM = 128
NA_WIN_H = 8
NA_WIN_W = 16
RET_HEADS = 8
RET_QK_DIM = 64
RET_V_DIM = 128
NA_WIDTH = NA_HEADS * NA_HEAD_DIM
RET_QK_WIDTH = RET_HEADS * RET_QK_DIM
RET_V_WIDTH = RET_HEADS * RET_V_DIM
IN_WIDTH = 3 * NA_WIDTH + 2 * RET_QK_WIDTH + 2 * RET_V_WIDTH
ROPE_BASE = 10000.0
EPS = 1e-6
MASK_VALUE = -1e30

OFF_NA_Q = 0
OFF_NA_K = NA_WIDTH
OFF_NA_V = 2 * NA_WIDTH
OFF_R_Q = 3 * NA_WIDTH
OFF_R_K = OFF_R_Q + RET_QK_WIDTH
OFF_R_V = OFF_R_K + RET_QK_WIDTH
OFF_R_G = OFF_R_V + RET_V_WIDTH

VMEM_LIMIT_BYTES = 56 * 1024 * 1024

RET_CHUNK = 256


def _params(*semantics):
    return pltpu.CompilerParams(dimension_semantics=semantics, vmem_limit_bytes=VMEM_LIMIT_BYTES)


def _dot(a, b):
    return jnp.dot(a, b, preferred_element_type=F32)


def _dot_nt(a, b):
    return lax.dot_general(a, b, (((1,), (1,)), ((), ())), preferred_element_type=F32)


def _silu(x):
    return x * jax.nn.sigmoid(x)


def _adaln_kernel(c_ref, w_ref, b_ref, o_ref):
    s = _silu(c_ref[...]).astype(BF16)
    o_ref[...] = _dot(s, w_ref[...].astype(BF16)) + b_ref[...]


def _adaln(cc, ada_w, ada_b, tn=1024):
    rows, d = cc.shape
    n = ada_w.shape[1]
    return pl.pallas_call(
        _adaln_kernel,
        grid=(n // tn,),
        in_specs=[
            pl.BlockSpec((rows, d), lambda j: (0, 0)),
            pl.BlockSpec((d, tn), lambda j: (0, j)),
            pl.BlockSpec((1, tn), lambda j: (0, j)),
        ],
        out_specs=pl.BlockSpec((rows, tn), lambda j: (0, j)),
        out_shape=jax.ShapeDtypeStruct((rows, n), F32),
        compiler_params=_params("arbitrary"),
        name="adaln",
    )(cc, ada_w, ada_b.reshape(1, n))


INPROJ_SUB_ROWS = 256


def _inproj_kernel(x_ref, g_ref, sh_ref, sc_ref, w_ref, o_ref, h_ref):
    j = pl.program_id(1)

    @pl.when(j == 0)
    def _():
        gain = g_ref[...] * (1.0 + sc_ref[...])
        shift = sh_ref[...]
        for s in range(x_ref.shape[0] // INPROJ_SUB_ROWS):
            rows = pl.ds(s * INPROJ_SUB_ROWS, INPROJ_SUB_ROWS)
            x = x_ref[rows, :]
            ms = jnp.mean(x * x, axis=-1, keepdims=True)
            h = (x * lax.rsqrt(ms + EPS) * gain + shift).astype(BF16)
            h_ref[rows, :] = h
            o_ref[rows, :] = _dot(h, w_ref[...]).astype(BF16)

    @pl.when(j > 0)
    def _():
        o_ref[...] = _dot(h_ref[...], w_ref[...]).astype(BF16)


def _inproj(x2, gain, mod3, w_bf16, *, rows_per_mod, mod_row0, col_tiles, tm, tn):
    m, d = x2.shape
    n_out = len(col_tiles) * tn
    first = col_tiles[0]
    gap_at = next((i for i in range(1, len(col_tiles)) if col_tiles[i] != col_tiles[i - 1] + 1), None)
    if gap_at is None:
        w_map = lambda i, j: (0, j + first)
    else:
        gap = col_tiles[gap_at] - col_tiles[gap_at - 1] - 1
        assert all(col_tiles[i] == first + i + (gap if i >= gap_at else 0) for i in range(len(col_tiles)))
        w_map = lambda i, j: (0, j + first + jnp.where(j >= gap_at, gap, 0))
    mod_row = lambda i: mod_row0 + (i * tm) // rows_per_mod
    return pl.pallas_call(
        _inproj_kernel,
        grid=(m // tm, len(col_tiles)),
        in_specs=[
            pl.BlockSpec((tm, d), lambda i, j: (i, 0)),
            pl.BlockSpec((1, d), lambda i, j: (0, 0)),
            pl.BlockSpec((None, 1, d), lambda i, j: (mod_row(i), 0, 0)),
            pl.BlockSpec((None, 1, d), lambda i, j: (mod_row(i), 0, 1)),
            pl.BlockSpec((d, tn), w_map),
        ],
        out_specs=pl.BlockSpec((tm, tn), lambda i, j: (i, j)),
        out_shape=jax.ShapeDtypeStruct((m, n_out), BF16),
        scratch_shapes=[pltpu.VMEM((tm, d), BF16)],
        compiler_params=_params("parallel", "arbitrary"),
        name="inproj",
    )(x2, gain, mod3, mod3, w_bf16)


NA_GROUP_ROWS = 4
NA_BAND_ROWS = NA_GROUP_ROWS + NA_WIN_H
NA_Q = NA_GROUP_ROWS * GRID_W
NA_BAND = NA_BAND_ROWS * GRID_W


def _na_band_start(g, rows):
    lo = g * NA_GROUP_ROWS - NA_WIN_H // 2
    if isinstance(g, (int, np.integer)):
        return min(max(lo, 0), rows - NA_BAND_ROWS)
    return jnp.minimum(jnp.maximum(lo, 0), rows - NA_BAND_ROWS)


def _na_kernel(q_ref, k_ref, v_ref, kc_ref, vc_ref, bias_ref, o_ref, qs_ref, *, rows):
    scale = NA_HEAD_DIM ** -0.5
    qs_ref[...] = (q_ref[...].astype(F32) * scale).astype(BF16)
    kc = kc_ref[...]
    vc = vc_ref[...]
    groups = rows // NA_GROUP_ROWS

    def group_body(g, carry):
        q0 = pl.multiple_of(g * NA_Q, NA_Q)
        k0 = pl.multiple_of(_na_band_start(g, rows) * GRID_W, GRID_W)
        pattern = jnp.where(g == 0, 0, jnp.where(g == groups - 1, 2, 1))
        q = qs_ref[pl.ds(q0, NA_Q), :]
        kb = k_ref[pl.ds(k0, NA_BAND), :]
        vb = v_ref[pl.ds(k0, NA_BAND), :]
        s_loc = _dot_nt(q, kb) + bias_ref[pattern]
        s_ctx = _dot_nt(q, kc)
        m = jnp.maximum(jnp.max(s_loc, axis=-1, keepdims=True), jnp.max(s_ctx, axis=-1, keepdims=True))
        p_loc = jnp.exp(s_loc - m)
        p_ctx = jnp.exp(s_ctx - m)
        l = jnp.sum(p_loc, axis=-1, keepdims=True) + jnp.sum(p_ctx, axis=-1, keepdims=True)
        o = _dot(p_loc.astype(BF16), vb) + _dot(p_ctx.astype(BF16), vc)
        o_ref[pl.ds(q0, NA_Q), :] = (o / l).astype(BF16)
        return carry

    lax.fori_loop(0, groups, group_body, 0, unroll=2)


def _na_bias_table(rpb, rows):
    groups = rows // NA_GROUP_ROWS
    assert rows % NA_GROUP_ROWS == 0 and rows >= NA_BAND_ROWS and groups >= 3
    cols = np.arange(GRID_W)
    col_start = np.clip(cols - NA_WIN_W // 2, 0, GRID_W - NA_WIN_W)
    kc = np.arange(GRID_W)
    col_ok = (kc[None, :] >= col_start[:, None]) & (kc[None, :] < col_start[:, None] + NA_WIN_W)
    dc_idx = np.clip(kc[None, :] - cols[:, None] + NA_WIN_W - 1, 0, 2 * NA_WIN_W - 2)

    def row_pattern(g):
        r = g * NA_GROUP_ROWS + np.arange(NA_GROUP_ROWS)[:, None]
        krow = _na_band_start(g, rows) + np.arange(NA_BAND_ROWS)[None, :]
        rs = np.clip(r - NA_WIN_H // 2, 0, rows - NA_WIN_H)
        ok = (krow >= rs) & (krow < rs + NA_WIN_H)
        return ok, np.clip(krow - r + NA_WIN_H - 1, 0, 2 * NA_WIN_H - 2)

    pats = [row_pattern(0), row_pattern(1), row_pattern(groups - 1)]
    for g in range(1, groups - 1):
        ok, dr = row_pattern(g)
        assert (ok == pats[1][0]).all() and (dr[ok] == pats[1][1][ok]).all()

    t = rpb.astype(F32)[:, :, dc_idx]
    out = []
    for ok, dr in pats:
        valid = ok[:, :, None, None] & col_ok[None, None]
        b = jnp.where(valid[None], t[:, dr], MASK_VALUE)
        out.append(b.transpose(0, 1, 3, 2, 4).reshape(NA_HEADS, NA_Q, NA_BAND))
    return jnp.stack(out, axis=1)


def _neighbourhood_attention(p, pc, pc_off_k, pc_off_v, bias):
    b, l, _ = p.shape
    lc = pc.shape[1]
    rows = l // GRID_W
    hd = NA_HEAD_DIM
    blk = lambda off: (lambda h, bi: (bi, 0, off // hd + h))
    return pl.pallas_call(
        functools.partial(_na_kernel, rows=rows),
        grid=(NA_HEADS, b),
        in_specs=[
            pl.BlockSpec((None, l, hd), blk(OFF_NA_Q)),
            pl.BlockSpec((None, l, hd), blk(OFF_NA_K)),
            pl.BlockSpec((None, l, hd), blk(OFF_NA_V)),
            pl.BlockSpec((None, lc, hd), blk(pc_off_k)),
            pl.BlockSpec((None, lc, hd), blk(pc_off_v)),
            pl.BlockSpec((None, 3, NA_Q, NA_BAND), lambda h, bi: (h, 0, 0, 0)),
        ],
        out_specs=pl.BlockSpec((None, l, hd), lambda h, bi: (bi, 0, h)),
        out_shape=jax.ShapeDtypeStruct((b, l, NA_WIDTH), BF16),
        scratch_shapes=[pltpu.VMEM((l, hd), BF16)],
        compiler_params=_params("parallel", "parallel"),
        name="na",
    )(p, p, p, pc, pc, bias)


def _rope_tables(l):
    t = np.arange(l)
    pos_row, pos_col = t // GRID_W, t % GRID_W
    quarter = RET_QK_DIM // 4
    inv_freq = ROPE_BASE ** (-np.arange(quarter, dtype=np.float64) / quarter)
    ang_r = pos_row[:, None] * inv_freq
    ang_c = pos_col[:, None] * inv_freq
    cos = np.concatenate([np.cos(ang_r), np.cos(ang_r), np.cos(ang_c), np.cos(ang_c)], axis=-1)
    sin = np.concatenate([-np.sin(ang_r), np.sin(ang_r), -np.sin(ang_c), np.sin(ang_c)], axis=-1)
    reps = 128 // RET_QK_DIM
    return (jnp.asarray(np.tile(cos, (1, reps)), F32), jnp.asarray(np.tile(sin, (1, reps)), F32))


def _ret_kernel(lgf_ref, lgb_ref, q_ref, k_ref, v_ref, g_ref, kc_ref, vc_ref, cos_ref, sin_ref,
                o_ref, qr_ref, kr_ref, kb_ref, dbi_ref, dq_ref, a_ref, sb_ref, st_ref, *, n_chunks):
    C = RET_CHUNK
    hp = pl.program_id(1)
    lane = lax.broadcasted_iota(jnp.int32, (1, 128), 1)
    sub = lax.broadcasted_iota(jnp.int32, (128, 1), 0)
    first_half = (lane % (RET_QK_DIM // 2)) < (RET_QK_DIM // 4)

    def rope(x):
        partner = jnp.where(first_half, pltpu.roll(x, 128 - RET_QK_DIM // 4, 1), pltpu.roll(x, RET_QK_DIM // 4, 1))
        return x * cos_ref[...] + partner * sin_ref[...]

    qr_ref[...] = rope(q_ref[...].astype(F32)) * (RET_QK_DIM ** -0.5)
    k_rot = rope(k_ref[...].astype(F32))
    kr_ref[...] = k_rot
    kb_ref[...] = k_rot.astype(BF16)

    row = lax.broadcasted_iota(jnp.int32, (C, 128), 0).astype(F32)
    ri = lax.broadcasted_iota(jnp.int32, (C, C), 0)
    ci = lax.broadcasted_iota(jnp.int32, (C, C), 1)
    diff = (ri - ci).astype(F32)

    for e in range(2):
        lgf = lgf_ref[2 * hp + e]
        lgb = lgb_ref[2 * hp + e]
        head_lanes = (lane // RET_QK_DIM) == e
        fwd_rows = (sub // RET_QK_DIM) == e
        vsl = slice(e * RET_V_DIM, (e + 1) * RET_V_DIM)

        dk = jnp.where(head_lanes, jnp.exp(lgf * (C - 1.0 - row)), jnp.exp(lgb * row))
        dq_ref[e] = jnp.where(head_lanes, jnp.exp(lgf * (row + 1.0)), jnp.exp(lgb * (C - row)))
        cdf = jnp.exp(jnp.full((1, RET_V_DIM), lgf * C, F32))
        cdb = jnp.exp(jnp.full((1, RET_V_DIM), lgb * C, F32))
        dbi_ref[e] = (jnp.where(diff >= 0, jnp.exp(lgf * jnp.maximum(diff, 0.0)), 0.0)
                      + jnp.where(diff <= 0, jnp.exp(lgb * jnp.maximum(-diff, 0.0)), 0.0))

        def local_state(kf32, v_bf16):
            k_both = jnp.where(head_lanes, kf32, pltpu.roll(kf32, RET_QK_DIM, 1))
            return _dot((k_both * dk).T.astype(BF16), v_bf16)

        ctx_state = local_state(kc_ref[...].astype(F32), vc_ref[:, vsl])
        for i in range(n_chunks):
            rows = pl.ds(i * C, C)
            a_ref[i] = local_state(kr_ref[rows, :], v_ref[rows, vsl])

        s_b = ctx_state
        for i in reversed(range(n_chunks)):
            sb_ref[i] = s_b
            s_b = s_b * cdb + a_ref[i]
        s_f = ctx_state
        for i in range(n_chunks):
            st_ref[e, i] = jnp.where(fwd_rows, s_f, sb_ref[i]).astype(BF16)
            s_f = s_f * cdf + a_ref[i]

    def chunk_body(i, carry):
        rows = pl.ds(pl.multiple_of(i * C, C), C)
        q = qr_ref[rows, :]
        q_rolled = pltpu.roll(q, RET_QK_DIM, 1)
        kk = kb_ref[rows, :]
        for e in range(2):
            head_lanes = (lane // RET_QK_DIM) == e
            vsl = slice(e * RET_V_DIM, (e + 1) * RET_V_DIM)
            q_own = jnp.where(head_lanes, q, 0.0).astype(BF16)
            q_both = (jnp.where(head_lanes, q, q_rolled) * dq_ref[e]).astype(BF16)
            inner = _dot_nt(q_own, kk) * dbi_ref[e]
            o = _dot(inner.astype(BF16), v_ref[rows, vsl]) + _dot(q_both, st_ref[e, i])
            mu = jnp.mean(o, axis=-1, keepdims=True)
            oc = o - mu
            var = jnp.mean(oc * oc, axis=-1, keepdims=True)
            gate = _silu(g_ref[rows, vsl].astype(F32))
            o_ref[rows, vsl] = (oc * lax.rsqrt(var + EPS) * gate).astype(BF16)
        return carry

    lax.fori_loop(0, n_chunks, chunk_body, 0)


def _retention(p, pc, pc_off_k, pc_off_v, lg_f, lg_b):
    b, l, _ = p.shape
    lc = pc.shape[1]
    assert lc == RET_CHUNK and l % RET_CHUNK == 0
    n_chunks = l // RET_CHUNK
    cos, sin = _rope_tables(l)
    smem = pl.BlockSpec(memory_space=pltpu.SMEM)
    qk_blk = lambda off: (lambda bi, hp: (bi, 0, off // 128 + hp))
    v_blk = lambda off: (lambda bi, hp: (bi, 0, off // 256 + hp))
    return pl.pallas_call(
        functools.partial(_ret_kernel, n_chunks=n_chunks),
        grid=(b, RET_HEADS // 2),
        in_specs=[
            smem, smem,
            pl.BlockSpec((None, l, 128), qk_blk(OFF_R_Q)),
            pl.BlockSpec((None, l, 128), qk_blk(OFF_R_K)),
            pl.BlockSpec((None, l, 256), v_blk(OFF_R_V)),
            pl.BlockSpec((None, l, 256), v_blk(OFF_R_G)),
            pl.BlockSpec((None, lc, 128), qk_blk(pc_off_k)),
            pl.BlockSpec((None, lc, 256), v_blk(pc_off_v)),
            pl.BlockSpec((l, 128), lambda bi, hp: (0, 0)),
            pl.BlockSpec((l, 128), lambda bi, hp: (0, 0)),
        ],
        out_specs=pl.BlockSpec((None, l, 256), lambda bi, hp: (bi, 0, hp)),
        out_shape=jax.ShapeDtypeStruct((b, l, RET_V_WIDTH), BF16),
        scratch_shapes=[
            pltpu.VMEM((l, 128), F32),
            pltpu.VMEM((l, 128), F32),
            pltpu.VMEM((l, 128), BF16),
            pltpu.VMEM((2, RET_CHUNK, RET_CHUNK), F32),
            pltpu.VMEM((2, RET_CHUNK, 128), F32),
            pltpu.VMEM((n_chunks, 128, RET_V_DIM), F32),
            pltpu.VMEM((n_chunks, 128, RET_V_DIM), F32),
            pltpu.VMEM((2, n_chunks, 128, RET_V_DIM), BF16),
        ],
        compiler_params=_params("parallel", "parallel"),
        name="ret",
    )(lg_f, lg_b, p, p, p, p, pc, pc, cos, sin)


OUTPROJ_SUB_ROWS = 256


def _outproj_kernel(na_ref, ret_ref, wt_ref, wb_ref, x_ref, g1_ref, sh2_ref, sc2_ref, npost_ref, npre_ref,
                    xn_ref, h2_ref):
    gate_gain = g1_ref[...] * npost_ref[...]
    gain2 = npre_ref[...] * (1.0 + sc2_ref[...])
    shift2 = sh2_ref[...]
    for s in range(x_ref.shape[0] // OUTPROJ_SUB_ROWS):
        rows = pl.ds(s * OUTPROJ_SUB_ROWS, OUTPROJ_SUB_ROWS)
        mix = _dot(na_ref[rows, :], wt_ref[...]) + _dot(ret_ref[rows, :], wb_ref[...])
        ms = jnp.mean(mix * mix, axis=-1, keepdims=True)
        xn = x_ref[rows, :] + mix * lax.rsqrt(ms + EPS) * gate_gain
        xn_ref[rows, :] = xn
        ms2 = jnp.mean(xn * xn, axis=-1, keepdims=True)
        h2_ref[rows, :] = (xn * lax.rsqrt(ms2 + EPS) * gain2 + shift2).astype(BF16)


def _outproj(na, ret, w_out_bf16, x2, mod3, norm_post, norm_pre, *, rows_per_mod, tm):
    m, d = x2.shape
    kh = na.shape[1]
    mod_row = lambda i: (i * tm) // rows_per_mod
    mod_blk = lambda k: pl.BlockSpec((None, 1, d), lambda i: (mod_row(i), 0, k))
    return pl.pallas_call(
        _outproj_kernel,
        grid=(m // tm,),
        in_specs=[
            pl.BlockSpec((tm, kh), lambda i: (i, 0)),
            pl.BlockSpec((tm, kh), lambda i: (i, 0)),
            pl.BlockSpec((kh, d), lambda i: (0, 0)),
            pl.BlockSpec((kh, d), lambda i: (1, 0)),
            pl.BlockSpec((tm, d), lambda i: (i, 0)),
            mod_blk(2), mod_blk(3), mod_blk(4),
            pl.BlockSpec((1, d), lambda i: (0, 0)),
            pl.BlockSpec((1, d), lambda i: (0, 0)),
        ],
        out_specs=[pl.BlockSpec((tm, d), lambda i: (i, 0)), pl.BlockSpec((tm, d), lambda i: (i, 0))],
        out_shape=[jax.ShapeDtypeStruct((m, d), F32), jax.ShapeDtypeStruct((m, d), BF16)],
        compiler_params=_params("parallel"),
        name="outproj",
    )(na, ret, w_out_bf16, w_out_bf16, x2, mod3, mod3, mod3, norm_post, norm_pre)


def _ffn_kernel(h_ref, wg_ref, wu_ref, wd_ref, xn_ref, g2_ref, npost_ref, o_ref, acc_ref):
    f = pl.program_id(1)

    @pl.when(f == 0)
    def _():
        acc_ref[...] = jnp.zeros_like(acc_ref)

    h = h_ref[...]
    a = _dot(h, wg_ref[...])
    u = _dot(h, wu_ref[...])
    t = (_silu(a) * u).astype(BF16)
    acc_ref[...] += _dot(t, wd_ref[...])

    @pl.when(f == pl.num_programs(1) - 1)
    def _():
        y = acc_ref[...]
        ms = jnp.mean(y * y, axis=-1, keepdims=True)
        o_ref[...] = xn_ref[...] + g2_ref[...] * (y * lax.rsqrt(ms + EPS) * npost_ref[...])


def _ffn(h2, wg, wu, wd, xn, mod3, norm_post, *, rows_per_mod, tm, tf):
    m, d = h2.shape
    dff = wg.shape[1]
    mod_row = lambda i: (i * tm) // rows_per_mod
    return pl.pallas_call(
        _ffn_kernel,
        grid=(m // tm, dff // tf),
        in_specs=[
            pl.BlockSpec((tm, d), lambda i, f: (i, 0)),
            pl.BlockSpec((d, tf), lambda i, f: (0, f)),
            pl.BlockSpec((d, tf), lambda i, f: (0, f)),
            pl.BlockSpec((tf, d), lambda i, f: (f, 0)),
            pl.BlockSpec((tm, d), lambda i, f: (i, 0)),
            pl.BlockSpec((None, 1, d), lambda i, f: (mod_row(i), 0, 5)),
            pl.BlockSpec((1, d), lambda i, f: (0, 0)),
        ],
        out_specs=pl.BlockSpec((tm, d), lambda i, f: (i, 0)),
        out_shape=jax.ShapeDtypeStruct((m, d), F32),
        scratch_shapes=[pltpu.VMEM((tm, d), F32)],
        compiler_params=_params("parallel", "arbitrary"),
        name="ffn",
    )(h2, wg, wu, wd, xn, mod3, norm_post)


def kernel(x, c, ctx, c_ctx, ada_w, ada_b, norm_pre_mix, norm_post_mix, norm_pre_ffn, norm_post_ffn, w_in,
           na_rpb, ret_log_gamma_fwd, ret_log_gamma_bwd, w_out, w_gate, w_up, w_down):
    b, l, d = x.shape
    lc = ctx.shape[1]
    depth = ada_w.shape[0]
    assert depth == 1, "the context-stream update between layers is not implemented"
    li = 0

    mod_rows = -(-(b + 1) // 8) * 8
    cc = jnp.concatenate([c, c_ctx[None, :], jnp.zeros((mod_rows - b - 1, d), F32)], axis=0)
    mod3 = _adaln(cc, ada_w[li], ada_b[li]).reshape(mod_rows, 1, 6 * d)

    w_in_b = w_in[li].astype(BF16)
    x2 = x.reshape(b * l, d)
    ctx2 = ctx.reshape(b * lc, d)
    gain_pre = norm_pre_mix[li].reshape(1, d)

    tn = 512
    p = _inproj(x2, gain_pre, mod3, w_in_b, rows_per_mod=l, mod_row0=0,
                col_tiles=tuple(range(IN_WIDTH // tn)), tm=1024, tn=tn)
    ctx_tiles = tuple(range(OFF_NA_K // tn, OFF_R_Q // tn)) + tuple(range(OFF_R_K // tn, OFF_R_G // tn))
    pc = _inproj(ctx2, gain_pre, mod3, w_in_b, rows_per_mod=b * lc, mod_row0=b,
                 col_tiles=ctx_tiles, tm=1024, tn=tn)
    p = p.reshape(b, l, IN_WIDTH)
    pc = pc.reshape(b, lc, len(ctx_tiles) * tn)
    pc_na_k, pc_na_v = 0, NA_WIDTH
    pc_r_k, pc_r_v = 2 * NA_WIDTH, 2 * NA_WIDTH + RET_QK_WIDTH

    o_na = _neighbourhood_attention(p, pc, pc_na_k, pc_na_v, _na_bias_table(na_rpb[li], l // GRID_W))
    o_ret = _retention(p, pc, pc_r_k, pc_r_v, ret_log_gamma_fwd[li].astype(F32), ret_log_gamma_bwd[li].astype(F32))

    x_new, h2 = _outproj(o_na.reshape(b * l, NA_WIDTH), o_ret.reshape(b * l, RET_V_WIDTH), w_out[li].astype(BF16),
                         x2, mod3, norm_post_mix[li].reshape(1, d), norm_pre_ffn[li].reshape(1, d),
                         rows_per_mod=l, tm=512)
    out = _ffn(h2, w_gate[li].astype(BF16), w_up[li].astype(BF16), w_down[li].astype(BF16), x_new, mod3,
               norm_post_ffn[li].reshape(1, d), rows_per_mod=l, tm=512, tf=512)
    return out.reshape(b, l, d)
```

```python
import functools

import numpy as np
import jax
import jax.numpy as jnp
from jax import lax
from jax.experimental import pallas as pl
from jax.experimental.pallas import tpu as pltpu

F32 = jnp.float32
BF16 = jnp.bfloat16

D_MODEL = 2048
GRID_W = 64
NA_HEADS = 8
NA_HEAD_DIM = 128
NA_WIN_H = 8
NA_WIN_W = 16
RET_HEADS = 8
RET_QK_DIM = 64
RET_V_DIM = 128
NA_WIDTH = NA_HEADS * NA_HEAD_DIM
RET_QK_WIDTH = RET_HEADS * RET_QK_DIM
RET_V_WIDTH = RET_HEADS * RET_V_DIM
IN_WIDTH = 3 * NA_WIDTH + 2 * RET_QK_WIDTH + 2 * RET_V_WIDTH
ROPE_BASE = 10000.0
EPS = 1e-6
MASK_VALUE = -1e30

OFF_NA_Q = 0
OFF_NA_K = NA_WIDTH
OFF_NA_V = 2 * NA_WIDTH
OFF_R_Q = 3 * NA_WIDTH
OFF_R_K = OFF_R_Q + RET_QK_WIDTH
OFF_R_V = OFF_R_K + RET_QK_WIDTH
OFF_R_G = OFF_R_V + RET_V_WIDTH

VMEM_LIMIT_BYTES = 56 * 1024 * 1024

RET_CHUNK = 256


def _params(*semantics):
    return pltpu.CompilerParams(dimension_semantics=semantics, vmem_limit_bytes=VMEM_LIMIT_BYTES)


def _dot(a, b):
    return jnp.dot(a, b, preferred_element_type=F32)


def _dot_nt(a, b):
    return lax.dot_general(a, b, (((1,), (1,)), ((), ())), preferred_element_type=F32)


def _silu(x):
    return x * jax.nn.sigmoid(x)


def _adaln_kernel(c_ref, w_ref, b_ref, o_ref):
    s = _silu(c_ref[...]).astype(BF16)
    o_ref[...] = _dot(s, w_ref[...].astype(BF16)) + b_ref[...]


def _adaln(cc, ada_w, ada_b, tn=1024):
    rows, d = cc.shape
    n = ada_w.shape[1]
    return pl.pallas_call(
        _adaln_kernel,
        grid=(n // tn,),
        in_specs=[
            pl.BlockSpec((rows, d), lambda j: (0, 0)),
            pl.BlockSpec((d, tn), lambda j: (0, j)),
            pl.BlockSpec((1, tn), lambda j: (0, j)),
        ],
        out_specs=pl.BlockSpec((rows, tn), lambda j: (0, j)),
        out_shape=jax.ShapeDtypeStruct((rows, n), F32),
        compiler_params=_params("arbitrary"),
        name="adaln",
    )(cc, ada_w, ada_b.reshape(1, n))


INPROJ_SUB_ROWS = 256


def _inproj_kernel(x_ref, g_ref, sh_ref, sc_ref, w_ref, o_ref, h_ref):
    j = pl.program_id(1)

    @pl.when(j == 0)
    def _():
        gain = g_ref[...] * (1.0 + sc_ref[...])
        shift = sh_ref[...]
        for s in range(x_ref.shape[0] // INPROJ_SUB_ROWS):
            rows = pl.ds(s * INPROJ_SUB_ROWS, INPROJ_SUB_ROWS)
            x = x_ref[rows, :]
            ms = jnp.mean(x * x, axis=-1, keepdims=True)
            h = (x * lax.rsqrt(ms + EPS) * gain + shift).astype(BF16)
            h_ref[rows, :] = h
            o_ref[rows, :] = _dot(h, w_ref[...]).astype(BF16)

    @pl.when(j > 0)
    def _():
        o_ref[...] = _dot(h_ref[...], w_ref[...]).astype(BF16)


def _inproj(x2, gain, mod3, w_bf16, *, rows_per_mod, mod_row0, col_tiles, tm, tn):
    m, d = x2.shape
    n_out = len(col_tiles) * tn
    first = col_tiles[0]
    gap_at = next((i for i in range(1, len(col_tiles)) if col_tiles[i] != col_tiles[i - 1] + 1), None)
    if gap_at is None:
        w_map = lambda i, j: (0, j + first)
    else:
        gap = col_tiles[gap_at] - col_tiles[gap_at - 1] - 1
        assert all(col_tiles[i] == first + i + (gap if i >= gap_at else 0) for i in range(len(col_tiles)))
        w_map = lambda i, j: (0, j + first + jnp.where(j >= gap_at, gap, 0))
    mod_row = lambda i: mod_row0 + (i * tm) // rows_per_mod
    return pl.pallas_call(
        _inproj_kernel,
        grid=(m // tm, len(col_tiles)),
        in_specs=[
            pl.BlockSpec((tm, d), lambda i, j: (i, 0)),
            pl.BlockSpec((1, d), lambda i, j: (0, 0)),
            pl.BlockSpec((None, 1, d), lambda i, j: (mod_row(i), 0, 0)),
            pl.BlockSpec((None, 1, d), lambda i, j: (mod_row(i), 0, 1)),
            pl.BlockSpec((d, tn), w_map),
        ],
        out_specs=pl.BlockSpec((tm, tn), lambda i, j: (i, j)),
        out_shape=jax.ShapeDtypeStruct((m, n_out), BF16),
        scratch_shapes=[pltpu.VMEM((tm, d), BF16)],
        compiler_params=_params("parallel", "arbitrary"),
        name="inproj",
    )(x2, gain, mod3, mod3, w_bf16)


NA_GROUP_ROWS = 4
NA_BAND_ROWS = NA_GROUP_ROWS + NA_WIN_H
NA_Q = NA_GROUP_ROWS * GRID_W
NA_BAND = NA_BAND_ROWS * GRID_W


def _na_band_start(g, rows):
    lo = g * NA_GROUP_ROWS - NA_WIN_H // 2
    if isinstance(g, (int, np.integer)):
        return min(max(lo, 0), rows - NA_BAND_ROWS)
    return jnp.minimum(jnp.maximum(lo, 0), rows - NA_BAND_ROWS)


def _na_kernel(q_ref, k_ref, v_ref, kc_ref, vc_ref, bias_ref, o_ref, qs_ref, s0_ref, s1_ref, *, rows):
    scale = NA_HEAD_DIM ** -0.5
    qs_ref[...] = (q_ref[...].astype(F32) * scale).astype(BF16)
    groups = rows // NA_GROUP_ROWS
    lc = kc_ref.shape[0]

    def scores(g, s_ref):
        q0 = g * NA_Q
        k0 = _na_band_start(g, rows) * GRID_W
        pattern = 0 if g == 0 else (2 if g == groups - 1 else 1)
        q = qs_ref[pl.ds(q0, NA_Q), :]
        s_ref[:, :NA_BAND] = _dot_nt(q, k_ref[pl.ds(k0, NA_BAND), :]) + bias_ref[pattern]
        s_ref[:, NA_BAND:] = _dot_nt(q, kc_ref[...])

    def attend(g, s_ref):
        q0 = g * NA_Q
        k0 = _na_band_start(g, rows) * GRID_W
        s = s_ref[...]
        m = jnp.max(s, axis=-1, keepdims=True)
        p = jnp.exp(s - m)
        l = jnp.sum(p, axis=-1, keepdims=True)
        pb = p.astype(BF16)
        o = _dot(pb[:, :NA_BAND], v_ref[pl.ds(k0, NA_BAND), :]) + _dot(pb[:, NA_BAND:], vc_ref[...])
        o_ref[pl.ds(q0, NA_Q), :] = (o / l).astype(BF16)

    s_refs = (s0_ref, s1_ref)
    scores(0, s_refs[0])
    for g in range(groups):
        if g + 1 < groups:
            scores(g + 1, s_refs[(g + 1) % 2])
        attend(g, s_refs[g % 2])


def _na_bias_table(rpb, rows):
    groups = rows // NA_GROUP_ROWS
    assert rows % NA_GROUP_ROWS == 0 and rows >= NA_BAND_ROWS and groups >= 3
    cols = np.arange(GRID_W)
    col_start = np.clip(cols - NA_WIN_W // 2, 0, GRID_W - NA_WIN_W)
    kc = np.arange(GRID_W)
    col_ok = (kc[None, :] >= col_start[:, None]) & (kc[None, :] < col_start[:, None] + NA_WIN_W)
    dc_idx = np.clip(kc[None, :] - cols[:, None] + NA_WIN_W - 1, 0, 2 * NA_WIN_W - 2)

    def row_pattern(g):
        r = g * NA_GROUP_ROWS + np.arange(NA_GROUP_ROWS)[:, None]
        krow = _na_band_start(g, rows) + np.arange(NA_BAND_ROWS)[None, :]
        rs = np.clip(r - NA_WIN_H // 2, 0, rows - NA_WIN_H)
        ok = (krow >= rs) & (krow < rs + NA_WIN_H)
        return ok, np.clip(krow - r + NA_WIN_H - 1, 0, 2 * NA_WIN_H - 2)

    pats = [row_pattern(0), row_pattern(1), row_pattern(groups - 1)]
    for g in range(1, groups - 1):
        ok, dr = row_pattern(g)
        assert (ok == pats[1][0]).all() and (dr[ok] == pats[1][1][ok]).all()

    n_dr, n_dc = 2 * NA_WIN_H - 1, 2 * NA_WIN_W - 1
    col_sel = np.zeros((n_dc, GRID_W, GRID_W), np.float32)
    cc, kk = np.nonzero(col_ok)
    col_sel[dc_idx[cc, kk], cc, kk] = 1.0
    row_sel = np.zeros((len(pats), NA_GROUP_ROWS, NA_BAND_ROWS, n_dr), np.float32)
    mask = np.empty((len(pats), NA_GROUP_ROWS, GRID_W, NA_BAND_ROWS, GRID_W), np.float32)
    for pi, (ok, dr) in enumerate(pats):
        aa, jj = np.nonzero(ok)
        row_sel[pi, aa, jj, dr[aa, jj]] = 1.0
        mask[pi] = np.where(ok[:, None, :, None] & col_ok[None, :, None, :], 0.0, MASK_VALUE)
    hi = lax.Precision.HIGHEST
    t = jnp.einsum('hdx,xck->hdck', rpb.astype(F32), col_sel, precision=hi)
    b = jnp.einsum('pajd,hdck->hpacjk', row_sel, t, precision=hi) + mask[None]
    return b.reshape(NA_HEADS, len(pats), NA_Q, NA_BAND)


def _neighbourhood_attention(p, pc, pc_off_k, pc_off_v, bias):
    b, l, _ = p.shape
    lc = pc.shape[1]
    rows = l // GRID_W
    hd = NA_HEAD_DIM
    blk = lambda off: (lambda h, bi: (bi, 0, off // hd + h))
    return pl.pallas_call(
        functools.partial(_na_kernel, rows=rows),
        grid=(NA_HEADS, b),
        in_specs=[
            pl.BlockSpec((None, l, hd), blk(OFF_NA_Q)),
            pl.BlockSpec((None, l, hd), blk(OFF_NA_K)),
            pl.BlockSpec((None, l, hd), blk(OFF_NA_V)),
            pl.BlockSpec((None, lc, hd), blk(pc_off_k)),
            pl.BlockSpec((None, lc, hd), blk(pc_off_v)),
            pl.BlockSpec((None, 3, NA_Q, NA_BAND), lambda h, bi: (h, 0, 0, 0)),
        ],
        out_specs=pl.BlockSpec((None, l, hd), lambda h, bi: (bi, 0, h)),
        out_shape=jax.ShapeDtypeStruct((b, l, NA_WIDTH), BF16),
        scratch_shapes=[pltpu.VMEM((l, hd), BF16),
                        pltpu.VMEM((NA_Q, NA_BAND + lc), F32),
                        pltpu.VMEM((NA_Q, NA_BAND + lc), F32)],
        compiler_params=_params("parallel", "parallel"),
        name="na",
    )(p, p, p, pc, pc, bias)


def _rope_tables(l):
    t = np.arange(l)
    pos_row, pos_col = t // GRID_W, t % GRID_W
    quarter = RET_QK_DIM // 4
    inv_freq = ROPE_BASE ** (-np.arange(quarter, dtype=np.float64) / quarter)
    ang_r = pos_row[:, None] * inv_freq
    ang_c = pos_col[:, None] * inv_freq
    cos = np.concatenate([np.cos(ang_r), np.cos(ang_r), np.cos(ang_c), np.cos(ang_c)], axis=-1)
    sin = np.concatenate([-np.sin(ang_r), np.sin(ang_r), -np.sin(ang_c), np.sin(ang_c)], axis=-1)
    reps = 128 // RET_QK_DIM
    return (jnp.asarray(np.tile(cos, (1, reps)), F32), jnp.asarray(np.tile(sin, (1, reps)), F32))


def _ret_kernel(lgf_ref, lgb_ref, q_ref, k_ref, v_ref, g_ref, kc_ref, vc_ref, cos_ref, sin_ref,
                o_ref, qr_ref, kr_ref, kb_ref, dbi_ref, dq_ref, a_ref, sb_ref, st_ref, *, n_chunks):
    C = RET_CHUNK
    hp = pl.program_id(1)
    lane = lax.broadcasted_iota(jnp.int32, (1, 128), 1)
    sub = lax.broadcasted_iota(jnp.int32, (128, 1), 0)
    first_half = (lane % (RET_QK_DIM // 2)) < (RET_QK_DIM // 4)

    def rope(x):
        partner = jnp.where(first_half, pltpu.roll(x, 128 - RET_QK_DIM // 4, 1), pltpu.roll(x, RET_QK_DIM // 4, 1))
        return x * cos_ref[...] + partner * sin_ref[...]

    qr_ref[...] = rope(q_ref[...].astype(F32)) * (RET_QK_DIM ** -0.5)
    k_rot = rope(k_ref[...].astype(F32))
    kr_ref[...] = k_rot
    kb_ref[...] = k_rot.astype(BF16)

    row = lax.broadcasted_iota(jnp.int32, (C, 128), 0).astype(F32)
    ri = lax.broadcasted_iota(jnp.int32, (C, C), 0)
    ci = lax.broadcasted_iota(jnp.int32, (C, C), 1)
    diff = (ri - ci).astype(F32)

    for e in range(2):
        lgf = lgf_ref[2 * hp + e]
        lgb = lgb_ref[2 * hp + e]
        head_lanes = (lane // RET_QK_DIM) == e
        fwd_rows = (sub // RET_QK_DIM) == e
        vsl = slice(e * RET_V_DIM, (e + 1) * RET_V_DIM)

        dk = jnp.where(head_lanes, jnp.exp(lgf * (C - 1.0 - row)), jnp.exp(lgb * row))
        dq_ref[e] = jnp.where(head_lanes, jnp.exp(lgf * (row + 1.0)), jnp.exp(lgb * (C - row)))
        cdf = jnp.exp(jnp.full((1, RET_V_DIM), lgf * C, F32))
        cdb = jnp.exp(jnp.full((1, RET_V_DIM), lgb * C, F32))
        dbi_ref[e] = (jnp.where(diff >= 0, jnp.exp(lgf * jnp.maximum(diff, 0.0)), 0.0)
                      + jnp.where(diff <= 0, jnp.exp(lgb * jnp.maximum(-diff, 0.0)), 0.0))

        def local_state(kf32, v_bf16):
            k_both = jnp.where(head_lanes, kf32, pltpu.roll(kf32, RET_QK_DIM, 1))
            return _dot((k_both * dk).T.astype(BF16), v_bf16)

        ctx_state = local_state(kc_ref[...].astype(F32), vc_ref[:, vsl])
        for i in range(n_chunks):
            rows = pl.ds(i * C, C)
            a_ref[i] = local_state(kr_ref[rows, :], v_ref[rows, vsl])

        s_b = ctx_state
        for i in reversed(range(n_chunks)):
            sb_ref[i] = s_b
            s_b = s_b * cdb + a_ref[i]
        s_f = ctx_state
        for i in range(n_chunks):
            st_ref[e, i] = jnp.where(fwd_rows, s_f, sb_ref[i]).astype(BF16)
            s_f = s_f * cdf + a_ref[i]

    def chunk_body(i, carry):
        rows = pl.ds(pl.multiple_of(i * C, C), C)
        q = qr_ref[rows, :]
        q_rolled = pltpu.roll(q, RET_QK_DIM, 1)
        kk = kb_ref[rows, :]
        for e in range(2):
            head_lanes = (lane // RET_QK_DIM) == e
            vsl = slice(e * RET_V_DIM, (e + 1) * RET_V_DIM)
            q_own = jnp.where(head_lanes, q, 0.0).astype(BF16)
            q_both = (jnp.where(head_lanes, q, q_rolled) * dq_ref[e]).astype(BF16)
            inner = _dot_nt(q_own, kk) * dbi_ref[e]
            o = _dot(inner.astype(BF16), v_ref[rows, vsl]) + _dot(q_both, st_ref[e, i])
            mu = jnp.mean(o, axis=-1, keepdims=True)
            oc = o - mu
            var = jnp.mean(oc * oc, axis=-1, keepdims=True)
            gate = _silu(g_ref[rows, vsl].astype(F32))
            o_ref[rows, vsl] = (oc * lax.rsqrt(var + EPS) * gate).astype(BF16)
        return carry

    lax.fori_loop(0, n_chunks, chunk_body, 0, unroll=True)


def _retention(p, pc, pc_off_k, pc_off_v, lg_f, lg_b):
    b, l, _ = p.shape
    lc = pc.shape[1]
    assert lc == RET_CHUNK and l % RET_CHUNK == 0
    n_chunks = l // RET_CHUNK
    cos, sin = _rope_tables(l)
    smem = pl.BlockSpec(memory_space=pltpu.SMEM)
    qk_blk = lambda off: (lambda bi, hp: (bi, 0, off // 128 + hp))
    v_blk = lambda off: (lambda bi, hp: (bi, 0, off // 256 + hp))
    return pl.pallas_call(
        functools.partial(_ret_kernel, n_chunks=n_chunks),
        grid=(b, RET_HEADS // 2),
        in_specs=[
            smem, smem,
            pl.BlockSpec((None, l, 128), qk_blk(OFF_R_Q)),
            pl.BlockSpec((None, l, 128), qk_blk(OFF_R_K)),
            pl.BlockSpec((None, l, 256), v_blk(OFF_R_V)),
            pl.BlockSpec((None, l, 256), v_blk(OFF_R_G)),
            pl.BlockSpec((None, lc, 128), qk_blk(pc_off_k)),
            pl.BlockSpec((None, lc, 256), v_blk(pc_off_v)),
            pl.BlockSpec((l, 128), lambda bi, hp: (0, 0)),
            pl.BlockSpec((l, 128), lambda bi, hp: (0, 0)),
        ],
        out_specs=pl.BlockSpec((None, l, 256), lambda bi, hp: (bi, 0, hp)),
        out_shape=jax.ShapeDtypeStruct((b, l, RET_V_WIDTH), BF16),
        scratch_shapes=[
            pltpu.VMEM((l, 128), F32),
            pltpu.VMEM((l, 128), F32),
            pltpu.VMEM((l, 128), BF16),
            pltpu.VMEM((2, RET_CHUNK, RET_CHUNK), F32),
            pltpu.VMEM((2, RET_CHUNK, 128), F32),
            pltpu.VMEM((n_chunks, 128, RET_V_DIM), F32),
            pltpu.VMEM((n_chunks, 128, RET_V_DIM), F32),
            pltpu.VMEM((2, n_chunks, 128, RET_V_DIM), BF16),
        ],
        compiler_params=_params("parallel", "parallel"),
        name="ret",
    )(lg_f, lg_b, p, p, p, p, pc, pc, cos, sin)


OUTPROJ_SUB_ROWS = 256


def _outproj_kernel(na_ref, ret_ref, wt_ref, wb_ref, x_ref, g1_ref, sh2_ref, sc2_ref, npost_ref, npre_ref,
                    xn_ref, h2_ref):
    gate_gain = g1_ref[...] * npost_ref[...]
    gain2 = npre_ref[...] * (1.0 + sc2_ref[...])
    shift2 = sh2_ref[...]
    for s in range(x_ref.shape[0] // OUTPROJ_SUB_ROWS):
        rows = pl.ds(s * OUTPROJ_SUB_ROWS, OUTPROJ_SUB_ROWS)
        mix = _dot(na_ref[rows, :], wt_ref[...]) + _dot(ret_ref[rows, :], wb_ref[...])
        ms = jnp.mean(mix * mix, axis=-1, keepdims=True)
        xn = x_ref[rows, :] + mix * lax.rsqrt(ms + EPS) * gate_gain
        xn_ref[rows, :] = xn
        ms2 = jnp.mean(xn * xn, axis=-1, keepdims=True)
        h2_ref[rows, :] = (xn * lax.rsqrt(ms2 + EPS) * gain2 + shift2).astype(BF16)


def _outproj(na, ret, w_out_bf16, x2, mod3, norm_post, norm_pre, *, rows_per_mod, tm):
    m, d = x2.shape
    kh = na.shape[1]
    mod_row = lambda i: (i * tm) // rows_per_mod
    mod_blk = lambda k: pl.BlockSpec((None, 1, d), lambda i: (mod_row(i), 0, k))
    return pl.pallas_call(
        _outproj_kernel,
        grid=(m // tm,),
        in_specs=[
            pl.BlockSpec((tm, kh), lambda i: (i, 0)),
            pl.BlockSpec((tm, kh), lambda i: (i, 0)),
            pl.BlockSpec((kh, d), lambda i: (0, 0)),
            pl.BlockSpec((kh, d), lambda i: (1, 0)),
            pl.BlockSpec((tm, d), lambda i: (i, 0)),
            mod_blk(2), mod_blk(3), mod_blk(4),
            pl.BlockSpec((1, d), lambda i: (0, 0)),
            pl.BlockSpec((1, d), lambda i: (0, 0)),
        ],
        out_specs=[pl.BlockSpec((tm, d), lambda i: (i, 0)), pl.BlockSpec((tm, d), lambda i: (i, 0))],
        out_shape=[jax.ShapeDtypeStruct((m, d), F32), jax.ShapeDtypeStruct((m, d), BF16)],
        compiler_params=_params("parallel"),
        name="outproj",
    )(na, ret, w_out_bf16, w_out_bf16, x2, mod3, mod3, mod3, norm_post, norm_pre)


FFN_SUB_ROWS = 256


def _ffn_kernel(h_ref, wgu_ref, wd_ref, xn_ref, g2_ref, npost_ref, o_ref, acc_ref):
    f = pl.program_id(1)

    @pl.when(f == 0)
    def _():
        acc_ref[...] = jnp.zeros_like(acc_ref)

    tf = wd_ref.shape[0]
    au = _dot(h_ref[...], wgu_ref[...])
    t = (_silu(au[:, :tf]) * au[:, tf:]).astype(BF16)
    acc_ref[...] += _dot(t, wd_ref[...])

    @pl.when(f == pl.num_programs(1) - 1)
    def _():
        gate_gain = g2_ref[...] * npost_ref[...]
        for s in range(h_ref.shape[0] // FFN_SUB_ROWS):
            rows = pl.ds(s * FFN_SUB_ROWS, FFN_SUB_ROWS)
            y = acc_ref[rows, :]
            ms = jnp.mean(y * y, axis=-1, keepdims=True)
            o_ref[rows, :] = xn_ref[rows, :] + y * lax.rsqrt(ms + EPS) * gate_gain


def _ffn_pack_gate_up(w_gate, w_up, tf):
    d, dff = w_gate.shape
    both = jnp.stack([w_gate.reshape(d, dff // tf, tf), w_up.reshape(d, dff // tf, tf)], axis=2)
    return both.astype(BF16).reshape(d, 2 * dff)


def _ffn(h2, wgu, wd, xn, mod3, norm_post, *, rows_per_mod, tm, tf):
    m, d = h2.shape
    dff = wd.shape[0]
    mod_row = lambda i: (i * tm) // rows_per_mod
    return pl.pallas_call(
        _ffn_kernel,
        grid=(m // tm, dff // tf),
        in_specs=[
            pl.BlockSpec((tm, d), lambda i, f: (i, 0)),
            pl.BlockSpec((d, 2 * tf), lambda i, f: (0, f)),
            pl.BlockSpec((tf, d), lambda i, f: (f, 0)),
            pl.BlockSpec((tm, d), lambda i, f: (i, 0)),
            pl.BlockSpec((None, 1, d), lambda i, f: (mod_row(i), 0, 5)),
            pl.BlockSpec((1, d), lambda i, f: (0, 0)),
        ],
        out_specs=pl.BlockSpec((tm, d), lambda i, f: (i, 0)),
        out_shape=jax.ShapeDtypeStruct((m, d), F32),
        scratch_shapes=[pltpu.VMEM((tm, d), F32)],
        compiler_params=_params("parallel", "arbitrary"),
        name="ffn",
    )(h2, wgu, wd, xn, mod3, norm_post)


def kernel(x, c, ctx, c_ctx, ada_w, ada_b, norm_pre_mix, norm_post_mix, norm_pre_ffn, norm_post_ffn, w_in,
           na_rpb, ret_log_gamma_fwd, ret_log_gamma_bwd, w_out, w_gate, w_up, w_down):
    b, l, d = x.shape
    lc = ctx.shape[1]
    depth = ada_w.shape[0]
    assert depth == 1, "the context-stream update between layers is not implemented"
    li = 0

    mod_rows = -(-(b + 1) // 8) * 8
    cc = jnp.concatenate([c, c_ctx[None, :], jnp.zeros((mod_rows - b - 1, d), F32)], axis=0)
    mod3 = _adaln(cc, ada_w[li], ada_b[li]).reshape(mod_rows, 1, 6 * d)

    w_in_b = w_in[li].astype(BF16)
    x2 = x.reshape(b * l, d)
    ctx2 = ctx.reshape(b * lc, d)
    gain_pre = norm_pre_mix[li].reshape(1, d)

    tn = 512
    p = _inproj(x2, gain_pre, mod3, w_in_b, rows_per_mod=l, mod_row0=0,
                col_tiles=tuple(range(IN_WIDTH // tn)), tm=1024, tn=tn)
    ctx_tiles = tuple(range(OFF_NA_K // tn, OFF_R_Q // tn)) + tuple(range(OFF_R_K // tn, OFF_R_G // tn))
    pc = _inproj(ctx2, gain_pre, mod3, w_in_b, rows_per_mod=b * lc, mod_row0=b,
                 col_tiles=ctx_tiles, tm=1024, tn=tn)
    p = p.reshape(b, l, IN_WIDTH)
    pc = pc.reshape(b, lc, len(ctx_tiles) * tn)
    pc_na_k, pc_na_v = 0, NA_WIDTH
    pc_r_k, pc_r_v = 2 * NA_WIDTH, 2 * NA_WIDTH + RET_QK_WIDTH

    o_na = _neighbourhood_attention(p, pc, pc_na_k, pc_na_v, _na_bias_table(na_rpb[li], l // GRID_W))
    o_ret = _retention(p, pc, pc_r_k, pc_r_v, ret_log_gamma_fwd[li].astype(F32), ret_log_gamma_bwd[li].astype(F32))

    x_new, h2 = _outproj(o_na.reshape(b * l, NA_WIDTH), o_ret.reshape(b * l, RET_V_WIDTH), w_out[li].astype(BF16),
                         x2, mod3, norm_post_mix[li].reshape(1, d), norm_pre_ffn[li].reshape(1, d),
                         rows_per_mod=l, tm=512)
    ffn_tf = 512
    out = _ffn(h2, _ffn_pack_gate_up(w_gate[li], w_up[li], ffn_tf), w_down[li].astype(BF16), x_new, mod3,
               norm_post_ffn[li].reshape(1, d), rows_per_mod=l, tm=512, tf=ffn_tf)
    return out.reshape(b, l, d)
```

```python
import functools

import numpy as np
import jax
import jax.numpy as jnp
from jax import lax
from jax.experimental import pallas as pl
from jax.experimental.pallas import tpu as pltpu

F32 = jnp.float32
BF16 = jnp.bfloat16

D_MODEL = 2048
GRID_W = 64
NA_HEADS = 8
NA_HEAD_DIM = 128
NA_WIN_H = 8
NA_WIN_W = 16
RET_HEADS = 8
RET_QK_DIM = 64
RET_V_DIM = 128
NA_WIDTH = NA_HEADS * NA_HEAD_DIM
RET_QK_WIDTH = RET_HEADS * RET_QK_DIM
RET_V_WIDTH = RET_HEADS * RET_V_DIM
IN_WIDTH = 3 * NA_WIDTH + 2 * RET_QK_WIDTH + 2 * RET_V_WIDTH
ROPE_BASE = 10000.0
EPS = 1e-6
MASK_VALUE = -1e30

OFF_NA_Q = 0
OFF_NA_K = NA_WIDTH
OFF_NA_V = 2 * NA_WIDTH
OFF_R_Q = 3 * NA_WIDTH
OFF_R_K = OFF_R_Q + RET_QK_WIDTH
OFF_R_V = OFF_R_K + RET_QK_WIDTH
OFF_R_G = OFF_R_V + RET_V_WIDTH

VMEM_LIMIT_BYTES = 56 * 1024 * 1024

RET_CHUNK = 256


def _params(*semantics):
    return pltpu.CompilerParams(dimension_semantics=semantics, vmem_limit_bytes=VMEM_LIMIT_BYTES)


def _dot(a, b):
    return jnp.dot(a, b, preferred_element_type=F32)


def _dot_nt(a, b):
    return lax.dot_general(a, b, (((1,), (1,)), ((), ())), preferred_element_type=F32)


def _silu(x):
    return x * jax.nn.sigmoid(x)


def _adaln_kernel(c_ref, w_ref, b_ref, o_ref):
    s = _silu(c_ref[...]).astype(BF16)
    o_ref[...] = _dot(s, w_ref[...].astype(BF16)) + b_ref[...]


def _adaln(cc, ada_w, ada_b, tn=1024):
    rows, d = cc.shape
    n = ada_w.shape[1]
    return pl.pallas_call(
        _adaln_kernel,
        grid=(n // tn,),
        in_specs=[
            pl.BlockSpec((rows, d), lambda j: (0, 0)),
            pl.BlockSpec((d, tn), lambda j: (0, j)),
            pl.BlockSpec((1, tn), lambda j: (0, j)),
        ],
        out_specs=pl.BlockSpec((rows, tn), lambda j: (0, j)),
        out_shape=jax.ShapeDtypeStruct((rows, n), F32),
        compiler_params=_params("arbitrary"),
        name="adaln",
    )(cc, ada_w, ada_b.reshape(1, n))


INPROJ_SUB_ROWS = 256


def _inproj_kernel(x_ref, g_ref, sh_ref, sc_ref, w_ref, o_ref, h_ref):
    j = pl.program_id(1)

    @pl.when(j == 0)
    def _():
        gain = g_ref[...] * (1.0 + sc_ref[...])
        shift = sh_ref[...]
        for s in range(x_ref.shape[0] // INPROJ_SUB_ROWS):
            rows = pl.ds(s * INPROJ_SUB_ROWS, INPROJ_SUB_ROWS)
            x = x_ref[rows, :]
            ms = jnp.mean(x * x, axis=-1, keepdims=True)
            h = (x * lax.rsqrt(ms + EPS) * gain + shift).astype(BF16)
            h_ref[rows, :] = h
            o_ref[rows, :] = _dot(h, w_ref[...]).astype(BF16)

    @pl.when(j > 0)
    def _():
        o_ref[...] = _dot(h_ref[...], w_ref[...]).astype(BF16)


def _inproj(x2, gain, mod3, w_bf16, *, rows_per_mod, mod_row0, col_tiles, tm, tn):
    m, d = x2.shape
    n_out = len(col_tiles) * tn
    first = col_tiles[0]
    gap_at = next((i for i in range(1, len(col_tiles)) if col_tiles[i] != col_tiles[i - 1] + 1), None)
    if gap_at is None:
        w_map = lambda i, j: (0, j + first)
    else:
        gap = col_tiles[gap_at] - col_tiles[gap_at - 1] - 1
        assert all(col_tiles[i] == first + i + (gap if i >= gap_at else 0) for i in range(len(col_tiles)))
        w_map = lambda i, j: (0, j + first + jnp.where(j >= gap_at, gap, 0))
    mod_row = lambda i: mod_row0 + (i * tm) // rows_per_mod
    return pl.pallas_call(
        _inproj_kernel,
        grid=(m // tm, len(col_tiles)),
        in_specs=[
            pl.BlockSpec((tm, d), lambda i, j: (i, 0)),
            pl.BlockSpec((1, d), lambda i, j: (0, 0)),
            pl.BlockSpec((None, 1, d), lambda i, j: (mod_row(i), 0, 0)),
            pl.BlockSpec((None, 1, d), lambda i, j: (mod_row(i), 0, 1)),
            pl.BlockSpec((d, tn), w_map),
        ],
        out_specs=pl.BlockSpec((tm, tn), lambda i, j: (i, j)),
        out_shape=jax.ShapeDtypeStruct((m, n_out), BF16),
        scratch_shapes=[pltpu.VMEM((tm, d), BF16)],
        compiler_params=_params("parallel", "arbitrary"),
        name="inproj",
    )(x2, gain, mod3, mod3, w_bf16)


NA_GROUP_ROWS = 4
NA_BAND_ROWS = NA_GROUP_ROWS + NA_WIN_H
NA_Q = NA_GROUP_ROWS * GRID_W
NA_BAND = NA_BAND_ROWS * GRID_W


def _na_band_start(g, rows):
    lo = g * NA_GROUP_ROWS - NA_WIN_H // 2
    if isinstance(g, (int, np.integer)):
        return min(max(lo, 0), rows - NA_BAND_ROWS)
    return jnp.minimum(jnp.maximum(lo, 0), rows - NA_BAND_ROWS)


def _na_row_patterns(rows):
    groups = rows // NA_GROUP_ROWS
    assert rows % NA_GROUP_ROWS == 0 and rows >= NA_BAND_ROWS and groups >= 3

    def row_pattern(g):
        r = g * NA_GROUP_ROWS + np.arange(NA_GROUP_ROWS)[:, None]
        krow = _na_band_start(g, rows) + np.arange(NA_BAND_ROWS)[None, :]
        rs = np.clip(r - NA_WIN_H // 2, 0, rows - NA_WIN_H)
        ok = (krow >= rs) & (krow < rs + NA_WIN_H)
        return ok, np.clip(krow - r + NA_WIN_H - 1, 0, 2 * NA_WIN_H - 2)

    pats = [row_pattern(0), row_pattern(1), row_pattern(groups - 1)]
    for g in range(1, groups - 1):
        ok, dr = row_pattern(g)
        assert (ok == pats[1][0]).all() and (dr[ok] == pats[1][1][ok]).all()
    return pats


def _na_kernel(q_ref, k_ref, v_ref, kc_ref, vc_ref, t_ref, o_ref, qs_ref, s0_ref, s1_ref, bias_ref, *, rows):
    scale = NA_HEAD_DIM ** -0.5
    qs_ref[...] = (q_ref[...].astype(F32) * scale).astype(BF16)
    groups = rows // NA_GROUP_ROWS

    @pl.when(pl.program_id(1) == 0)
    def _():
        masked = jnp.full((GRID_W, GRID_W), MASK_VALUE, F32)
        for pi, (ok, dr) in enumerate(_na_row_patterns(rows)):
            for a in range(NA_GROUP_ROWS):
                for j in range(NA_BAND_ROWS):
                    tile = t_ref[int(dr[a, j])] if ok[a, j] else masked
                    bias_ref[pi, a * GRID_W:(a + 1) * GRID_W, j * GRID_W:(j + 1) * GRID_W] = tile

    def scores(g, s_ref):
        q0 = g * NA_Q
        k0 = _na_band_start(g, rows) * GRID_W
        pattern = 0 if g == 0 else (2 if g == groups - 1 else 1)
        q = qs_ref[pl.ds(q0, NA_Q), :]
        s_ref[:, :NA_BAND] = _dot_nt(q, k_ref[pl.ds(k0, NA_BAND), :]) + bias_ref[pattern]
        s_ref[:, NA_BAND:] = _dot_nt(q, kc_ref[...])

    def attend(g, s_ref):
        q0 = g * NA_Q
        k0 = _na_band_start(g, rows) * GRID_W
        s = s_ref[...]
        m = jnp.max(s, axis=-1, keepdims=True)
        p = jnp.exp(s - m)
        l = jnp.sum(p, axis=-1, keepdims=True)
        pb = p.astype(BF16)
        o = _dot(pb[:, :NA_BAND], v_ref[pl.ds(k0, NA_BAND), :]) + _dot(pb[:, NA_BAND:], vc_ref[...])
        o_ref[pl.ds(q0, NA_Q), :] = (o / l).astype(BF16)

    s_refs = (s0_ref, s1_ref)
    scores(0, s_refs[0])
    for g in range(groups):
        if g + 1 < groups:
            scores(g + 1, s_refs[(g + 1) % 2])
        attend(g, s_refs[g % 2])


def _na_col_table(rpb):
    cols = np.arange(GRID_W)
    col_start = np.clip(cols - NA_WIN_W // 2, 0, GRID_W - NA_WIN_W)
    kc = np.arange(GRID_W)
    col_ok = (kc[None, :] >= col_start[:, None]) & (kc[None, :] < col_start[:, None] + NA_WIN_W)
    dc_idx = np.clip(kc[None, :] - cols[:, None] + NA_WIN_W - 1, 0, 2 * NA_WIN_W - 2)
    return jnp.where(col_ok[None, None], rpb.astype(F32)[:, :, dc_idx], MASK_VALUE)


def _neighbourhood_attention(p, pc, pc_off_k, pc_off_v, col_table):
    b, l, _ = p.shape
    lc = pc.shape[1]
    rows = l // GRID_W
    hd = NA_HEAD_DIM
    blk = lambda off: (lambda h, bi: (bi, 0, off // hd + h))
    return pl.pallas_call(
        functools.partial(_na_kernel, rows=rows),
        grid=(NA_HEADS, b),
        in_specs=[
            pl.BlockSpec((None, l, hd), blk(OFF_NA_Q)),
            pl.BlockSpec((None, l, hd), blk(OFF_NA_K)),
            pl.BlockSpec((None, l, hd), blk(OFF_NA_V)),
            pl.BlockSpec((None, lc, hd), blk(pc_off_k)),
            pl.BlockSpec((None, lc, hd), blk(pc_off_v)),
            pl.BlockSpec((None,) + col_table.shape[1:], lambda h, bi: (h, 0, 0, 0)),
        ],
        out_specs=pl.BlockSpec((None, l, hd), lambda h, bi: (bi, 0, h)),
        out_shape=jax.ShapeDtypeStruct((b, l, NA_WIDTH), BF16),
        scratch_shapes=[pltpu.VMEM((l, hd), BF16),
                        pltpu.VMEM((NA_Q, NA_BAND + lc), F32),
                        pltpu.VMEM((NA_Q, NA_BAND + lc), F32),
                        pltpu.VMEM((3, NA_Q, NA_BAND), F32)],
        compiler_params=_params("parallel", "arbitrary"),
        name="na",
    )(p, p, p, pc, pc, col_table)


def _rope_tables(l):
    t = np.arange(l)
    pos_row, pos_col = t // GRID_W, t % GRID_W
    quarter = RET_QK_DIM // 4
    inv_freq = ROPE_BASE ** (-np.arange(quarter, dtype=np.float64) / quarter)
    ang_r = pos_row[:, None] * inv_freq
    ang_c = pos_col[:, None] * inv_freq
    cos = np.concatenate([np.cos(ang_r), np.cos(ang_r), np.cos(ang_c), np.cos(ang_c)], axis=-1)
    sin = np.concatenate([-np.sin(ang_r), np.sin(ang_r), -np.sin(ang_c), np.sin(ang_c)], axis=-1)
    reps = 128 // RET_QK_DIM
    return (jnp.asarray(np.tile(cos, (1, reps)), F32), jnp.asarray(np.tile(sin, (1, reps)), F32))


def _ret_kernel(lgf_ref, lgb_ref, q_ref, k_ref, v_ref, g_ref, kc_ref, vc_ref, cos_ref, sin_ref,
                o_ref, qr_ref, kr_ref, kb_ref, dbi_ref, dq_ref, a_ref, sb_ref, st_ref, *, n_chunks):
    C = RET_CHUNK
    hp = pl.program_id(1)
    lane = lax.broadcasted_iota(jnp.int32, (1, 128), 1)
    sub = lax.broadcasted_iota(jnp.int32, (128, 1), 0)
    first_half = (lane % (RET_QK_DIM // 2)) < (RET_QK_DIM // 4)

    def rope(x):
        partner = jnp.where(first_half, pltpu.roll(x, 128 - RET_QK_DIM // 4, 1), pltpu.roll(x, RET_QK_DIM // 4, 1))
        return x * cos_ref[...] + partner * sin_ref[...]

    qr_ref[...] = rope(q_ref[...].astype(F32)) * (RET_QK_DIM ** -0.5)
    k_rot = rope(k_ref[...].astype(F32))
    kr_ref[...] = k_rot
    kb_ref[...] = k_rot.astype(BF16)

    row = lax.broadcasted_iota(jnp.int32, (C, 128), 0).astype(F32)
    ri = lax.broadcasted_iota(jnp.int32, (C, C), 0)
    ci = lax.broadcasted_iota(jnp.int32, (C, C), 1)
    diff = (ri - ci).astype(F32)

    for e in range(2):
        lgf = lgf_ref[2 * hp + e]
        lgb = lgb_ref[2 * hp + e]
        head_lanes = (lane // RET_QK_DIM) == e
        fwd_rows = (sub // RET_QK_DIM) == e
        vsl = slice(e * RET_V_DIM, (e + 1) * RET_V_DIM)

        dk = jnp.where(head_lanes, jnp.exp(lgf * (C - 1.0 - row)), jnp.exp(lgb * row))
        dq_ref[e] = jnp.where(head_lanes, jnp.exp(lgf * (row + 1.0)), jnp.exp(lgb * (C - row)))
        cdf = jnp.exp(jnp.full((1, RET_V_DIM), lgf * C, F32))
        cdb = jnp.exp(jnp.full((1, RET_V_DIM), lgb * C, F32))
        dbi_ref[e] = (jnp.where(diff >= 0, jnp.exp(lgf * jnp.maximum(diff, 0.0)), 0.0)
                      + jnp.where(diff <= 0, jnp.exp(lgb * jnp.maximum(-diff, 0.0)), 0.0))

        def local_state(kf32, v_bf16):
            k_both = jnp.where(head_lanes, kf32, pltpu.roll(kf32, RET_QK_DIM, 1))
            return _dot((k_both * dk).T.astype(BF16), v_bf16)

        ctx_state = local_state(kc_ref[...].astype(F32), vc_ref[:, vsl])
        for i in range(n_chunks):
            rows = pl.ds(i * C, C)
            a_ref[i] = local_state(kr_ref[rows, :], v_ref[rows, vsl])

        s_b = ctx_state
        for i in reversed(range(n_chunks)):
            sb_ref[i] = s_b
            s_b = s_b * cdb + a_ref[i]
        s_f = ctx_state
        for i in range(n_chunks):
            st_ref[e, i] = jnp.where(fwd_rows, s_f, sb_ref[i]).astype(BF16)
            s_f = s_f * cdf + a_ref[i]

    def chunk_body(i, carry):
        rows = pl.ds(pl.multiple_of(i * C, C), C)
        q = qr_ref[rows, :]
        q_rolled = pltpu.roll(q, RET_QK_DIM, 1)
        kk = kb_ref[rows, :]
        for e in range(2):
            head_lanes = (lane // RET_QK_DIM) == e
            vsl = slice(e * RET_V_DIM, (e + 1) * RET_V_DIM)
            q_own = jnp.where(head_lanes, q, 0.0).astype(BF16)
            q_both = (jnp.where(head_lanes, q, q_rolled) * dq_ref[e]).astype(BF16)
            inner = _dot_nt(q_own, kk) * dbi_ref[e]
            o = _dot(inner.astype(BF16), v_ref[rows, vsl]) + _dot(q_both, st_ref[e, i])
            mu = jnp.mean(o, axis=-1, keepdims=True)
            oc = o - mu
            var = jnp.mean(oc * oc, axis=-1, keepdims=True)
            gate = _silu(g_ref[rows, vsl].astype(F32))
            o_ref[rows, vsl] = (oc * lax.rsqrt(var + EPS) * gate).astype(BF16)
        return carry

    lax.fori_loop(0, n_chunks, chunk_body, 0, unroll=True)


def _retention(p, pc, pc_off_k, pc_off_v, lg_f, lg_b):
    b, l, _ = p.shape
    lc = pc.shape[1]
    assert lc == RET_CHUNK and l % RET_CHUNK == 0
    n_chunks = l // RET_CHUNK
    cos, sin = _rope_tables(l)
    smem = pl.BlockSpec(memory_space=pltpu.SMEM)
    qk_blk = lambda off: (lambda bi, hp: (bi, 0, off // 128 + hp))
    v_blk = lambda off: (lambda bi, hp: (bi, 0, off // 256 + hp))
    return pl.pallas_call(
        functools.partial(_ret_kernel, n_chunks=n_chunks),
        grid=(b, RET_HEADS // 2),
        in_specs=[
            smem, smem,
            pl.BlockSpec((None, l, 128), qk_blk(OFF_R_Q)),
            pl.BlockSpec((None, l, 128), qk_blk(OFF_R_K)),
            pl.BlockSpec((None, l, 256), v_blk(OFF_R_V)),
            pl.BlockSpec((None, l, 256), v_blk(OFF_R_G)),
            pl.BlockSpec((None, lc, 128), qk_blk(pc_off_k)),
            pl.BlockSpec((None, lc, 256), v_blk(pc_off_v)),
            pl.BlockSpec((l, 128), lambda bi, hp: (0, 0)),
            pl.BlockSpec((l, 128), lambda bi, hp: (0, 0)),
        ],
        out_specs=pl.BlockSpec((None, l, 256), lambda bi, hp: (bi, 0, hp)),
        out_shape=jax.ShapeDtypeStruct((b, l, RET_V_WIDTH), BF16),
        scratch_shapes=[
            pltpu.VMEM((l, 128), F32),
            pltpu.VMEM((l, 128), F32),
            pltpu.VMEM((l, 128), BF16),
            pltpu.VMEM((2, RET_CHUNK, RET_CHUNK), F32),
            pltpu.VMEM((2, RET_CHUNK, 128), F32),
            pltpu.VMEM((n_chunks, 128, RET_V_DIM), F32),
            pltpu.VMEM((n_chunks, 128, RET_V_DIM), F32),
            pltpu.VMEM((2, n_chunks, 128, RET_V_DIM), BF16),
        ],
        compiler_params=_params("parallel", "parallel"),
        name="ret",
    )(lg_f, lg_b, p, p, p, p, pc, pc, cos, sin)


OUTPROJ_SUB_ROWS = 256


def _outproj_kernel(na_ref, ret_ref, wt_ref, wb_ref, x_ref, g1_ref, sh2_ref, sc2_ref, npost_ref, npre_ref,
                    xn_ref, h2_ref):
    gate_gain = g1_ref[...] * npost_ref[...]
    gain2 = npre_ref[...] * (1.0 + sc2_ref[...])
    shift2 = sh2_ref[...]
    for s in range(x_ref.shape[0] // OUTPROJ_SUB_ROWS):
        rows = pl.ds(s * OUTPROJ_SUB_ROWS, OUTPROJ_SUB_ROWS)
        mix = _dot(na_ref[rows, :], wt_ref[...]) + _dot(ret_ref[rows, :], wb_ref[...])
        ms = jnp.mean(mix * mix, axis=-1, keepdims=True)
        xn = x_ref[rows, :] + mix * lax.rsqrt(ms + EPS) * gate_gain
        xn_ref[rows, :] = xn
        ms2 = jnp.mean(xn * xn, axis=-1, keepdims=True)
        h2_ref[rows, :] = (xn * lax.rsqrt(ms2 + EPS) * gain2 + shift2).astype(BF16)


def _outproj(na, ret, w_out_bf16, x2, mod3, norm_post, norm_pre, *, rows_per_mod, tm):
    m, d = x2.shape
    kh = na.shape[1]
    mod_row = lambda i: (i * tm) // rows_per_mod
    mod_blk = lambda k: pl.BlockSpec((None, 1, d), lambda i: (mod_row(i), 0, k))
    return pl.pallas_call(
        _outproj_kernel,
        grid=(m // tm,),
        in_specs=[
            pl.BlockSpec((tm, kh), lambda i: (i, 0)),
            pl.BlockSpec((tm, kh), lambda i: (i, 0)),
            pl.BlockSpec((kh, d), lambda i: (0, 0)),
            pl.BlockSpec((kh, d), lambda i: (1, 0)),
            pl.BlockSpec((tm, d), lambda i: (i, 0)),
            mod_blk(2), mod_blk(3), mod_blk(4),
            pl.BlockSpec((1, d), lambda i: (0, 0)),
            pl.BlockSpec((1, d), lambda i: (0, 0)),
        ],
        out_specs=[pl.BlockSpec((tm, d), lambda i: (i, 0)), pl.BlockSpec((tm, d), lambda i: (i, 0))],
        out_shape=[jax.ShapeDtypeStruct((m, d), F32), jax.ShapeDtypeStruct((m, d), BF16)],
        compiler_params=_params("parallel"),
        name="outproj",
    )(na, ret, w_out_bf16, w_out_bf16, x2, mod3, mod3, mod3, norm_post, norm_pre)


FFN_SUB_ROWS = 256


def _ffn_kernel(h_ref, wg_ref, wu_ref, wd_ref, xn_hbm, g2_ref, npost_ref, o_ref, xn_buf, xn_sem):
    i = pl.program_id(0)
    f = pl.program_id(1)
    tm = h_ref.shape[0]

    def residual_copy():
        return pltpu.make_async_copy(xn_hbm.at[pl.ds(i * tm, tm), :], xn_buf, xn_sem)

    @pl.when(f == 0)
    def _():
        residual_copy().start()
        o_ref[...] = jnp.zeros_like(o_ref)

    h = h_ref[...]
    t = (_silu(_dot(h, wg_ref[...])) * _dot(h, wu_ref[...])).astype(BF16)
    o_ref[...] += _dot(t, wd_ref[...])

    @pl.when(f == pl.num_programs(1) - 1)
    def _():
        residual_copy().wait()
        gate_gain = g2_ref[...] * npost_ref[...]
        for s in range(tm // FFN_SUB_ROWS):
            rows = pl.ds(s * FFN_SUB_ROWS, FFN_SUB_ROWS)
            y = o_ref[rows, :]
            ms = jnp.mean(y * y, axis=-1, keepdims=True)
            o_ref[rows, :] = xn_buf[rows, :] + y * lax.rsqrt(ms + EPS) * gate_gain


def _ffn(h2, wg, wu, wd, xn, mod3, norm_post, *, rows_per_mod, tm, tf):
    m, d = h2.shape
    dff = wd.shape[0]
    mod_row = lambda i: (i * tm) // rows_per_mod
    return pl.pallas_call(
        _ffn_kernel,
        grid=(m // tm, dff // tf),
        in_specs=[
            pl.BlockSpec((tm, d), lambda i, f: (i, 0)),
            pl.BlockSpec((d, tf), lambda i, f: (0, f)),
            pl.BlockSpec((d, tf), lambda i, f: (0, f)),
            pl.BlockSpec((tf, d), lambda i, f: (f, 0)),
            pl.BlockSpec(memory_space=pl.ANY),
            pl.BlockSpec((None, 1, d), lambda i, f: (mod_row(i), 0, 5)),
            pl.BlockSpec((1, d), lambda i, f: (0, 0)),
        ],
        out_specs=pl.BlockSpec((tm, d), lambda i, f: (i, 0)),
        out_shape=jax.ShapeDtypeStruct((m, d), F32),
        scratch_shapes=[pltpu.VMEM((tm, d), F32), pltpu.SemaphoreType.DMA(())],
        compiler_params=_params("parallel", "arbitrary"),
        name="ffn",
    )(h2, wg, wu, wd, xn, mod3, norm_post)


def kernel(x, c, ctx, c_ctx, ada_w, ada_b, norm_pre_mix, norm_post_mix, norm_pre_ffn, norm_post_ffn, w_in,
           na_rpb, ret_log_gamma_fwd, ret_log_gamma_bwd, w_out, w_gate, w_up, w_down):
    b, l, d = x.shape
    lc = ctx.shape[1]
    depth = ada_w.shape[0]
    assert depth == 1, "the context-stream update between layers is not implemented"
    li = 0

    mod_rows = -(-(b + 1) // 8) * 8
    cc = jnp.concatenate([c, c_ctx[None, :], jnp.zeros((mod_rows - b - 1, d), F32)], axis=0)
    mod3 = _adaln(cc, ada_w[li], ada_b[li]).reshape(mod_rows, 1, 6 * d)

    w_in_b = w_in[li].astype(BF16)
    x2 = x.reshape(b * l, d)
    ctx2 = ctx.reshape(b * lc, d)
    gain_pre = norm_pre_mix[li].reshape(1, d)

    tn = 512
    p = _inproj(x2, gain_pre, mod3, w_in_b, rows_per_mod=l, mod_row0=0,
                col_tiles=tuple(range(IN_WIDTH // tn)), tm=1024, tn=tn)
    ctx_tiles = tuple(range(OFF_NA_K // tn, OFF_R_Q // tn)) + tuple(range(OFF_R_K // tn, OFF_R_G // tn))
    pc = _inproj(ctx2, gain_pre, mod3, w_in_b, rows_per_mod=b * lc, mod_row0=b,
                 col_tiles=ctx_tiles, tm=1024, tn=tn)
    p = p.reshape(b, l, IN_WIDTH)
    pc = pc.reshape(b, lc, len(ctx_tiles) * tn)
    pc_na_k, pc_na_v = 0, NA_WIDTH
    pc_r_k, pc_r_v = 2 * NA_WIDTH, 2 * NA_WIDTH + RET_QK_WIDTH

    o_na = _neighbourhood_attention(p, pc, pc_na_k, pc_na_v, _na_col_table(na_rpb[li]))
    o_ret = _retention(p, pc, pc_r_k, pc_r_v, ret_log_gamma_fwd[li].astype(F32), ret_log_gamma_bwd[li].astype(F32))

    x_new, h2 = _outproj(o_na.reshape(b * l, NA_WIDTH), o_ret.reshape(b * l, RET_V_WIDTH), w_out[li].astype(BF16),
                         x2, mod3, norm_post_mix[li].reshape(1, d), norm_pre_ffn[li].reshape(1, d),
                         rows_per_mod=l, tm=512)
    out = _ffn(h2, w_gate[li].astype(BF16), w_up[li].astype(BF16), w_down[li].astype(BF16), x_new, mod3,
               norm_post_ffn[li].reshape(1, d), rows_per_mod=l, tm=1024, tf=512)
    return out.reshape(b, l, d)
```

```python
import functools

import numpy as np
import jax
import jax.numpy as jnp
from jax import lax
from jax.experimental import pallas as pl
from jax.experimental.pallas import tpu as pltpu

F32 = jnp.float32
BF16 = jnp.bfloat16

D_MODEL = 2048
GRID_W = 64
NA_HEADS = 8
NA_HEAD_DIM = 128
NA_WIN_H = 8
NA_WIN_W = 16
RET_HEADS = 8
RET_QK_DIM = 64
RET_V_DIM = 128
NA_WIDTH = NA_HEADS * NA_HEAD_DIM
RET_QK_WIDTH = RET_HEADS * RET_QK_DIM
RET_V_WIDTH = RET_HEADS * RET_V_DIM
IN_WIDTH = 3 * NA_WIDTH + 2 * RET_QK_WIDTH + 2 * RET_V_WIDTH
ROPE_BASE = 10000.0
EPS = 1e-6
MASK_VALUE = -1e30

OFF_NA_Q = 0
OFF_NA_K = NA_WIDTH
OFF_NA_V = 2 * NA_WIDTH
OFF_R_Q = 3 * NA_WIDTH
OFF_R_K = OFF_R_Q + RET_QK_WIDTH
OFF_R_V = OFF_R_K + RET_QK_WIDTH
OFF_R_G = OFF_R_V + RET_V_WIDTH

VMEM_LIMIT_BYTES = 56 * 1024 * 1024

RET_CHUNK = 256


def _params(*semantics):
    return pltpu.CompilerParams(dimension_semantics=semantics, vmem_limit_bytes=VMEM_LIMIT_BYTES)


def _dot(a, b):
    return jnp.dot(a, b, preferred_element_type=F32)


def _dot_nt(a, b):
    return lax.dot_general(a, b, (((1,), (1,)), ((), ())), preferred_element_type=F32)


def _silu(x):
    return x * jax.nn.sigmoid(x)


def _adaln_kernel(c_ref, w_ref, b_ref, o_ref):
    s = _silu(c_ref[...]).astype(BF16)
    o_ref[...] = _dot(s, w_ref[...].astype(BF16)) + b_ref[...]


def _adaln(cc, ada_w, ada_b, tn=1024):
    rows, d = cc.shape
    n = ada_w.shape[1]
    return pl.pallas_call(
        _adaln_kernel,
        grid=(n // tn,),
        in_specs=[
            pl.BlockSpec((rows, d), lambda j: (0, 0)),
            pl.BlockSpec((d, tn), lambda j: (0, j)),
            pl.BlockSpec((1, tn), lambda j: (0, j)),
        ],
        out_specs=pl.BlockSpec((rows, tn), lambda j: (0, j)),
        out_shape=jax.ShapeDtypeStruct((rows, n), F32),
        compiler_params=_params("arbitrary"),
        name="adaln",
    )(cc, ada_w, ada_b.reshape(1, n))


INPROJ_SUB_ROWS = 256


def _rope_rotate(x, cos, sin_signed):
    lane = lax.broadcasted_iota(jnp.int32, (1, 128), 1)
    first_half = (lane % (RET_QK_DIM // 2)) < (RET_QK_DIM // 4)
    partner = jnp.where(first_half, pltpu.roll(x, 128 - RET_QK_DIM // 4, 1), pltpu.roll(x, RET_QK_DIM // 4, 1))
    return x * cos + partner * sin_signed


def _inproj_kernel(x_ref, g_ref, sh_ref, sc_ref, w_ref, *refs, first_tile_scale, rope_tile):
    if rope_tile is None:
        o_ref, h_ref = refs
    else:
        cos_ref, sin_ref, o_ref, h_ref = refs
    j = pl.program_id(1)
    n_slices = x_ref.shape[0] // INPROJ_SUB_ROWS

    @pl.when(j == 0)
    def _():
        gain = g_ref[...] * (1.0 + sc_ref[...])
        shift = sh_ref[...]
        for s in range(n_slices):
            rows = pl.ds(s * INPROJ_SUB_ROWS, INPROJ_SUB_ROWS)
            x = x_ref[rows, :]
            ms = jnp.mean(x * x, axis=-1, keepdims=True)
            h = (x * lax.rsqrt(ms + EPS) * gain + shift).astype(BF16)
            h_ref[rows, :] = h
            res = _dot(h, w_ref[...])
            if first_tile_scale != 1.0:
                res = res * first_tile_scale
            o_ref[rows, :] = res.astype(BF16)

    plain = (j > 0) if rope_tile is None else ((j > 0) & (j != rope_tile))

    @pl.when(plain)
    def _():
        o_ref[...] = _dot(h_ref[...], w_ref[...]).astype(BF16)

    if rope_tile is not None:
        @pl.when(j == rope_tile)
        def _():
            q_scale = RET_QK_DIM ** -0.5
            for s in range(n_slices):
                rows = pl.ds(s * INPROJ_SUB_ROWS, INPROJ_SUB_ROWS)
                res = _dot(h_ref[rows, :], w_ref[...])
                cos, sin = cos_ref[rows, :], sin_ref[rows, :]
                cos_q, sin_q = cos * q_scale, sin * q_scale
                for cg in range(o_ref.shape[1] // 128):
                    lanes = slice(cg * 128, (cg + 1) * 128)
                    is_q = cg < RET_QK_WIDTH // 128
                    y = _rope_rotate(res[:, lanes], cos_q if is_q else cos, sin_q if is_q else sin)
                    o_ref[rows, lanes] = y.astype(BF16)


def _inproj(x2, gain, mod3, w_bf16, *, rows_per_mod, mod_row0, col_tiles, tm, tn,
            first_tile_scale=1.0, rope_tile=None, rope_tables=None):
    m, d = x2.shape
    n_out = len(col_tiles) * tn
    first = col_tiles[0]
    gap_at = next((i for i in range(1, len(col_tiles)) if col_tiles[i] != col_tiles[i - 1] + 1), None)
    if gap_at is None:
        w_map = lambda i, j: (0, j + first)
    else:
        gap = col_tiles[gap_at] - col_tiles[gap_at - 1] - 1
        assert all(col_tiles[i] == first + i + (gap if i >= gap_at else 0) for i in range(len(col_tiles)))
        w_map = lambda i, j: (0, j + first + jnp.where(j >= gap_at, gap, 0))
    mod_row = lambda i: mod_row0 + (i * tm) // rows_per_mod
    in_specs = [
        pl.BlockSpec((tm, d), lambda i, j: (i, 0)),
        pl.BlockSpec((1, d), lambda i, j: (0, 0)),
        pl.BlockSpec((None, 1, d), lambda i, j: (mod_row(i), 0, 0)),
        pl.BlockSpec((None, 1, d), lambda i, j: (mod_row(i), 0, 1)),
        pl.BlockSpec((d, tn), w_map),
    ]
    operands = [x2, gain, mod3, mod3, w_bf16]
    if rope_tile is not None:
        assert tn == 2 * RET_QK_WIDTH and col_tiles[rope_tile] * tn == OFF_R_Q and rows_per_mod % tm == 0
        pos_blk = pl.BlockSpec((tm, 128), lambda i, j: (i % (rows_per_mod // tm), 0))
        in_specs += [pos_blk, pos_blk]
        operands += list(rope_tables)
    return pl.pallas_call(
        functools.partial(_inproj_kernel, first_tile_scale=first_tile_scale, rope_tile=rope_tile),
        grid=(m // tm, len(col_tiles)),
        in_specs=in_specs,
        out_specs=pl.BlockSpec((tm, tn), lambda i, j: (i, j)),
        out_shape=jax.ShapeDtypeStruct((m, n_out), BF16),
        scratch_shapes=[pltpu.VMEM((tm, d), BF16)],
        compiler_params=_params("parallel", "arbitrary"),
        name="inproj",
    )(*operands)


NA_GROUP_ROWS = 4
NA_BAND_ROWS = NA_GROUP_ROWS + NA_WIN_H
NA_Q = NA_GROUP_ROWS * GRID_W
NA_BAND = NA_BAND_ROWS * GRID_W


def _na_band_start(g, rows):
    lo = g * NA_GROUP_ROWS - NA_WIN_H // 2
    if isinstance(g, (int, np.integer)):
        return min(max(lo, 0), rows - NA_BAND_ROWS)
    return jnp.minimum(jnp.maximum(lo, 0), rows - NA_BAND_ROWS)


def _na_row_patterns(rows):
    groups = rows // NA_GROUP_ROWS
    assert rows % NA_GROUP_ROWS == 0 and rows >= NA_BAND_ROWS and groups >= 3

    def row_pattern(g):
        r = g * NA_GROUP_ROWS + np.arange(NA_GROUP_ROWS)[:, None]
        krow = _na_band_start(g, rows) + np.arange(NA_BAND_ROWS)[None, :]
        rs = np.clip(r - NA_WIN_H // 2, 0, rows - NA_WIN_H)
        ok = (krow >= rs) & (krow < rs + NA_WIN_H)
        return ok, np.clip(krow - r + NA_WIN_H - 1, 0, 2 * NA_WIN_H - 2)

    pats = [row_pattern(0), row_pattern(1), row_pattern(groups - 1)]
    for g in range(1, groups - 1):
        ok, dr = row_pattern(g)
        assert (ok == pats[1][0]).all() and (dr[ok] == pats[1][1][ok]).all()
    return pats


def _na_kernel(q_ref, k_ref, v_ref, kc_ref, vc_ref, t_ref, o_ref, s0_ref, s1_ref, bias_ref, *, rows):
    groups = rows // NA_GROUP_ROWS

    @pl.when(pl.program_id(1) == 0)
    def _():
        masked = jnp.full((GRID_W, GRID_W), MASK_VALUE, F32)
        for pi, (ok, dr) in enumerate(_na_row_patterns(rows)):
            for a in range(NA_GROUP_ROWS):
                for j in range(NA_BAND_ROWS):
                    tile = t_ref[int(dr[a, j])] if ok[a, j] else masked
                    bias_ref[pi, a * GRID_W:(a + 1) * GRID_W, j * GRID_W:(j + 1) * GRID_W] = tile

    def scores(g, s_ref):
        q0 = g * NA_Q
        k0 = _na_band_start(g, rows) * GRID_W
        pattern = 0 if g == 0 else (2 if g == groups - 1 else 1)
        q = q_ref[pl.ds(q0, NA_Q), :]
        s_ref[:, :NA_BAND] = _dot_nt(q, k_ref[pl.ds(k0, NA_BAND), :]) + bias_ref[pattern]
        s_ref[:, NA_BAND:] = _dot_nt(q, kc_ref[...])

    def attend(g, s_ref):
        q0 = g * NA_Q
        k0 = _na_band_start(g, rows) * GRID_W
        s = s_ref[...]
        m = jnp.max(s, axis=-1, keepdims=True)
        p = jnp.exp(s - m)
        l = jnp.sum(p, axis=-1, keepdims=True)
        pb = p.astype(BF16)
        o = _dot(pb[:, :NA_BAND], v_ref[pl.ds(k0, NA_BAND), :]) + _dot(pb[:, NA_BAND:], vc_ref[...])
        o_ref[pl.ds(q0, NA_Q), :] = (o / l).astype(BF16)

    s_refs = (s0_ref, s1_ref)
    scores(0, s_refs[0])
    for g in range(groups):
        if g + 1 < groups:
            scores(g + 1, s_refs[(g + 1) % 2])
        attend(g, s_refs[g % 2])


def _na_col_table(rpb):
    cols = np.arange(GRID_W)
    col_start = np.clip(cols - NA_WIN_W // 2, 0, GRID_W - NA_WIN_W)
    kc = np.arange(GRID_W)
    col_ok = (kc[None, :] >= col_start[:, None]) & (kc[None, :] < col_start[:, None] + NA_WIN_W)
    dc_idx = np.clip(kc[None, :] - cols[:, None] + NA_WIN_W - 1, 0, 2 * NA_WIN_W - 2)
    return jnp.where(col_ok[None, None], rpb.astype(F32)[:, :, dc_idx], MASK_VALUE)


def _neighbourhood_attention(p, pc, pc_off_k, pc_off_v, col_table):
    b, l, _ = p.shape
    lc = pc.shape[1]
    rows = l // GRID_W
    hd = NA_HEAD_DIM
    blk = lambda off: (lambda h, bi: (bi, 0, off // hd + h))
    return pl.pallas_call(
        functools.partial(_na_kernel, rows=rows),
        grid=(NA_HEADS, b),
        in_specs=[
            pl.BlockSpec((None, l, hd), blk(OFF_NA_Q)),
            pl.BlockSpec((None, l, hd), blk(OFF_NA_K)),
            pl.BlockSpec((None, l, hd), blk(OFF_NA_V)),
            pl.BlockSpec((None, lc, hd), blk(pc_off_k)),
            pl.BlockSpec((None, lc, hd), blk(pc_off_v)),
            pl.BlockSpec((None,) + col_table.shape[1:], lambda h, bi: (h, 0, 0, 0)),
        ],
        out_specs=pl.BlockSpec((None, l, hd), lambda h, bi: (bi, 0, h)),
        out_shape=jax.ShapeDtypeStruct((b, l, NA_WIDTH), BF16),
        scratch_shapes=[pltpu.VMEM((NA_Q, NA_BAND + lc), F32),
                        pltpu.VMEM((NA_Q, NA_BAND + lc), F32),
                        pltpu.VMEM((3, NA_Q, NA_BAND), F32)],
        compiler_params=_params("parallel", "arbitrary"),
        name="na",
    )(p, p, p, pc, pc, col_table)


def _rope_tables(l):
    t = np.arange(l)
    pos_row, pos_col = t // GRID_W, t % GRID_W
    quarter = RET_QK_DIM // 4
    inv_freq = ROPE_BASE ** (-np.arange(quarter, dtype=np.float64) / quarter)
    ang_r = pos_row[:, None] * inv_freq
    ang_c = pos_col[:, None] * inv_freq
    cos = np.concatenate([np.cos(ang_r), np.cos(ang_r), np.cos(ang_c), np.cos(ang_c)], axis=-1)
    sin = np.concatenate([-np.sin(ang_r), np.sin(ang_r), -np.sin(ang_c), np.sin(ang_c)], axis=-1)
    reps = 128 // RET_QK_DIM
    return (jnp.asarray(np.tile(cos, (1, reps)), F32), jnp.asarray(np.tile(sin, (1, reps)), F32))


def _ret_tables_kernel(lgf_ref, lgb_ref, dk_ref, dq_ref, dbi_ref, cd_ref):
    C = RET_CHUNK
    h = pl.program_id(0)
    lgf = lgf_ref[h]
    lgb = lgb_ref[h]
    lane = lax.broadcasted_iota(jnp.int32, (1, 128), 1)
    head_lanes = (lane // RET_QK_DIM) == (h % 2)
    row = lax.broadcasted_iota(jnp.int32, (C, 128), 0).astype(F32)
    diff = (lax.broadcasted_iota(jnp.int32, (C, C), 0) - lax.broadcasted_iota(jnp.int32, (C, C), 1)).astype(F32)
    dk_ref[...] = jnp.where(head_lanes, jnp.exp(lgf * (C - 1.0 - row)), jnp.exp(lgb * row))
    dq_ref[...] = jnp.where(head_lanes, jnp.exp(lgf * (row + 1.0)), jnp.exp(lgb * (C - row)))
    dbi_ref[...] = (jnp.where(diff >= 0, jnp.exp(lgf * jnp.maximum(diff, 0.0)), 0.0)
                    + jnp.where(diff <= 0, jnp.exp(lgb * jnp.maximum(-diff, 0.0)), 0.0))
    sub = lax.broadcasted_iota(jnp.int32, (8, 128), 0)
    cd_ref[...] = jnp.exp(jnp.where(sub == 0, lgf, lgb) * jnp.full((8, 128), float(C), F32))


def _ret_tables(lg_f, lg_b):
    C = RET_CHUNK
    smem = pl.BlockSpec(memory_space=pltpu.SMEM)
    per_head = lambda *shape: pl.BlockSpec((None,) + shape, lambda h: (h,) + (0,) * len(shape))
    return pl.pallas_call(
        _ret_tables_kernel,
        grid=(RET_HEADS,),
        in_specs=[smem, smem],
        out_specs=[per_head(C, 128), per_head(C, 128), per_head(C, C), per_head(8, 128)],
        out_shape=[jax.ShapeDtypeStruct((RET_HEADS, C, 128), F32), jax.ShapeDtypeStruct((RET_HEADS, C, 128), F32),
                   jax.ShapeDtypeStruct((RET_HEADS, C, C), F32), jax.ShapeDtypeStruct((RET_HEADS, 8, 128), F32)],
        compiler_params=_params("arbitrary"),
        name="ret_tables",
    )(lg_f, lg_b)


def _ret_kernel(q_ref, k_ref, v_ref, g_ref, kc_ref, vc_ref, dk_ref, dq_ref, dbi_ref, cd_ref,
                o_ref, a_ref, sb_ref, st_ref, *, n_chunks):
    C = RET_CHUNK
    lane = lax.broadcasted_iota(jnp.int32, (1, 128), 1)
    sub = lax.broadcasted_iota(jnp.int32, (128, 1), 0)

    for e in range(2):
        head_lanes = (lane // RET_QK_DIM) == e
        fwd_rows = (sub // RET_QK_DIM) == e
        vsl = slice(e * RET_V_DIM, (e + 1) * RET_V_DIM)
        dk = dk_ref[e]
        cdf = cd_ref[e, 0:1, :]
        cdb = cd_ref[e, 1:2, :]

        def local_state(k_bf16, v_bf16):
            kf32 = k_bf16.astype(F32)
            k_both = jnp.where(head_lanes, kf32, pltpu.roll(kf32, RET_QK_DIM, 1))
            return _dot((k_both * dk).T.astype(BF16), v_bf16)

        ctx_state = local_state(kc_ref[...], vc_ref[:, vsl])
        for i in range(n_chunks):
            rows = pl.ds(i * C, C)
            a_ref[i] = local_state(k_ref[rows, :], v_ref[rows, vsl])

        s_b = ctx_state
        for i in reversed(range(n_chunks)):
            sb_ref[i] = s_b
            s_b = s_b * cdb + a_ref[i]
        s_f = ctx_state
        for i in range(n_chunks):
            st_ref[e, i] = jnp.where(fwd_rows, s_f, sb_ref[i]).astype(BF16)
            s_f = s_f * cdf + a_ref[i]

    for i in range(n_chunks):
        rows = pl.ds(i * C, C)
        q = q_ref[rows, :].astype(F32)
        q_rolled = pltpu.roll(q, RET_QK_DIM, 1)
        kk = k_ref[rows, :]
        for e in range(2):
            head_lanes = (lane // RET_QK_DIM) == e
            vsl = slice(e * RET_V_DIM, (e + 1) * RET_V_DIM)
            q_own = jnp.where(head_lanes, q, 0.0).astype(BF16)
            q_both = (jnp.where(head_lanes, q, q_rolled) * dq_ref[e]).astype(BF16)
            inner = _dot_nt(q_own, kk) * dbi_ref[e]
            o = _dot(inner.astype(BF16), v_ref[rows, vsl]) + _dot(q_both, st_ref[e, i])
            mu = jnp.mean(o, axis=-1, keepdims=True)
            oc = o - mu
            var = jnp.mean(oc * oc, axis=-1, keepdims=True)
            gate = _silu(g_ref[rows, vsl].astype(F32))
            o_ref[rows, vsl] = (oc * lax.rsqrt(var + EPS) * gate).astype(BF16)


def _retention(p, pc, pc_off_k, pc_off_v, lg_f, lg_b):
    b, l, _ = p.shape
    lc = pc.shape[1]
    assert lc == RET_CHUNK and l % RET_CHUNK == 0
    n_chunks = l // RET_CHUNK
    dk, dq, dbi, cd = _ret_tables(lg_f, lg_b)
    qk_blk = lambda off: (lambda hp, bi: (bi, 0, off // 128 + hp))
    v_blk = lambda off: (lambda hp, bi: (bi, 0, off // 256 + hp))
    pair = lambda *shape: pl.BlockSpec((2,) + shape, lambda hp, bi: (hp,) + (0,) * len(shape))
    return pl.pallas_call(
        functools.partial(_ret_kernel, n_chunks=n_chunks),
        grid=(RET_HEADS // 2, b),
        in_specs=[
            pl.BlockSpec((None, l, 128), qk_blk(OFF_R_Q)),
            pl.BlockSpec((None, l, 128), qk_blk(OFF_R_K)),
            pl.BlockSpec((None, l, 256), v_blk(OFF_R_V)),
            pl.BlockSpec((None, l, 256), v_blk(OFF_R_G)),
            pl.BlockSpec((None, lc, 128), qk_blk(pc_off_k)),
            pl.BlockSpec((None, lc, 256), v_blk(pc_off_v)),
            pair(RET_CHUNK, 128), pair(RET_CHUNK, 128), pair(RET_CHUNK, RET_CHUNK), pair(8, 128),
        ],
        out_specs=pl.BlockSpec((None, l, 256), lambda hp, bi: (bi, 0, hp)),
        out_shape=jax.ShapeDtypeStruct((b, l, RET_V_WIDTH), BF16),
        scratch_shapes=[
            pltpu.VMEM((n_chunks, 128, RET_V_DIM), F32),
            pltpu.VMEM((n_chunks, 128, RET_V_DIM), F32),
            pltpu.VMEM((2, n_chunks, 128, RET_V_DIM), BF16),
        ],
        compiler_params=_params("parallel", "parallel"),
        name="ret",
    )(p, p, p, p, pc, pc, dk, dq, dbi, cd)


OUTPROJ_SUB_ROWS = 256


def _outproj_kernel(na_ref, ret_ref, wt_ref, wb_ref, x_ref, g1_ref, sh2_ref, sc2_ref, npost_ref, npre_ref,
                    xn_ref, h2_ref):
    gate_gain = g1_ref[...] * npost_ref[...]
    gain2 = npre_ref[...] * (1.0 + sc2_ref[...])
    shift2 = sh2_ref[...]
    for s in range(x_ref.shape[0] // OUTPROJ_SUB_ROWS):
        rows = pl.ds(s * OUTPROJ_SUB_ROWS, OUTPROJ_SUB_ROWS)
        mix = _dot(na_ref[rows, :], wt_ref[...]) + _dot(ret_ref[rows, :], wb_ref[...])
        ms = jnp.mean(mix * mix, axis=-1, keepdims=True)
        xn = x_ref[rows, :] + mix * lax.rsqrt(ms + EPS) * gate_gain
        xn_ref[rows, :] = xn
        ms2 = jnp.mean(xn * xn, axis=-1, keepdims=True)
        h2_ref[rows, :] = (xn * lax.rsqrt(ms2 + EPS) * gain2 + shift2).astype(BF16)


def _outproj(na, ret, w_out_bf16, x2, mod3, norm_post, norm_pre, *, rows_per_mod, tm):
    m, d = x2.shape
    kh = na.shape[1]
    mod_row = lambda i: (i * tm) // rows_per_mod
    mod_blk = lambda k: pl.BlockSpec((None, 1, d), lambda i: (mod_row(i), 0, k))
    return pl.pallas_call(
        _outproj_kernel,
        grid=(m // tm,),
        in_specs=[
            pl.BlockSpec((tm, kh), lambda i: (i, 0)),
            pl.BlockSpec((tm, kh), lambda i: (i, 0)),
            pl.BlockSpec((kh, d), lambda i: (0, 0)),
            pl.BlockSpec((kh, d), lambda i: (1, 0)),
            pl.BlockSpec((tm, d), lambda i: (i, 0)),
            mod_blk(2), mod_blk(3), mod_blk(4),
            pl.BlockSpec((1, d), lambda i: (0, 0)),
            pl.BlockSpec((1, d), lambda i: (0, 0)),
        ],
        out_specs=[pl.BlockSpec((tm, d), lambda i: (i, 0)), pl.BlockSpec((tm, d), lambda i: (i, 0))],
        out_shape=[jax.ShapeDtypeStruct((m, d), F32), jax.ShapeDtypeStruct((m, d), BF16)],
        compiler_params=_params("parallel"),
        name="outproj",
    )(na, ret, w_out_bf16, w_out_bf16, x2, mod3, mod3, mod3, norm_post, norm_pre)


FFN_SUB_ROWS = 256


def _ffn_kernel(h_ref, wg_ref, wu_ref, wd_ref, xn_hbm, g2_ref, npost_ref, o_ref, xn_buf, xn_sem):
    i = pl.program_id(0)
    f = pl.program_id(1)
    tm = h_ref.shape[0]

    def residual_copy():
        return pltpu.make_async_copy(xn_hbm.at[pl.ds(i * tm, tm), :], xn_buf, xn_sem)

    @pl.when(f == 0)
    def _():
        residual_copy().start()
        o_ref[...] = jnp.zeros_like(o_ref)

    h = h_ref[...]
    t = (_silu(_dot(h, wg_ref[...])) * _dot(h, wu_ref[...])).astype(BF16)
    o_ref[...] += _dot(t, wd_ref[...])

    @pl.when(f == pl.num_programs(1) - 1)
    def _():
        residual_copy().wait()
        gate_gain = g2_ref[...] * npost_ref[...]
        for s in range(tm // FFN_SUB_ROWS):
            rows = pl.ds(s * FFN_SUB_ROWS, FFN_SUB_ROWS)
            y = o_ref[rows, :]
            ms = jnp.mean(y * y, axis=-1, keepdims=True)
            o_ref[rows, :] = xn_buf[rows, :] + y * lax.rsqrt(ms + EPS) * gate_gain


def _ffn(h2, wg, wu, wd, xn, mod3, norm_post, *, rows_per_mod, tm, tf):
    m, d = h2.shape
    dff = wd.shape[0]
    mod_row = lambda i: (i * tm) // rows_per_mod
    return pl.pallas_call(
        _ffn_kernel,
        grid=(m // tm, dff // tf),
        in_specs=[
            pl.BlockSpec((tm, d), lambda i, f: (i, 0)),
            pl.BlockSpec((d, tf), lambda i, f: (0, f)),
            pl.BlockSpec((d, tf), lambda i, f: (0, f)),
            pl.BlockSpec((tf, d), lambda i, f: (f, 0)),
            pl.BlockSpec(memory_space=pl.ANY),
            pl.BlockSpec((None, 1, d), lambda i, f: (mod_row(i), 0, 5)),
            pl.BlockSpec((1, d), lambda i, f: (0, 0)),
        ],
        out_specs=pl.BlockSpec((tm, d), lambda i, f: (i, 0)),
        out_shape=jax.ShapeDtypeStruct((m, d), F32),
        scratch_shapes=[pltpu.VMEM((tm, d), F32), pltpu.SemaphoreType.DMA(())],
        compiler_params=_params("parallel", "arbitrary"),
        name="ffn",
    )(h2, wg, wu, wd, xn, mod3, norm_post)


def kernel(x, c, ctx, c_ctx, ada_w, ada_b, norm_pre_mix, norm_post_mix, norm_pre_ffn, norm_post_ffn, w_in,
           na_rpb, ret_log_gamma_fwd, ret_log_gamma_bwd, w_out, w_gate, w_up, w_down):
    b, l, d = x.shape
    lc = ctx.shape[1]
    depth = ada_w.shape[0]
    assert depth == 1, "the context-stream update between layers is not implemented"
    li = 0

    mod_rows = -(-(b + 1) // 8) * 8
    cc = jnp.concatenate([c, c_ctx[None, :], jnp.zeros((mod_rows - b - 1, d), F32)], axis=0)
    mod3 = _adaln(cc, ada_w[li], ada_b[li]).reshape(mod_rows, 1, 6 * d)

    w_in_b = w_in[li].astype(BF16)
    x2 = x.reshape(b * l, d)
    ctx2 = ctx.reshape(b * lc, d)
    gain_pre = norm_pre_mix[li].reshape(1, d)

    p = _inproj(x2, gain_pre, mod3, w_in_b, rows_per_mod=l, mod_row0=0,
                col_tiles=tuple(range(IN_WIDTH // 1024)), tm=1024, tn=1024,
                first_tile_scale=NA_HEAD_DIM ** -0.5, rope_tile=OFF_R_Q // 1024, rope_tables=_rope_tables(l))
    tn = 512
    ctx_tiles = tuple(range(OFF_NA_K // tn, OFF_R_Q // tn)) + tuple(range(OFF_R_K // tn, OFF_R_G // tn))
    pc = _inproj(ctx2, gain_pre, mod3, w_in_b, rows_per_mod=b * lc, mod_row0=b,
                 col_tiles=ctx_tiles, tm=1024, tn=tn)
    p = p.reshape(b, l, IN_WIDTH)
    pc = pc.reshape(b, lc, len(ctx_tiles) * tn)
    pc_na_k, pc_na_v = 0, NA_WIDTH
    pc_r_k, pc_r_v = 2 * NA_WIDTH, 2 * NA_WIDTH + RET_QK_WIDTH

    o_na = _neighbourhood_attention(p, pc, pc_na_k, pc_na_v, _na_col_table(na_rpb[li]))
    o_ret = _retention(p, pc, pc_r_k, pc_r_v, ret_log_gamma_fwd[li].astype(F32), ret_log_gamma_bwd[li].astype(F32))

    x_new, h2 = _outproj(o_na.reshape(b * l, NA_WIDTH), o_ret.reshape(b * l, RET_V_WIDTH), w_out[li].astype(BF16),
                         x2, mod3, norm_post_mix[li].reshape(1, d), norm_pre_ffn[li].reshape(1, d),
                         rows_per_mod=l, tm=512)
    out = _ffn(h2, w_gate[li].astype(BF16), w_up[li].astype(BF16), w_down[li].astype(BF16), x_new, mod3,
               norm_post_ffn[li].reshape(1, d), rows_per_mod=l, tm=1024, tf=512)
    return out.reshape(b, l, d)
```

```python
import functools

import numpy as np
import jax
import jax.numpy as jnp
from jax import lax
from jax.experimental import pallas as pl
from jax.experimental.pallas import tpu as pltpu

F32 = jnp.float32
BF16 = jnp.bfloat16

D_MODEL = 2048
GRID_W = 64
NA_HEADS = 8
NA_HEAD_DIM = 128
NA_WIN_H = 8
NA_WIN_W = 16
RET_HEADS = 8
RET_QK_DIM = 64
RET_V_DIM = 128
NA_WIDTH = NA_HEADS * NA_HEAD_DIM
RET_QK_WIDTH = RET_HEADS * RET_QK_DIM
RET_V_WIDTH = RET_HEADS * RET_V_DIM
IN_WIDTH = 3 * NA_WIDTH + 2 * RET_QK_WIDTH + 2 * RET_V_WIDTH
ROPE_BASE = 10000.0
EPS = 1e-6
MASK_VALUE = -1e30
LOG2_E = float(np.log2(np.e))

OFF_NA_Q = 0
OFF_NA_K = NA_WIDTH
OFF_NA_V = 2 * NA_WIDTH
OFF_R_Q = 3 * NA_WIDTH
OFF_R_K = OFF_R_Q + RET_QK_WIDTH
OFF_R_V = OFF_R_K + RET_QK_WIDTH
OFF_R_G = OFF_R_V + RET_V_WIDTH

VMEM_LIMIT_BYTES = 56 * 1024 * 1024

RET_CHUNK = 256


def _params(*semantics):
    return pltpu.CompilerParams(dimension_semantics=semantics, vmem_limit_bytes=VMEM_LIMIT_BYTES)


def _dot(a, b):
    return jnp.dot(a, b, preferred_element_type=F32)


def _dot_nt(a, b):
    return lax.dot_general(a, b, (((1,), (1,)), ((), ())), preferred_element_type=F32)


def _silu(x):
    return x * jax.nn.sigmoid(x)


def _adaln_kernel(c_ref, w_ref, b_ref, o_ref):
    s = _silu(c_ref[...]).astype(BF16)
    o_ref[...] = _dot(s, w_ref[...].astype(BF16)) + b_ref[...]


def _adaln(cc, ada_w, ada_b, tn=1024):
    rows, d = cc.shape
    n = ada_w.shape[1]
    return pl.pallas_call(
        _adaln_kernel,
        grid=(n // tn,),
        in_specs=[
            pl.BlockSpec((rows, d), lambda j: (0, 0)),
            pl.BlockSpec((d, tn), lambda j: (0, j)),
            pl.BlockSpec((1, tn), lambda j: (0, j)),
        ],
        out_specs=pl.BlockSpec((rows, tn), lambda j: (0, j)),
        out_shape=jax.ShapeDtypeStruct((rows, n), F32),
        compiler_params=_params("arbitrary"),
        name="adaln",
    )(cc, ada_w, ada_b.reshape(1, n))


INPROJ_SUB_ROWS = 256


def _rope_rotate(x, cos, sin_signed):
    lane = lax.broadcasted_iota(jnp.int32, (1, 128), 1)
    first_half = (lane % (RET_QK_DIM // 2)) < (RET_QK_DIM // 4)
    partner = jnp.where(first_half, pltpu.roll(x, 128 - RET_QK_DIM // 4, 1), pltpu.roll(x, RET_QK_DIM // 4, 1))
    return x * cos + partner * sin_signed


def _inproj_kernel(x_ref, g_ref, sh_ref, sc_ref, w_ref, *refs, first_tile_scale, rope_tile):
    if rope_tile is None:
        o_ref, h_ref = refs
    else:
        cos_ref, sin_ref, o_ref, h_ref = refs
    j = pl.program_id(1)
    n_slices = x_ref.shape[0] // INPROJ_SUB_ROWS

    @pl.when(j == 0)
    def _():
        gain = g_ref[...] * (1.0 + sc_ref[...])
        shift = sh_ref[...]
        for s in range(n_slices):
            rows = pl.ds(s * INPROJ_SUB_ROWS, INPROJ_SUB_ROWS)
            x = x_ref[rows, :]
            ms = jnp.mean(x * x, axis=-1, keepdims=True)
            h = (x * lax.rsqrt(ms + EPS) * gain + shift).astype(BF16)
            h_ref[rows, :] = h
            res = _dot(h, w_ref[...])
            if first_tile_scale != 1.0:
                res = res * first_tile_scale
            o_ref[rows, :] = res.astype(BF16)

    plain = (j > 0) if rope_tile is None else ((j > 0) & (j != rope_tile))

    @pl.when(plain)
    def _():
        o_ref[...] = _dot(h_ref[...], w_ref[...]).astype(BF16)

    if rope_tile is not None:
        @pl.when(j == rope_tile)
        def _():
            q_scale = RET_QK_DIM ** -0.5
            for s in range(n_slices):
                rows = pl.ds(s * INPROJ_SUB_ROWS, INPROJ_SUB_ROWS)
                res = _dot(h_ref[rows, :], w_ref[...])
                cos, sin = cos_ref[rows, :], sin_ref[rows, :]
                cos_q, sin_q = cos * q_scale, sin * q_scale
                for cg in range(o_ref.shape[1] // 128):
                    lanes = slice(cg * 128, (cg + 1) * 128)
                    is_q = cg < RET_QK_WIDTH // 128
                    y = _rope_rotate(res[:, lanes], cos_q if is_q else cos, sin_q if is_q else sin)
                    o_ref[rows, lanes] = y.astype(BF16)


def _inproj(x2, gain, mod3, w_bf16, *, rows_per_mod, mod_row0, col_tiles, tm, tn,
            first_tile_scale=1.0, rope_tile=None, rope_tables=None):
    m, d = x2.shape
    n_out = len(col_tiles) * tn
    first = col_tiles[0]
    gap_at = next((i for i in range(1, len(col_tiles)) if col_tiles[i] != col_tiles[i - 1] + 1), None)
    if gap_at is None:
        w_map = lambda i, j: (0, j + first)
    else:
        gap = col_tiles[gap_at] - col_tiles[gap_at - 1] - 1
        assert all(col_tiles[i] == first + i + (gap if i >= gap_at else 0) for i in range(len(col_tiles)))
        w_map = lambda i, j: (0, j + first + jnp.where(j >= gap_at, gap, 0))
    mod_row = lambda i: mod_row0 + (i * tm) // rows_per_mod
    in_specs = [
        pl.BlockSpec((tm, d), lambda i, j: (i, 0)),
        pl.BlockSpec((1, d), lambda i, j: (0, 0)),
        pl.BlockSpec((None, 1, d), lambda i, j: (mod_row(i), 0, 0)),
        pl.BlockSpec((None, 1, d), lambda i, j: (mod_row(i), 0, 1)),
        pl.BlockSpec((d, tn), w_map),
    ]
    operands = [x2, gain, mod3, mod3, w_bf16]
    if rope_tile is not None:
        assert tn == 2 * RET_QK_WIDTH and col_tiles[rope_tile] * tn == OFF_R_Q and rows_per_mod % tm == 0
        pos_blk = pl.BlockSpec((tm, 128), lambda i, j: (i % (rows_per_mod // tm), 0))
        in_specs += [pos_blk, pos_blk]
        operands += list(rope_tables)
    return pl.pallas_call(
        functools.partial(_inproj_kernel, first_tile_scale=first_tile_scale, rope_tile=rope_tile),
        grid=(m // tm, len(col_tiles)),
        in_specs=in_specs,
        out_specs=pl.BlockSpec((tm, tn), lambda i, j: (i, j)),
        out_shape=jax.ShapeDtypeStruct((m, n_out), BF16),
        scratch_shapes=[pltpu.VMEM((tm, d), BF16)],
        compiler_params=_params("parallel", "arbitrary"),
        name="inproj",
    )(*operands)


NA_GROUP_ROWS = 4
NA_BAND_ROWS = NA_GROUP_ROWS + NA_WIN_H
NA_Q = NA_GROUP_ROWS * GRID_W
NA_BAND = NA_BAND_ROWS * GRID_W


def _na_band_start(g, rows):
    lo = g * NA_GROUP_ROWS - NA_WIN_H // 2
    if isinstance(g, (int, np.integer)):
        return min(max(lo, 0), rows - NA_BAND_ROWS)
    return jnp.minimum(jnp.maximum(lo, 0), rows - NA_BAND_ROWS)


def _na_row_patterns(rows):
    groups = rows // NA_GROUP_ROWS
    assert rows % NA_GROUP_ROWS == 0 and rows >= NA_BAND_ROWS and groups >= 3

    def row_pattern(g):
        r = g * NA_GROUP_ROWS + np.arange(NA_GROUP_ROWS)[:, None]
        krow = _na_band_start(g, rows) + np.arange(NA_BAND_ROWS)[None, :]
        rs = np.clip(r - NA_WIN_H // 2, 0, rows - NA_WIN_H)
        ok = (krow >= rs) & (krow < rs + NA_WIN_H)
        return ok, np.clip(krow - r + NA_WIN_H - 1, 0, 2 * NA_WIN_H - 2)

    pats = [row_pattern(0), row_pattern(1), row_pattern(groups - 1)]
    for g in range(1, groups - 1):
        ok, dr = row_pattern(g)
        assert (ok == pats[1][0]).all() and (dr[ok] == pats[1][1][ok]).all()
    return pats


NA_SCORE_AHEAD = 4


def _na_kernel(rpb_ref, q_ref, k_ref, v_ref, kc_ref, vc_ref, o_ref, s_ref, bias_ref, va_ref, vca_ref, *, rows):
    groups = rows // NA_GROUP_ROWS
    hd = NA_HEAD_DIM

    @pl.when(pl.program_id(1) == 0)
    def _():
        n_dr, n_dc = 2 * NA_WIN_H - 1, 2 * NA_WIN_W - 1
        base = pl.program_id(0) * (n_dr * n_dc)
        c = lax.broadcasted_iota(jnp.int32, (GRID_W, GRID_W), 0)
        kc = lax.broadcasted_iota(jnp.int32, (GRID_W, GRID_W), 1)
        col_start = jnp.minimum(jnp.maximum(c - NA_WIN_W // 2, 0), GRID_W - NA_WIN_W)
        col_ok = (kc >= col_start) & (kc < col_start + NA_WIN_W)
        dc = kc - c + (NA_WIN_W - 1)
        masked = jnp.full((GRID_W, GRID_W), MASK_VALUE, F32)
        pats = _na_row_patterns(rows)
        used = sorted({int(dr[a, j]) for ok, dr in pats for a, j in zip(*np.nonzero(ok))})
        tiles = {}
        for d in used:
            t = jnp.zeros((GRID_W, GRID_W), F32)
            for x in range(n_dc):
                t = jnp.where(dc == x, rpb_ref[base + d * n_dc + x], t)
            tiles[d] = jnp.where(col_ok, t * LOG2_E, MASK_VALUE)
        for pi, (ok, dr) in enumerate(pats):
            for a in range(NA_GROUP_ROWS):
                for j in range(NA_BAND_ROWS):
                    tile = tiles[int(dr[a, j])] if ok[a, j] else masked
                    bias_ref[pi, a * GRID_W:(a + 1) * GRID_W, j * GRID_W:(j + 1) * GRID_W] = tile

    va_ref[:, :hd] = v_ref[...]
    va_ref[:, hd:] = jnp.ones((va_ref.shape[0], hd), BF16)
    vca_ref[:, :hd] = vc_ref[...]
    vca_ref[:, hd:] = jnp.ones((vca_ref.shape[0], hd), BF16)

    def scores(g, s_ref):
        q0 = g * NA_Q
        k0 = _na_band_start(g, rows) * GRID_W
        pattern = 0 if g == 0 else (2 if g == groups - 1 else 1)
        q = q_ref[pl.ds(q0, NA_Q), :]
        s_ref[:, :NA_BAND] = _dot_nt(q, k_ref[pl.ds(k0, NA_BAND), :]) + bias_ref[pattern]
        s_ref[:, NA_BAND:] = _dot_nt(q, kc_ref[...])

    def attend(g, s_ref):
        q0 = g * NA_Q
        k0 = _na_band_start(g, rows) * GRID_W
        s = s_ref[...]
        pb = jnp.exp2(s - jnp.max(s, axis=-1, keepdims=True)).astype(BF16)
        o = _dot(pb[:, :NA_BAND], va_ref[pl.ds(k0, NA_BAND), :]) + _dot(pb[:, NA_BAND:], vca_ref[...])
        o_ref[pl.ds(q0, NA_Q), :] = (o[:, :hd] / o[:, hd:hd + 1]).astype(BF16)

    for g in range(min(NA_SCORE_AHEAD, groups)):
        scores(g, s_ref.at[g])
    for g in range(groups):
        if g + NA_SCORE_AHEAD < groups:
            scores(g + NA_SCORE_AHEAD, s_ref.at[g + NA_SCORE_AHEAD])
        attend(g, s_ref.at[g])


def _neighbourhood_attention(p, pc, pc_off_k, pc_off_v, rpb):
    b, l, _ = p.shape
    lc = pc.shape[1]
    rows = l // GRID_W
    hd = NA_HEAD_DIM
    assert rpb.shape == (NA_HEADS, 2 * NA_WIN_H - 1, 2 * NA_WIN_W - 1)
    blk = lambda off: (lambda h, bi: (bi, 0, off // hd + h))
    return pl.pallas_call(
        functools.partial(_na_kernel, rows=rows),
        grid=(NA_HEADS, b),
        in_specs=[
            pl.BlockSpec(memory_space=pltpu.SMEM),
            pl.BlockSpec((None, l, hd), blk(OFF_NA_Q)),
            pl.BlockSpec((None, l, hd), blk(OFF_NA_K)),
            pl.BlockSpec((None, l, hd), blk(OFF_NA_V)),
            pl.BlockSpec((None, lc, hd), blk(pc_off_k)),
            pl.BlockSpec((None, lc, hd), blk(pc_off_v)),
        ],
        out_specs=pl.BlockSpec((None, l, hd), lambda h, bi: (bi, 0, h)),
        out_shape=jax.ShapeDtypeStruct((b, l, NA_WIDTH), BF16),
        scratch_shapes=[pltpu.VMEM((rows // NA_GROUP_ROWS, NA_Q, NA_BAND + lc), F32),
                        pltpu.VMEM((3, NA_Q, NA_BAND), F32),
                        pltpu.VMEM((l, 2 * hd), BF16),
                        pltpu.VMEM((lc, 2 * hd), BF16)],
        compiler_params=_params("parallel", "arbitrary"),
        name="na",
    )(rpb.astype(F32).reshape(-1), p, p, p, pc, pc)


def _rope_tables(l):
    t = np.arange(l)
    pos_row, pos_col = t // GRID_W, t % GRID_W
    quarter = RET_QK_DIM // 4
    inv_freq = ROPE_BASE ** (-np.arange(quarter, dtype=np.float64) / quarter)
    ang_r = pos_row[:, None] * inv_freq
    ang_c = pos_col[:, None] * inv_freq
    cos = np.concatenate([np.cos(ang_r), np.cos(ang_r), np.cos(ang_c), np.cos(ang_c)], axis=-1)
    sin = np.concatenate([-np.sin(ang_r), np.sin(ang_r), -np.sin(ang_c), np.sin(ang_c)], axis=-1)
    reps = 128 // RET_QK_DIM
    return (jnp.asarray(np.tile(cos, (1, reps)), F32), jnp.asarray(np.tile(sin, (1, reps)), F32))


def _ret_tables_kernel(lgf_ref, lgb_ref, dk_ref, dq_ref, dbi_ref, cd_ref):
    C = RET_CHUNK
    h = pl.program_id(0)
    lgf = lgf_ref[h]
    lgb = lgb_ref[h]
    lane = lax.broadcasted_iota(jnp.int32, (1, 128), 1)
    head_lanes = (lane // RET_QK_DIM) == (h % 2)
    row = lax.broadcasted_iota(jnp.int32, (C, 128), 0).astype(F32)
    diff = (lax.broadcasted_iota(jnp.int32, (C, C), 0) - lax.broadcasted_iota(jnp.int32, (C, C), 1)).astype(F32)
    dk_ref[...] = jnp.where(head_lanes, jnp.exp(lgf * (C - 1.0 - row)), jnp.exp(lgb * row))
    dq_ref[...] = jnp.where(head_lanes, jnp.exp(lgf * (row + 1.0)), jnp.exp(lgb * (C - row)))
    dbi_ref[...] = (jnp.where(diff >= 0, jnp.exp(lgf * jnp.maximum(diff, 0.0)), 0.0)
                    + jnp.where(diff <= 0, jnp.exp(lgb * jnp.maximum(-diff, 0.0)), 0.0))
    sub = lax.broadcasted_iota(jnp.int32, (8, 128), 0)
    cd_ref[...] = jnp.exp(jnp.where(sub == 0, lgf, lgb) * jnp.full((8, 128), float(C), F32))


def _ret_tables(lg_f, lg_b):
    C = RET_CHUNK
    smem = pl.BlockSpec(memory_space=pltpu.SMEM)
    per_head = lambda *shape: pl.BlockSpec((None,) + shape, lambda h: (h,) + (0,) * len(shape))
    return pl.pallas_call(
        _ret_tables_kernel,
        grid=(RET_HEADS,),
        in_specs=[smem, smem],
        out_specs=[per_head(C, 128), per_head(C, 128), per_head(C, C), per_head(8, 128)],
        out_shape=[jax.ShapeDtypeStruct((RET_HEADS, C, 128), F32), jax.ShapeDtypeStruct((RET_HEADS, C, 128), F32),
                   jax.ShapeDtypeStruct((RET_HEADS, C, C), F32), jax.ShapeDtypeStruct((RET_HEADS, 8, 128), F32)],
        compiler_params=_params("arbitrary"),
        name="ret_tables",
    )(lg_f, lg_b)


def _ret_kernel(q_ref, k_ref, v_ref, g_ref, kc_ref, vc_ref, dk_ref, dq_ref, dbi_ref, cd_ref,
                o_ref, a_ref, sb_ref, st_ref, *, n_chunks):
    C = RET_CHUNK
    lane = lax.broadcasted_iota(jnp.int32, (1, 128), 1)
    sub = lax.broadcasted_iota(jnp.int32, (128, 1), 0)

    for e in range(2):
        head_lanes = (lane // RET_QK_DIM) == e
        fwd_rows = (sub // RET_QK_DIM) == e
        vsl = slice(e * RET_V_DIM, (e + 1) * RET_V_DIM)
        dk = dk_ref[e]
        cdf = cd_ref[e, 0:1, :]
        cdb = cd_ref[e, 1:2, :]

        def local_state(k_bf16, v_bf16):
            kf32 = k_bf16.astype(F32)
            k_both = jnp.where(head_lanes, kf32, pltpu.roll(kf32, RET_QK_DIM, 1))
            return _dot((k_both * dk).T.astype(BF16), v_bf16)

        ctx_state = local_state(kc_ref[...], vc_ref[:, vsl])
        for i in range(n_chunks):
            rows = pl.ds(i * C, C)
            a_ref[i] = local_state(k_ref[rows, :], v_ref[rows, vsl])

        s_b = ctx_state
        for i in reversed(range(n_chunks)):
            sb_ref[i] = s_b
            s_b = s_b * cdb + a_ref[i]
        s_f = ctx_state
        for i in range(n_chunks):
            st_ref[e, i] = jnp.where(fwd_rows, s_f, sb_ref[i]).astype(BF16)
            s_f = s_f * cdf + a_ref[i]

    for i in range(n_chunks):
        rows = pl.ds(i * C, C)
        q = q_ref[rows, :].astype(F32)
        q_rolled = pltpu.roll(q, RET_QK_DIM, 1)
        kk = k_ref[rows, :]
        for e in range(2):
            head_lanes = (lane // RET_QK_DIM) == e
            vsl = slice(e * RET_V_DIM, (e + 1) * RET_V_DIM)
            q_own = jnp.where(head_lanes, q, 0.0).astype(BF16)
            q_both = (jnp.where(head_lanes, q, q_rolled) * dq_ref[e]).astype(BF16)
            inner = _dot_nt(q_own, kk) * dbi_ref[e]
            o = _dot(inner.astype(BF16), v_ref[rows, vsl]) + _dot(q_both, st_ref[e, i])
            mu = jnp.mean(o, axis=-1, keepdims=True)
            oc = o - mu
            var = jnp.mean(oc * oc, axis=-1, keepdims=True)
            gate = _silu(g_ref[rows, vsl].astype(F32))
            o_ref[rows, vsl] = (oc * lax.rsqrt(var + EPS) * gate).astype(BF16)


def _retention(p, pc, pc_off_k, pc_off_v, lg_f, lg_b):
    b, l, _ = p.shape
    lc = pc.shape[1]
    assert lc == RET_CHUNK and l % RET_CHUNK == 0
    n_chunks = l // RET_CHUNK
    dk, dq, dbi, cd = _ret_tables(lg_f, lg_b)
    qk_blk = lambda off: (lambda hp, bi: (bi, 0, off // 128 + hp))
    v_blk = lambda off: (lambda hp, bi: (bi, 0, off // 256 + hp))
    pair = lambda *shape: pl.BlockSpec((2,) + shape, lambda hp, bi: (hp,) + (0,) * len(shape))
    return pl.pallas_call(
        functools.partial(_ret_kernel, n_chunks=n_chunks),
        grid=(RET_HEADS // 2, b),
        in_specs=[
            pl.BlockSpec((None, l, 128), qk_blk(OFF_R_Q)),
            pl.BlockSpec((None, l, 128), qk_blk(OFF_R_K)),
            pl.BlockSpec((None, l, 256), v_blk(OFF_R_V)),
            pl.BlockSpec((None, l, 256), v_blk(OFF_R_G)),
            pl.BlockSpec((None, lc, 128), qk_blk(pc_off_k)),
            pl.BlockSpec((None, lc, 256), v_blk(pc_off_v)),
            pair(RET_CHUNK, 128), pair(RET_CHUNK, 128), pair(RET_CHUNK, RET_CHUNK), pair(8, 128),
        ],
        out_specs=pl.BlockSpec((None, l, 256), lambda hp, bi: (bi, 0, hp)),
        out_shape=jax.ShapeDtypeStruct((b, l, RET_V_WIDTH), BF16),
        scratch_shapes=[
            pltpu.VMEM((n_chunks, 128, RET_V_DIM), F32),
            pltpu.VMEM((n_chunks, 128, RET_V_DIM), F32),
            pltpu.VMEM((2, n_chunks, 128, RET_V_DIM), BF16),
        ],
        compiler_params=_params("parallel", "parallel"),
        name="ret",
    )(p, p, p, p, pc, pc, dk, dq, dbi, cd)


OUTPROJ_SUB_ROWS = 256


def _outproj_kernel(na_ref, ret_ref, wt_ref, wb_ref, x_ref, g1_ref, sh2_ref, sc2_ref, npost_ref, npre_ref,
                    xn_ref, h2_ref):
    gate_gain = g1_ref[...] * npost_ref[...]
    gain2 = npre_ref[...] * (1.0 + sc2_ref[...])
    shift2 = sh2_ref[...]
    for s in range(x_ref.shape[0] // OUTPROJ_SUB_ROWS):
        rows = pl.ds(s * OUTPROJ_SUB_ROWS, OUTPROJ_SUB_ROWS)
        mix = _dot(na_ref[rows, :], wt_ref[...]) + _dot(ret_ref[rows, :], wb_ref[...])
        ms = jnp.mean(mix * mix, axis=-1, keepdims=True)
        xn = x_ref[rows, :] + mix * lax.rsqrt(ms + EPS) * gate_gain
        xn_ref[rows, :] = xn
        ms2 = jnp.mean(xn * xn, axis=-1, keepdims=True)
        h2_ref[rows, :] = (xn * lax.rsqrt(ms2 + EPS) * gain2 + shift2).astype(BF16)


def _outproj(na, ret, w_out_bf16, x2, mod3, norm_post, norm_pre, *, rows_per_mod, tm):
    m, d = x2.shape
    kh = na.shape[1]
    mod_row = lambda i: (i * tm) // rows_per_mod
    mod_blk = lambda k: pl.BlockSpec((None, 1, d), lambda i: (mod_row(i), 0, k))
    return pl.pallas_call(
        _outproj_kernel,
        grid=(m // tm,),
        in_specs=[
            pl.BlockSpec((tm, kh), lambda i: (i, 0)),
            pl.BlockSpec((tm, kh), lambda i: (i, 0)),
            pl.BlockSpec((kh, d), lambda i: (0, 0)),
            pl.BlockSpec((kh, d), lambda i: (1, 0)),
            pl.BlockSpec((tm, d), lambda i: (i, 0)),
            mod_blk(2), mod_blk(3), mod_blk(4),
            pl.BlockSpec((1, d), lambda i: (0, 0)),
            pl.BlockSpec((1, d), lambda i: (0, 0)),
        ],
        out_specs=[pl.BlockSpec((tm, d), lambda i: (i, 0)), pl.BlockSpec((tm, d), lambda i: (i, 0))],
        out_shape=[jax.ShapeDtypeStruct((m, d), F32), jax.ShapeDtypeStruct((m, d), BF16)],
        compiler_params=_params("parallel"),
        name="outproj",
    )(na, ret, w_out_bf16, w_out_bf16, x2, mod3, mod3, mod3, norm_post, norm_pre)


FFN_SUB_ROWS = 256


def _ffn_kernel(h_ref, wg_ref, wu_ref, wd_ref, xn_hbm, g2_ref, npost_ref, o_ref, xn_buf, xn_sem):
    i = pl.program_id(0)
    f = pl.program_id(1)
    tm = h_ref.shape[0]

    def residual_copy():
        return pltpu.make_async_copy(xn_hbm.at[pl.ds(i * tm, tm), :], xn_buf, xn_sem)

    @pl.when(f == 0)
    def _():
        residual_copy().start()
        o_ref[...] = jnp.zeros_like(o_ref)

    h = h_ref[...]
    t = (_silu(_dot(h, wg_ref[...])) * _dot(h, wu_ref[...])).astype(BF16)
    o_ref[...] += _dot(t, wd_ref[...])

    @pl.when(f == pl.num_programs(1) - 1)
    def _():
        residual_copy().wait()
        gate_gain = g2_ref[...] * npost_ref[...]
        for s in range(tm // FFN_SUB_ROWS):
            rows = pl.ds(s * FFN_SUB_ROWS, FFN_SUB_ROWS)
            y = o_ref[rows, :]
            ms = jnp.mean(y * y, axis=-1, keepdims=True)
            o_ref[rows, :] = xn_buf[rows, :] + y * lax.rsqrt(ms + EPS) * gate_gain


def _ffn(h2, wg, wu, wd, xn, mod3, norm_post, *, rows_per_mod, tm, tf):
    m, d = h2.shape
    dff = wd.shape[0]
    mod_row = lambda i: (i * tm) // rows_per_mod
    return pl.pallas_call(
        _ffn_kernel,
        grid=(m // tm, dff // tf),
        in_specs=[
            pl.BlockSpec((tm, d), lambda i, f: (i, 0)),
            pl.BlockSpec((d, tf), lambda i, f: (0, f)),
            pl.BlockSpec((d, tf), lambda i, f: (0, f)),
            pl.BlockSpec((tf, d), lambda i, f: (f, 0)),
            pl.BlockSpec(memory_space=pl.ANY),
            pl.BlockSpec((None, 1, d), lambda i, f: (mod_row(i), 0, 5)),
            pl.BlockSpec((1, d), lambda i, f: (0, 0)),
        ],
        out_specs=pl.BlockSpec((tm, d), lambda i, f: (i, 0)),
        out_shape=jax.ShapeDtypeStruct((m, d), F32),
        scratch_shapes=[pltpu.VMEM((tm, d), F32), pltpu.SemaphoreType.DMA(())],
        compiler_params=_params("parallel", "arbitrary"),
        name="ffn",
    )(h2, wg, wu, wd, xn, mod3, norm_post)


def kernel(x, c, ctx, c_ctx, ada_w, ada_b, norm_pre_mix, norm_post_mix, norm_pre_ffn, norm_post_ffn, w_in,
           na_rpb, ret_log_gamma_fwd, ret_log_gamma_bwd, w_out, w_gate, w_up, w_down):
    b, l, d = x.shape
    lc = ctx.shape[1]
    depth = ada_w.shape[0]
    assert depth == 1, "the context-stream update between layers is not implemented"
    li = 0

    mod_rows = -(-(b + 1) // 8) * 8
    cc = jnp.concatenate([c, c_ctx[None, :], jnp.zeros((mod_rows - b - 1, d), F32)], axis=0)
    mod3 = _adaln(cc, ada_w[li], ada_b[li]).reshape(mod_rows, 1, 6 * d)

    w_in_b = w_in[li].astype(BF16)
    x2 = x.reshape(b * l, d)
    ctx2 = ctx.reshape(b * lc, d)
    gain_pre = norm_pre_mix[li].reshape(1, d)

    p = _inproj(x2, gain_pre, mod3, w_in_b, rows_per_mod=l, mod_row0=0,
                col_tiles=tuple(range(IN_WIDTH // 1024)), tm=1024, tn=1024,
                first_tile_scale=NA_HEAD_DIM ** -0.5 * LOG2_E, rope_tile=OFF_R_Q // 1024, rope_tables=_rope_tables(l))
    tn = 512
    ctx_tiles = tuple(range(OFF_NA_K // tn, OFF_R_Q // tn)) + tuple(range(OFF_R_K // tn, OFF_R_G // tn))
    pc = _inproj(ctx2, gain_pre, mod3, w_in_b, rows_per_mod=b * lc, mod_row0=b,
                 col_tiles=ctx_tiles, tm=1024, tn=tn)
    p = p.reshape(b, l, IN_WIDTH)
    pc = pc.reshape(b, lc, len(ctx_tiles) * tn)
    pc_na_k, pc_na_v = 0, NA_WIDTH
    pc_r_k, pc_r_v = 2 * NA_WIDTH, 2 * NA_WIDTH + RET_QK_WIDTH

    o_na = _neighbourhood_attention(p, pc, pc_na_k, pc_na_v, na_rpb[li])
    o_ret = _retention(p, pc, pc_r_k, pc_r_v, ret_log_gamma_fwd[li].astype(F32), ret_log_gamma_bwd[li].astype(F32))

    x_new, h2 = _outproj(o_na.reshape(b * l, NA_WIDTH), o_ret.reshape(b * l, RET_V_WIDTH), w_out[li].astype(BF16),
                         x2, mod3, norm_post_mix[li].reshape(1, d), norm_pre_ffn[li].reshape(1, d),
                         rows_per_mod=l, tm=512)
    out = _ffn(h2, w_gate[li].astype(BF16), w_up[li].astype(BF16), w_down[li].astype(BF16), x_new, mod3,
               norm_post_ffn[li].reshape(1, d), rows_per_mod=l, tm=1024, tf=512)
    return out.reshape(b, l, d)
```

```python
import functools

import numpy as np
import jax
import jax.numpy as jnp
from jax import lax
from jax.experimental import pallas as pl
from jax.experimental.pallas import tpu as pltpu

F32 = jnp.float32
BF16 = jnp.bfloat16

D_MODEL = 2048
GRID_W = 64
NA_HEADS = 8
NA_HEAD_DIM = 128
NA_WIN_H = 8
NA_WIN_W = 16
RET_HEADS = 8
RET_QK_DIM = 64
RET_V_DIM = 128
NA_WIDTH = NA_HEADS * NA_HEAD_DIM
RET_QK_WIDTH = RET_HEADS * RET_QK_DIM
RET_V_WIDTH = RET_HEADS * RET_V_DIM
IN_WIDTH = 3 * NA_WIDTH + 2 * RET_QK_WIDTH + 2 * RET_V_WIDTH
ROPE_BASE = 10000.0
EPS = 1e-6
MASK_VALUE = -1e30
LOG2_E = float(np.log2(np.e))

OFF_NA_Q = 0
OFF_NA_K = NA_WIDTH
OFF_NA_V = 2 * NA_WIDTH
OFF_R_Q = 3 * NA_WIDTH
OFF_R_K = OFF_R_Q + RET_QK_WIDTH
OFF_R_V = OFF_R_K + RET_QK_WIDTH
OFF_R_G = OFF_R_V + RET_V_WIDTH

VMEM_LIMIT_BYTES = 56 * 1024 * 1024

RET_CHUNK = 256


def _params(*semantics):
    return pltpu.CompilerParams(dimension_semantics=semantics, vmem_limit_bytes=VMEM_LIMIT_BYTES)


def _dot(a, b):
    return jnp.dot(a, b, preferred_element_type=F32)


def _dot_nt(a, b):
    return lax.dot_general(a, b, (((1,), (1,)), ((), ())), preferred_element_type=F32)


def _silu(x):
    return x * jax.nn.sigmoid(x)


def _adaln_kernel(c_ref, w_ref, b_ref, o_ref):
    s = _silu(c_ref[...]).astype(BF16)
    o_ref[...] = _dot(s, w_ref[...].astype(BF16)) + b_ref[...]


def _adaln(cc, ada_w, ada_b, tn=1024):
    rows, d = cc.shape
    n = ada_w.shape[1]
    return pl.pallas_call(
        _adaln_kernel,
        grid=(n // tn,),
        in_specs=[
            pl.BlockSpec((rows, d), lambda j: (0, 0)),
            pl.BlockSpec((d, tn), lambda j: (0, j)),
            pl.BlockSpec((1, tn), lambda j: (0, j)),
        ],
        out_specs=pl.BlockSpec((rows, tn), lambda j: (0, j)),
        out_shape=jax.ShapeDtypeStruct((rows, n), F32),
        compiler_params=_params("arbitrary"),
        name="adaln",
    )(cc, ada_w, ada_b.reshape(1, n))


INPROJ_SUB_ROWS = 256


def _rope_rotate(x, cos, sin_signed):
    lane = lax.broadcasted_iota(jnp.int32, (1, 128), 1)
    first_half = (lane % (RET_QK_DIM // 2)) < (RET_QK_DIM // 4)
    partner = jnp.where(first_half, pltpu.roll(x, 128 - RET_QK_DIM // 4, 1), pltpu.roll(x, RET_QK_DIM // 4, 1))
    return x * cos + partner * sin_signed


def _inproj_kernel(x_ref, g_ref, sh_ref, sc_ref, w_ref, *refs, first_tile_scale, rope_tile):
    if rope_tile is None:
        o_ref, h_ref = refs
    else:
        cos_ref, sin_ref, o_ref, h_ref = refs
    j = pl.program_id(1)
    n_slices = x_ref.shape[0] // INPROJ_SUB_ROWS

    @pl.when(j == 0)
    def _():
        gain = g_ref[...] * (1.0 + sc_ref[...])
        shift = sh_ref[...]
        for s in range(n_slices):
            rows = pl.ds(s * INPROJ_SUB_ROWS, INPROJ_SUB_ROWS)
            x = x_ref[rows, :]
            ms = jnp.mean(x * x, axis=-1, keepdims=True)
            h = (x * lax.rsqrt(ms + EPS) * gain + shift).astype(BF16)
            h_ref[rows, :] = h
            res = _dot(h, w_ref[...])
            if first_tile_scale != 1.0:
                res = res * first_tile_scale
            o_ref[rows, :] = res.astype(BF16)

    plain = (j > 0) if rope_tile is None else ((j > 0) & (j != rope_tile))

    @pl.when(plain)
    def _():
        o_ref[...] = _dot(h_ref[...], w_ref[...]).astype(BF16)

    if rope_tile is not None:
        @pl.when(j == rope_tile)
        def _():
            q_scale = RET_QK_DIM ** -0.5
            for s in range(n_slices):
                rows = pl.ds(s * INPROJ_SUB_ROWS, INPROJ_SUB_ROWS)
                res = _dot(h_ref[rows, :], w_ref[...])
                cos, sin = cos_ref[rows, :], sin_ref[rows, :]
                cos_q, sin_q = cos * q_scale, sin * q_scale
                for cg in range(o_ref.shape[1] // 128):
                    lanes = slice(cg * 128, (cg + 1) * 128)
                    is_q = cg < RET_QK_WIDTH // 128
                    y = _rope_rotate(res[:, lanes], cos_q if is_q else cos, sin_q if is_q else sin)
                    o_ref[rows, lanes] = y.astype(BF16)


def _inproj(x2, gain, mod3, w_bf16, *, rows_per_mod, mod_row0, col_tiles, tm, tn,
            first_tile_scale=1.0, rope_tile=None, rope_tables=None):
    m, d = x2.shape
    n_out = len(col_tiles) * tn
    first = col_tiles[0]
    gap_at = next((i for i in range(1, len(col_tiles)) if col_tiles[i] != col_tiles[i - 1] + 1), None)
    if gap_at is None:
        w_map = lambda i, j: (0, j + first)
    else:
        gap = col_tiles[gap_at] - col_tiles[gap_at - 1] - 1
        assert all(col_tiles[i] == first + i + (gap if i >= gap_at else 0) for i in range(len(col_tiles)))
        w_map = lambda i, j: (0, j + first + jnp.where(j >= gap_at, gap, 0))
    mod_row = lambda i: mod_row0 + (i * tm) // rows_per_mod
    in_specs = [
        pl.BlockSpec((tm, d), lambda i, j: (i, 0)),
        pl.BlockSpec((1, d), lambda i, j: (0, 0)),
        pl.BlockSpec((None, 1, d), lambda i, j: (mod_row(i), 0, 0)),
        pl.BlockSpec((None, 1, d), lambda i, j: (mod_row(i), 0, 1)),
        pl.BlockSpec((d, tn), w_map),
    ]
    operands = [x2, gain, mod3, mod3, w_bf16]
    if rope_tile is not None:
        assert tn == 2 * RET_QK_WIDTH and col_tiles[rope_tile] * tn == OFF_R_Q and rows_per_mod % tm == 0
        pos_blk = pl.BlockSpec((tm, 128), lambda i, j: (i % (rows_per_mod // tm), 0))
        in_specs += [pos_blk, pos_blk]
        operands += list(rope_tables)
    return pl.pallas_call(
        functools.partial(_inproj_kernel, first_tile_scale=first_tile_scale, rope_tile=rope_tile),
        grid=(m // tm, len(col_tiles)),
        in_specs=in_specs,
        out_specs=pl.BlockSpec((tm, tn), lambda i, j: (i, j)),
        out_shape=jax.ShapeDtypeStruct((m, n_out), BF16),
        scratch_shapes=[pltpu.VMEM((tm, d), BF16)],
        compiler_params=_params("parallel", "arbitrary"),
        name="inproj",
    )(*operands)


NA_GROUP_ROWS = 4
NA_BAND_ROWS = NA_GROUP_ROWS + NA_WIN_H
NA_Q = NA_GROUP_ROWS * GRID_W
NA_BAND = NA_BAND_ROWS * GRID_W


def _na_band_start(g, rows):
    lo = g * NA_GROUP_ROWS - NA_WIN_H // 2
    if isinstance(g, (int, np.integer)):
        return min(max(lo, 0), rows - NA_BAND_ROWS)
    return jnp.minimum(jnp.maximum(lo, 0), rows - NA_BAND_ROWS)


def _na_row_patterns(rows):
    groups = rows // NA_GROUP_ROWS
    assert rows % NA_GROUP_ROWS == 0 and rows >= NA_BAND_ROWS and groups >= 3

    def row_pattern(g):
        r = g * NA_GROUP_ROWS + np.arange(NA_GROUP_ROWS)[:, None]
        krow = _na_band_start(g, rows) + np.arange(NA_BAND_ROWS)[None, :]
        rs = np.clip(r - NA_WIN_H // 2, 0, rows - NA_WIN_H)
        ok = (krow >= rs) & (krow < rs + NA_WIN_H)
        return ok, np.clip(krow - r + NA_WIN_H - 1, 0, 2 * NA_WIN_H - 2)

    pats = [row_pattern(0), row_pattern(1), row_pattern(groups - 1)]
    for g in range(1, groups - 1):
        ok, dr = row_pattern(g)
        assert (ok == pats[1][0]).all() and (dr[ok] == pats[1][1][ok]).all()
    return pats


NA_SCORE_AHEAD = 4


def _na_kernel(rpb_ref, q_ref, k_ref, v_ref, kc_ref, vc_ref, o_ref, s_ref, bias_ref, va_ref, vca_ref, *, rows):
    groups = rows // NA_GROUP_ROWS
    hd = NA_HEAD_DIM

    @pl.when(pl.program_id(1) == 0)
    def _():
        n_dr, n_dc = 2 * NA_WIN_H - 1, 2 * NA_WIN_W - 1
        base = pl.program_id(0) * (n_dr * n_dc)
        c = lax.broadcasted_iota(jnp.int32, (GRID_W, GRID_W), 0)
        kc = lax.broadcasted_iota(jnp.int32, (GRID_W, GRID_W), 1)
        col_start = jnp.minimum(jnp.maximum(c - NA_WIN_W // 2, 0), GRID_W - NA_WIN_W)
        col_ok = (kc >= col_start) & (kc < col_start + NA_WIN_W)
        dc = kc - c + (NA_WIN_W - 1)
        masked = jnp.full((GRID_W, GRID_W), MASK_VALUE, F32)
        pats = _na_row_patterns(rows)
        used = sorted({int(dr[a, j]) for ok, dr in pats for a, j in zip(*np.nonzero(ok))})
        tiles = {}
        for d in used:
            t = jnp.zeros((GRID_W, GRID_W), F32)
            for x in range(n_dc):
                t = jnp.where(dc == x, rpb_ref[base + d * n_dc + x], t)
            tiles[d] = jnp.where(col_ok, t * LOG2_E, MASK_VALUE)
        for pi, (ok, dr) in enumerate(pats):
            for a in range(NA_GROUP_ROWS):
                for j in range(NA_BAND_ROWS):
                    tile = tiles[int(dr[a, j])] if ok[a, j] else masked
                    bias_ref[pi, a * GRID_W:(a + 1) * GRID_W, j * GRID_W:(j + 1) * GRID_W] = tile

    va_ref[:, :hd] = v_ref[...]
    va_ref[:, hd:] = jnp.ones((va_ref.shape[0], hd), BF16)
    vca_ref[:, :hd] = vc_ref[...]
    vca_ref[:, hd:] = jnp.ones((vca_ref.shape[0], hd), BF16)

    def scores(g, s_ref):
        q0 = g * NA_Q
        k0 = _na_band_start(g, rows) * GRID_W
        pattern = 0 if g == 0 else (2 if g == groups - 1 else 1)
        q = q_ref[pl.ds(q0, NA_Q), :]
        s_ref[:, :NA_BAND] = _dot_nt(q, k_ref[pl.ds(k0, NA_BAND), :]) + bias_ref[pattern]
        s_ref[:, NA_BAND:] = _dot_nt(q, kc_ref[...])

    def attend(g, s_ref):
        q0 = g * NA_Q
        k0 = _na_band_start(g, rows) * GRID_W
        s = s_ref[...]
        pb = jnp.exp2(s - jnp.max(s, axis=-1, keepdims=True)).astype(BF16)
        o = _dot(pb[:, :NA_BAND], va_ref[pl.ds(k0, NA_BAND), :]) + _dot(pb[:, NA_BAND:], vca_ref[...])
        o_ref[pl.ds(q0, NA_Q), :] = (o[:, :hd] / o[:, hd:hd + 1]).astype(BF16)

    for g in range(min(NA_SCORE_AHEAD, groups)):
        scores(g, s_ref.at[g])
    for g in range(groups):
        if g + NA_SCORE_AHEAD < groups:
            scores(g + NA_SCORE_AHEAD, s_ref.at[g + NA_SCORE_AHEAD])
        attend(g, s_ref.at[g])


def _neighbourhood_attention(p, pc, pc_off_k, pc_off_v, rpb):
    b, l, _ = p.shape
    lc = pc.shape[1]
    rows = l // GRID_W
    hd = NA_HEAD_DIM
    assert rpb.shape == (NA_HEADS, 2 * NA_WIN_H - 1, 2 * NA_WIN_W - 1)
    blk = lambda off: (lambda h, bi: (bi, 0, off // hd + h))
    return pl.pallas_call(
        functools.partial(_na_kernel, rows=rows),
        grid=(NA_HEADS, b),
        in_specs=[
            pl.BlockSpec(memory_space=pltpu.SMEM),
            pl.BlockSpec((None, l, hd), blk(OFF_NA_Q)),
            pl.BlockSpec((None, l, hd), blk(OFF_NA_K)),
            pl.BlockSpec((None, l, hd), blk(OFF_NA_V)),
            pl.BlockSpec((None, lc, hd), blk(pc_off_k)),
            pl.BlockSpec((None, lc, hd), blk(pc_off_v)),
        ],
        out_specs=pl.BlockSpec((None, l, hd), lambda h, bi: (bi, 0, h)),
        out_shape=jax.ShapeDtypeStruct((b, l, NA_WIDTH), BF16),
        scratch_shapes=[pltpu.VMEM((rows // NA_GROUP_ROWS, NA_Q, NA_BAND + lc), F32),
                        pltpu.VMEM((3, NA_Q, NA_BAND), F32),
                        pltpu.VMEM((l, 2 * hd), BF16),
                        pltpu.VMEM((lc, 2 * hd), BF16)],
        compiler_params=_params("parallel", "arbitrary"),
        name="na",
    )(rpb.astype(F32).reshape(-1), p, p, p, pc, pc)


def _rope_tables(l):
    t = np.arange(l)
    pos_row, pos_col = t // GRID_W, t % GRID_W
    quarter = RET_QK_DIM // 4
    inv_freq = ROPE_BASE ** (-np.arange(quarter, dtype=np.float64) / quarter)
    ang_r = pos_row[:, None] * inv_freq
    ang_c = pos_col[:, None] * inv_freq
    cos = np.concatenate([np.cos(ang_r), np.cos(ang_r), np.cos(ang_c), np.cos(ang_c)], axis=-1)
    sin = np.concatenate([-np.sin(ang_r), np.sin(ang_r), -np.sin(ang_c), np.sin(ang_c)], axis=-1)
    reps = 128 // RET_QK_DIM
    return (jnp.asarray(np.tile(cos, (1, reps)), F32), jnp.asarray(np.tile(sin, (1, reps)), F32))


def _ret_tables_kernel(lgf_ref, lgb_ref, dk_ref, dq_ref, dbi_ref, cd_ref):
    C = RET_CHUNK
    h = pl.program_id(0)
    lgf = lgf_ref[h]
    lgb = lgb_ref[h]
    lane = lax.broadcasted_iota(jnp.int32, (1, 128), 1)
    head_lanes = (lane // RET_QK_DIM) == (h % 2)
    row = lax.broadcasted_iota(jnp.int32, (C, 128), 0).astype(F32)
    diff = (lax.broadcasted_iota(jnp.int32, (C, C), 0) - lax.broadcasted_iota(jnp.int32, (C, C), 1)).astype(F32)
    dk_ref[...] = jnp.where(head_lanes, jnp.exp(lgf * (C - 1.0 - row)), jnp.exp(lgb * row))
    dq_ref[...] = jnp.where(head_lanes, jnp.exp(lgf * (row + 1.0)), jnp.exp(lgb * (C - row)))
    dbi_ref[...] = (jnp.where(diff >= 0, jnp.exp(lgf * jnp.maximum(diff, 0.0)), 0.0)
                    + jnp.where(diff <= 0, jnp.exp(lgb * jnp.maximum(-diff, 0.0)), 0.0))
    sub = lax.broadcasted_iota(jnp.int32, (8, 128), 0)
    cd_ref[...] = jnp.exp(jnp.where(sub == 0, lgf, lgb) * jnp.full((8, 128), float(C), F32))


def _ret_tables(lg_f, lg_b):
    C = RET_CHUNK
    smem = pl.BlockSpec(memory_space=pltpu.SMEM)
    per_head = lambda *shape: pl.BlockSpec((None,) + shape, lambda h: (h,) + (0,) * len(shape))
    return pl.pallas_call(
        _ret_tables_kernel,
        grid=(RET_HEADS,),
        in_specs=[smem, smem],
        out_specs=[per_head(C, 128), per_head(C, 128), per_head(C, C), per_head(8, 128)],
        out_shape=[jax.ShapeDtypeStruct((RET_HEADS, C, 128), F32), jax.ShapeDtypeStruct((RET_HEADS, C, 128), F32),
                   jax.ShapeDtypeStruct((RET_HEADS, C, C), F32), jax.ShapeDtypeStruct((RET_HEADS, 8, 128), F32)],
        compiler_params=_params("arbitrary"),
        name="ret_tables",
    )(lg_f, lg_b)


def _ret_kernel(q_ref, k_ref, v_ref, g_ref, kc_ref, vc_ref, dk_ref, dq_ref, dbi_ref, cd_ref,
                o_ref, a_ref, sb_ref, st_ref, *, n_chunks):
    C = RET_CHUNK
    lane = lax.broadcasted_iota(jnp.int32, (1, 128), 1)
    sub = lax.broadcasted_iota(jnp.int32, (128, 1), 0)

    for e in range(2):
        head_lanes = (lane // RET_QK_DIM) == e
        fwd_rows = (sub // RET_QK_DIM) == e
        vsl = slice(e * RET_V_DIM, (e + 1) * RET_V_DIM)
        dk = dk_ref[e]
        cdf = cd_ref[e, 0:1, :]
        cdb = cd_ref[e, 1:2, :]

        def local_state(k_bf16, v_bf16):
            kf32 = k_bf16.astype(F32)
            k_both = jnp.where(head_lanes, kf32, pltpu.roll(kf32, RET_QK_DIM, 1))
            return _dot((k_both * dk).T.astype(BF16), v_bf16)

        ctx_state = local_state(kc_ref[...], vc_ref[:, vsl])
        for i in range(n_chunks):
            rows = pl.ds(i * C, C)
            a_ref[i] = local_state(k_ref[rows, :], v_ref[rows, vsl])

        s_b = ctx_state
        for i in reversed(range(n_chunks)):
            sb_ref[i] = s_b
            s_b = s_b * cdb + a_ref[i]
        s_f = ctx_state
        for i in range(n_chunks):
            st_ref[e, i] = jnp.where(fwd_rows, s_f, sb_ref[i]).astype(BF16)
            s_f = s_f * cdf + a_ref[i]

    for i in range(n_chunks):
        rows = pl.ds(i * C, C)
        q = q_ref[rows, :].astype(F32)
        q_rolled = pltpu.roll(q, RET_QK_DIM, 1)
        kk = k_ref[rows, :]
        for e in range(2):
            head_lanes = (lane // RET_QK_DIM) == e
            vsl = slice(e * RET_V_DIM, (e + 1) * RET_V_DIM)
            q_own = jnp.where(head_lanes, q, 0.0).astype(BF16)
            q_both = (jnp.where(head_lanes, q, q_rolled) * dq_ref[e]).astype(BF16)
            inner = _dot_nt(q_own, kk) * dbi_ref[e]
            o = _dot(inner.astype(BF16), v_ref[rows, vsl]) + _dot(q_both, st_ref[e, i])
            mu = jnp.mean(o, axis=-1, keepdims=True)
            oc = o - mu
            var = jnp.mean(oc * oc, axis=-1, keepdims=True)
            gate = _silu(g_ref[rows, vsl].astype(F32))
            o_ref[rows, vsl] = (oc * lax.rsqrt(var + EPS) * gate).astype(BF16)


def _retention(p, pc, pc_off_k, pc_off_v, lg_f, lg_b):
    b, l, _ = p.shape
    lc = pc.shape[1]
    assert lc == RET_CHUNK and l % RET_CHUNK == 0
    n_chunks = l // RET_CHUNK
    dk, dq, dbi, cd = _ret_tables(lg_f, lg_b)
    qk_blk = lambda off: (lambda hp, bi: (bi, 0, off // 128 + hp))
    v_blk = lambda off: (lambda hp, bi: (bi, 0, off // 256 + hp))
    pair = lambda *shape: pl.BlockSpec((2,) + shape, lambda hp, bi: (hp,) + (0,) * len(shape))
    return pl.pallas_call(
        functools.partial(_ret_kernel, n_chunks=n_chunks),
        grid=(RET_HEADS // 2, b),
        in_specs=[
            pl.BlockSpec((None, l, 128), qk_blk(OFF_R_Q)),
            pl.BlockSpec((None, l, 128), qk_blk(OFF_R_K)),
            pl.BlockSpec((None, l, 256), v_blk(OFF_R_V)),
            pl.BlockSpec((None, l, 256), v_blk(OFF_R_G)),
            pl.BlockSpec((None, lc, 128), qk_blk(pc_off_k)),
            pl.BlockSpec((None, lc, 256), v_blk(pc_off_v)),
            pair(RET_CHUNK, 128), pair(RET_CHUNK, 128), pair(RET_CHUNK, RET_CHUNK), pair(8, 128),
        ],
        out_specs=pl.BlockSpec((None, l, 256), lambda hp, bi: (bi, 0, hp)),
        out_shape=jax.ShapeDtypeStruct((b, l, RET_V_WIDTH), BF16),
        scratch_shapes=[
            pltpu.VMEM((n_chunks, 128, RET_V_DIM), F32),
            pltpu.VMEM((n_chunks, 128, RET_V_DIM), F32),
            pltpu.VMEM((2, n_chunks, 128, RET_V_DIM), BF16),
        ],
        compiler_params=_params("parallel", "parallel"),
        name="ret",
    )(p, p, p, p, pc, pc, dk, dq, dbi, cd)


OUTPROJ_SUB_ROWS = 256


def _outproj_kernel(na_ref, ret_ref, wt_ref, wb_ref, x_ref, g1_ref, sh2_ref, sc2_ref, npost_ref, npre_ref,
                    xn_ref, h2_ref):
    gate_gain = g1_ref[...] * npost_ref[...]
    gain2 = npre_ref[...] * (1.0 + sc2_ref[...])
    shift2 = sh2_ref[...]
    for s in range(x_ref.shape[0] // OUTPROJ_SUB_ROWS):
        rows = pl.ds(s * OUTPROJ_SUB_ROWS, OUTPROJ_SUB_ROWS)
        mix = _dot(na_ref[rows, :], wt_ref[...]) + _dot(ret_ref[rows, :], wb_ref[...])
        ms = jnp.mean(mix * mix, axis=-1, keepdims=True)
        xn = x_ref[rows, :] + mix * lax.rsqrt(ms + EPS) * gate_gain
        xn_ref[rows, :] = xn
        ms2 = jnp.mean(xn * xn, axis=-1, keepdims=True)
        h2_ref[rows, :] = (xn * lax.rsqrt(ms2 + EPS) * gain2 + shift2).astype(BF16)


def _outproj(na, ret, w_out_bf16, x2, mod3, norm_post, norm_pre, *, rows_per_mod, tm):
    m, d = x2.shape
    kh = na.shape[1]
    mod_row = lambda i: (i * tm) // rows_per_mod
    mod_blk = lambda k: pl.BlockSpec((None, 1, d), lambda i: (mod_row(i), 0, k))
    return pl.pallas_call(
        _outproj_kernel,
        grid=(m // tm,),
        in_specs=[
            pl.BlockSpec((tm, kh), lambda i: (i, 0)),
            pl.BlockSpec((tm, kh), lambda i: (i, 0)),
            pl.BlockSpec((kh, d), lambda i: (0, 0)),
            pl.BlockSpec((kh, d), lambda i: (1, 0)),
            pl.BlockSpec((tm, d), lambda i: (i, 0)),
            mod_blk(2), mod_blk(3), mod_blk(4),
            pl.BlockSpec((1, d), lambda i: (0, 0)),
            pl.BlockSpec((1, d), lambda i: (0, 0)),
        ],
        out_specs=[pl.BlockSpec((tm, d), lambda i: (i, 0)), pl.BlockSpec((tm, d), lambda i: (i, 0))],
        out_shape=[jax.ShapeDtypeStruct((m, d), F32), jax.ShapeDtypeStruct((m, d), BF16)],
        compiler_params=_params("parallel"),
        name="outproj",
    )(na, ret, w_out_bf16, w_out_bf16, x2, mod3, mod3, mod3, norm_post, norm_pre)


FFN_SUB_ROWS = 256


def _ffn_kernel(h_ref, wg_hbm, wu_hbm, wd_hbm, xn_hbm, g2_ref, npost_ref, o_ref,
                wg_buf, wu_buf, wd_buf, xn_buf, w_sem, xn_sem, *, n_f):
    i = pl.program_id(0)
    n_i = pl.num_programs(0)
    tm = h_ref.shape[0]
    tf = wd_buf.shape[1]

    def weight_copies(f, slot):
        cols = pl.ds(f * tf, tf)
        return (pltpu.make_async_copy(wg_hbm.at[:, cols], wg_buf.at[slot], w_sem.at[0, slot]),
                pltpu.make_async_copy(wu_hbm.at[:, cols], wu_buf.at[slot], w_sem.at[1, slot]),
                pltpu.make_async_copy(wd_hbm.at[cols, :], wd_buf.at[slot], w_sem.at[2, slot]))

    def start_weights(f, slot):
        for cp in weight_copies(f, slot):
            cp.start()

    def wait_weights(f, slot):
        for cp in weight_copies(f, slot):
            cp.wait()

    def residual_copy():
        return pltpu.make_async_copy(xn_hbm.at[pl.ds(i * tm, tm), :], xn_buf, xn_sem)

    first_slot = lax.rem(i * n_f, 2)

    @pl.when(i == 0)
    def _():
        start_weights(0, first_slot)

    residual_copy().start()
    wait_weights(0, first_slot)

    for f in range(n_f):
        slot = lax.rem(first_slot + f, 2)
        nxt = 1 - slot
        if f + 1 < n_f:
            start_weights(f + 1, nxt)
        else:
            @pl.when(i + 1 < n_i)
            def _():
                start_weights(0, nxt)
        h = h_ref[...]
        t = (_silu(_dot(h, wg_buf[slot])) * _dot(h, wu_buf[slot])).astype(BF16)
        if f + 1 < n_f:
            wait_weights(f + 1, nxt)
        part = _dot(t, wd_buf[slot])
        if f == 0:
            o_ref[...] = part
        else:
            o_ref[...] += part

    residual_copy().wait()
    gate_gain = g2_ref[...] * npost_ref[...]
    for s in range(tm // FFN_SUB_ROWS):
        rows = pl.ds(s * FFN_SUB_ROWS, FFN_SUB_ROWS)
        y = o_ref[rows, :]
        ms = jnp.mean(y * y, axis=-1, keepdims=True)
        o_ref[rows, :] = xn_buf[rows, :] + y * lax.rsqrt(ms + EPS) * gate_gain


def _ffn(h2, wg, wu, wd, xn, mod3, norm_post, *, rows_per_mod, tm, tf):
    m, d = h2.shape
    dff = wd.shape[0]
    mod_row = lambda i: (i * tm) // rows_per_mod
    hbm = pl.BlockSpec(memory_space=pl.ANY)
    return pl.pallas_call(
        functools.partial(_ffn_kernel, n_f=dff // tf),
        grid=(m // tm,),
        in_specs=[
            pl.BlockSpec((tm, d), lambda i: (i, 0)),
            hbm, hbm, hbm, hbm,
            pl.BlockSpec((None, 1, d), lambda i: (mod_row(i), 0, 5)),
            pl.BlockSpec((1, d), lambda i: (0, 0)),
        ],
        out_specs=pl.BlockSpec((tm, d), lambda i: (i, 0)),
        out_shape=jax.ShapeDtypeStruct((m, d), F32),
        scratch_shapes=[
            pltpu.VMEM((2, d, tf), BF16), pltpu.VMEM((2, d, tf), BF16), pltpu.VMEM((2, tf, d), BF16),
            pltpu.VMEM((tm, d), F32),
            pltpu.SemaphoreType.DMA((3, 2)), pltpu.SemaphoreType.DMA(()),
        ],
        compiler_params=_params("arbitrary"),
        name="ffn",
    )(h2, wg, wu, wd, xn, mod3, norm_post)


def kernel(x, c, ctx, c_ctx, ada_w, ada_b, norm_pre_mix, norm_post_mix, norm_pre_ffn, norm_post_ffn, w_in,
           na_rpb, ret_log_gamma_fwd, ret_log_gamma_bwd, w_out, w_gate, w_up, w_down):
    b, l, d = x.shape
    lc = ctx.shape[1]
    depth = ada_w.shape[0]
    assert depth == 1, "the context-stream update between layers is not implemented"
    li = 0

    mod_rows = -(-(b + 1) // 8) * 8
    cc = jnp.concatenate([c, c_ctx[None, :], jnp.zeros((mod_rows - b - 1, d), F32)], axis=0)
    mod3 = _adaln(cc, ada_w[li], ada_b[li]).reshape(mod_rows, 1, 6 * d)

    w_in_b = w_in[li].astype(BF16)
    x2 = x.reshape(b * l, d)
    ctx2 = ctx.reshape(b * lc, d)
    gain_pre = norm_pre_mix[li].reshape(1, d)

    p = _inproj(x2, gain_pre, mod3, w_in_b, rows_per_mod=l, mod_row0=0,
                col_tiles=tuple(range(IN_WIDTH // 1024)), tm=1024, tn=1024,
                first_tile_scale=NA_HEAD_DIM ** -0.5 * LOG2_E, rope_tile=OFF_R_Q // 1024, rope_tables=_rope_tables(l))
    tn = 512
    ctx_tiles = tuple(range(OFF_NA_K // tn, OFF_R_Q // tn)) + tuple(range(OFF_R_K // tn, OFF_R_G // tn))
    pc = _inproj(ctx2, gain_pre, mod3, w_in_b, rows_per_mod=b * lc, mod_row0=b,
                 col_tiles=ctx_tiles, tm=1024, tn=tn)
    p = p.reshape(b, l, IN_WIDTH)
    pc = pc.reshape(b, lc, len(ctx_tiles) * tn)
    pc_na_k, pc_na_v = 0, NA_WIDTH
    pc_r_k, pc_r_v = 2 * NA_WIDTH, 2 * NA_WIDTH + RET_QK_WIDTH

    o_na = _neighbourhood_attention(p, pc, pc_na_k, pc_na_v, na_rpb[li])
    o_ret = _retention(p, pc, pc_r_k, pc_r_v, ret_log_gamma_fwd[li].astype(F32), ret_log_gamma_bwd[li].astype(F32))

    x_new, h2 = _outproj(o_na.reshape(b * l, NA_WIDTH), o_ret.reshape(b * l, RET_V_WIDTH), w_out[li].astype(BF16),
                         x2, mod3, norm_post_mix[li].reshape(1, d), norm_pre_ffn[li].reshape(1, d),
                         rows_per_mod=l, tm=512)
    out = _ffn(h2, w_gate[li].astype(BF16), w_up[li].astype(BF16), w_down[li].astype(BF16), x_new, mod3,
               norm_post_ffn[li].reshape(1, d), rows_per_mod=l, tm=1024, tf=512)
    return out.reshape(b, l, d)
```

```python
import functools

import numpy as np
import jax
import jax.numpy as jnp
from jax import lax
from jax.experimental import pallas as pl
from jax.experimental.pallas import tpu as pltpu

F32 = jnp.float32
BF16 = jnp.bfloat16

D_MODEL = 2048
GRID_W = 64
NA_HEADS = 8
NA_HEAD_DIM = 128
NA_WIN_H = 8
NA_WIN_W = 16
RET_HEADS = 8
RET_QK_DIM = 64
RET_V_DIM = 128
NA_WIDTH = NA_HEADS * NA_HEAD_DIM
RET_QK_WIDTH = RET_HEADS * RET_QK_DIM
RET_V_WIDTH = RET_HEADS * RET_V_DIM
IN_WIDTH = 3 * NA_WIDTH + 2 * RET_QK_WIDTH + 2 * RET_V_WIDTH
ROPE_BASE = 10000.0
EPS = 1e-6
MASK_VALUE = -1e30
LOG2_E = float(np.log2(np.e))

OFF_NA_Q = 0
OFF_NA_K = NA_WIDTH
OFF_NA_V = 2 * NA_WIDTH
OFF_R_Q = 3 * NA_WIDTH
OFF_R_K = OFF_R_Q + RET_QK_WIDTH
OFF_R_V = OFF_R_K + RET_QK_WIDTH
OFF_R_G = OFF_R_V + RET_V_WIDTH

VMEM_LIMIT_BYTES = 56 * 1024 * 1024

RET_CHUNK = 256


def _params(*semantics):
    return pltpu.CompilerParams(dimension_semantics=semantics, vmem_limit_bytes=VMEM_LIMIT_BYTES)


def _dot(a, b):
    return jnp.dot(a, b, preferred_element_type=F32)


def _dot_nt(a, b):
    return lax.dot_general(a, b, (((1,), (1,)), ((), ())), preferred_element_type=F32)


def _silu(x):
    return x * jax.nn.sigmoid(x)


def _adaln_kernel(c_ref, w_ref, b_ref, o_ref):
    s = _silu(c_ref[...]).astype(BF16)
    o_ref[...] = _dot(s, w_ref[...].astype(BF16)) + b_ref[...]


def _adaln(cc, ada_w, ada_b, tn=1024):
    rows, d = cc.shape
    n = ada_w.shape[1]
    return pl.pallas_call(
        _adaln_kernel,
        grid=(n // tn,),
        in_specs=[
            pl.BlockSpec((rows, d), lambda j: (0, 0)),
            pl.BlockSpec((d, tn), lambda j: (0, j)),
            pl.BlockSpec((1, tn), lambda j: (0, j)),
        ],
        out_specs=pl.BlockSpec((rows, tn), lambda j: (0, j)),
        out_shape=jax.ShapeDtypeStruct((rows, n), F32),
        compiler_params=_params("arbitrary"),
        name="adaln",
    )(cc, ada_w, ada_b.reshape(1, n))


INPROJ_SUB_ROWS = 256


def _rope_rotate(x, cos, sin_signed):
    lane = lax.broadcasted_iota(jnp.int32, (1, 128), 1)
    first_half = (lane % (RET_QK_DIM // 2)) < (RET_QK_DIM // 4)
    partner = jnp.where(first_half, pltpu.roll(x, 128 - RET_QK_DIM // 4, 1), pltpu.roll(x, RET_QK_DIM // 4, 1))
    return x * cos + partner * sin_signed


def _inproj_kernel(x_ref, g_ref, sh_ref, sc_ref, w_ref, *refs, first_tile_scale, rope_tile):
    if rope_tile is None:
        o_ref, h_ref = refs
    else:
        cos_ref, sin_ref, o_ref, h_ref = refs
    j = pl.program_id(1)
    n_slices = x_ref.shape[0] // INPROJ_SUB_ROWS

    @pl.when(j == 0)
    def _():
        gain = g_ref[...] * (1.0 + sc_ref[...])
        shift = sh_ref[...]
        for s in range(n_slices):
            rows = pl.ds(s * INPROJ_SUB_ROWS, INPROJ_SUB_ROWS)
            x = x_ref[rows, :]
            ms = jnp.mean(x * x, axis=-1, keepdims=True)
            h = (x * lax.rsqrt(ms + EPS) * gain + shift).astype(BF16)
            h_ref[rows, :] = h
            res = _dot(h, w_ref[...])
            if first_tile_scale != 1.0:
                res = res * first_tile_scale
            o_ref[rows, :] = res.astype(BF16)

    plain = (j > 0) if rope_tile is None else ((j > 0) & (j != rope_tile))

    @pl.when(plain)
    def _():
        o_ref[...] = _dot(h_ref[...], w_ref[...]).astype(BF16)

    if rope_tile is not None:
        @pl.when(j == rope_tile)
        def _():
            q_scale = RET_QK_DIM ** -0.5
            for s in range(n_slices):
                rows = pl.ds(s * INPROJ_SUB_ROWS, INPROJ_SUB_ROWS)
                res = _dot(h_ref[rows, :], w_ref[...])
                cos, sin = cos_ref[rows, :], sin_ref[rows, :]
                cos_q, sin_q = cos * q_scale, sin * q_scale
                for cg in range(o_ref.shape[1] // 128):
                    lanes = slice(cg * 128, (cg + 1) * 128)
                    is_q = cg < RET_QK_WIDTH // 128
                    y = _rope_rotate(res[:, lanes], cos_q if is_q else cos, sin_q if is_q else sin)
                    o_ref[rows, lanes] = y.astype(BF16)


def _inproj(x2, gain, mod3, w_tiles, *, rows_per_mod, mod_row0, tile0, n_tiles, tm,
            first_tile_scale=1.0, rope_tile=None, rope_tables=None):
    m, d = x2.shape
    tn = w_tiles.shape[2]
    mod_row = lambda i: mod_row0 + (i * tm) // rows_per_mod
    in_specs = [
        pl.BlockSpec((tm, d), lambda i, j: (i, 0)),
        pl.BlockSpec((1, d), lambda i, j: (0, 0)),
        pl.BlockSpec((None, 1, d), lambda i, j: (mod_row(i), 0, 0)),
        pl.BlockSpec((None, 1, d), lambda i, j: (mod_row(i), 0, 1)),
        pl.BlockSpec((None, d, tn), lambda i, j: (tile0 + j, 0, 0)),
    ]
    operands = [x2, gain, mod3, mod3, w_tiles]
    if rope_tile is not None:
        assert tn == 2 * RET_QK_WIDTH and (tile0 + rope_tile) * tn == OFF_R_Q and rows_per_mod % tm == 0
        pos_blk = pl.BlockSpec((tm, 128), lambda i, j: (i % (rows_per_mod // tm), 0))
        in_specs += [pos_blk, pos_blk]
        operands += list(rope_tables)
    return pl.pallas_call(
        functools.partial(_inproj_kernel, first_tile_scale=first_tile_scale, rope_tile=rope_tile),
        grid=(m // tm, n_tiles),
        in_specs=in_specs,
        out_specs=pl.BlockSpec((None, tm, tn), lambda i, j: (j, i, 0)),
        out_shape=jax.ShapeDtypeStruct((n_tiles, m, tn), BF16),
        scratch_shapes=[pltpu.VMEM((tm, d), BF16)],
        compiler_params=_params("parallel", "arbitrary"),
        name="inproj",
    )(*operands)


PROJ_TILE = 1024
NA_GROUP_ROWS = 4
NA_BAND_ROWS = NA_GROUP_ROWS + NA_WIN_H
NA_Q = NA_GROUP_ROWS * GRID_W
NA_BAND = NA_BAND_ROWS * GRID_W


def _na_band_start(g, rows):
    lo = g * NA_GROUP_ROWS - NA_WIN_H // 2
    if isinstance(g, (int, np.integer)):
        return min(max(lo, 0), rows - NA_BAND_ROWS)
    return jnp.minimum(jnp.maximum(lo, 0), rows - NA_BAND_ROWS)


def _na_row_patterns(rows):
    groups = rows // NA_GROUP_ROWS
    assert rows % NA_GROUP_ROWS == 0 and rows >= NA_BAND_ROWS and groups >= 3

    def row_pattern(g):
        r = g * NA_GROUP_ROWS + np.arange(NA_GROUP_ROWS)[:, None]
        krow = _na_band_start(g, rows) + np.arange(NA_BAND_ROWS)[None, :]
        rs = np.clip(r - NA_WIN_H // 2, 0, rows - NA_WIN_H)
        ok = (krow >= rs) & (krow < rs + NA_WIN_H)
        return ok, np.clip(krow - r + NA_WIN_H - 1, 0, 2 * NA_WIN_H - 2)

    pats = [row_pattern(0), row_pattern(1), row_pattern(groups - 1)]
    for g in range(1, groups - 1):
        ok, dr = row_pattern(g)
        assert (ok == pats[1][0]).all() and (dr[ok] == pats[1][1][ok]).all()
    return pats


NA_SCORE_AHEAD = 4


def _na_kernel(rpb_ref, q_ref, k_ref, v_ref, kc_ref, vc_ref, o_ref, s_ref, bias_ref, va_ref, vca_ref, *, rows):
    groups = rows // NA_GROUP_ROWS
    hd = NA_HEAD_DIM

    @pl.when(pl.program_id(1) == 0)
    def _():
        n_dr, n_dc = 2 * NA_WIN_H - 1, 2 * NA_WIN_W - 1
        base = pl.program_id(0) * (n_dr * n_dc)
        c = lax.broadcasted_iota(jnp.int32, (GRID_W, GRID_W), 0)
        kc = lax.broadcasted_iota(jnp.int32, (GRID_W, GRID_W), 1)
        col_start = jnp.minimum(jnp.maximum(c - NA_WIN_W // 2, 0), GRID_W - NA_WIN_W)
        col_ok = (kc >= col_start) & (kc < col_start + NA_WIN_W)
        dc = kc - c + (NA_WIN_W - 1)
        masked = jnp.full((GRID_W, GRID_W), MASK_VALUE, F32)
        pats = _na_row_patterns(rows)
        used = sorted({int(dr[a, j]) for ok, dr in pats for a, j in zip(*np.nonzero(ok))})
        tiles = {}
        for d in used:
            t = jnp.zeros((GRID_W, GRID_W), F32)
            for x in range(n_dc):
                t = jnp.where(dc == x, rpb_ref[base + d * n_dc + x], t)
            tiles[d] = jnp.where(col_ok, t * LOG2_E, MASK_VALUE)
        for pi, (ok, dr) in enumerate(pats):
            for a in range(NA_GROUP_ROWS):
                for j in range(NA_BAND_ROWS):
                    tile = tiles[int(dr[a, j])] if ok[a, j] else masked
                    bias_ref[pi, a * GRID_W:(a + 1) * GRID_W, j * GRID_W:(j + 1) * GRID_W] = tile

    va_ref[:, :hd] = v_ref[...]
    va_ref[:, hd:] = jnp.ones((va_ref.shape[0], hd), BF16)
    vca_ref[:, :hd] = vc_ref[...]
    vca_ref[:, hd:] = jnp.ones((vca_ref.shape[0], hd), BF16)

    def scores(g, s_ref):
        q0 = g * NA_Q
        k0 = _na_band_start(g, rows) * GRID_W
        pattern = 0 if g == 0 else (2 if g == groups - 1 else 1)
        q = q_ref[pl.ds(q0, NA_Q), :]
        s_ref[:, :NA_BAND] = _dot_nt(q, k_ref[pl.ds(k0, NA_BAND), :]) + bias_ref[pattern]
        s_ref[:, NA_BAND:] = _dot_nt(q, kc_ref[...])

    def attend(g, s_ref):
        q0 = g * NA_Q
        k0 = _na_band_start(g, rows) * GRID_W
        s = s_ref[...]
        pb = jnp.exp2(s - jnp.max(s, axis=-1, keepdims=True)).astype(BF16)
        o = _dot(pb[:, :NA_BAND], va_ref[pl.ds(k0, NA_BAND), :]) + _dot(pb[:, NA_BAND:], vca_ref[...])
        o_ref[pl.ds(q0, NA_Q), :] = (o[:, :hd] / o[:, hd:hd + 1]).astype(BF16)

    for g in range(min(NA_SCORE_AHEAD, groups)):
        scores(g, s_ref.at[g])
    for g in range(groups):
        if g + NA_SCORE_AHEAD < groups:
            scores(g + NA_SCORE_AHEAD, s_ref.at[g + NA_SCORE_AHEAD])
        attend(g, s_ref.at[g])


def _proj_block(proj, tile0, off, width):
    tile, col = off // PROJ_TILE - tile0, (off % PROJ_TILE) // width
    assert 0 <= tile < proj.shape[0] and off % width == 0
    return pl.BlockSpec((None, None, proj.shape[2], width), lambda h, bi: (tile, bi, 0, col + h))


def _neighbourhood_attention(p, pc, pc_tile0, rpb):
    _, b, l, _ = p.shape
    lc = pc.shape[2]
    rows = l // GRID_W
    hd = NA_HEAD_DIM
    assert rpb.shape == (NA_HEADS, 2 * NA_WIN_H - 1, 2 * NA_WIN_W - 1)
    return pl.pallas_call(
        functools.partial(_na_kernel, rows=rows),
        grid=(NA_HEADS, b),
        in_specs=[
            pl.BlockSpec(memory_space=pltpu.SMEM),
            _proj_block(p, 0, OFF_NA_Q, hd),
            _proj_block(p, 0, OFF_NA_K, hd),
            _proj_block(p, 0, OFF_NA_V, hd),
            _proj_block(pc, pc_tile0, OFF_NA_K, hd),
            _proj_block(pc, pc_tile0, OFF_NA_V, hd),
        ],
        out_specs=pl.BlockSpec((None, l, hd), lambda h, bi: (bi, 0, h)),
        out_shape=jax.ShapeDtypeStruct((b, l, NA_WIDTH), BF16),
        scratch_shapes=[pltpu.VMEM((rows // NA_GROUP_ROWS, NA_Q, NA_BAND + lc), F32),
                        pltpu.VMEM((3, NA_Q, NA_BAND), F32),
                        pltpu.VMEM((l, 2 * hd), BF16),
                        pltpu.VMEM((lc, 2 * hd), BF16)],
        compiler_params=_params("parallel", "arbitrary"),
        name="na",
    )(rpb.astype(F32).reshape(-1), p, p, p, pc, pc)


def _rope_tables(l):
    t = np.arange(l)
    pos_row, pos_col = t // GRID_W, t % GRID_W
    quarter = RET_QK_DIM // 4
    inv_freq = ROPE_BASE ** (-np.arange(quarter, dtype=np.float64) / quarter)
    ang_r = pos_row[:, None] * inv_freq
    ang_c = pos_col[:, None] * inv_freq
    cos = np.concatenate([np.cos(ang_r), np.cos(ang_r), np.cos(ang_c), np.cos(ang_c)], axis=-1)
    sin = np.concatenate([-np.sin(ang_r), np.sin(ang_r), -np.sin(ang_c), np.sin(ang_c)], axis=-1)
    reps = 128 // RET_QK_DIM
    return (jnp.asarray(np.tile(cos, (1, reps)), F32), jnp.asarray(np.tile(sin, (1, reps)), F32))


def _ret_tables_kernel(lgf_ref, lgb_ref, dk_ref, dq_ref, dbi_ref, cd_ref):
    C = RET_CHUNK
    h = pl.program_id(0)
    lgf = lgf_ref[h]
    lgb = lgb_ref[h]
    lane = lax.broadcasted_iota(jnp.int32, (1, 128), 1)
    head_lanes = (lane // RET_QK_DIM) == (h % 2)
    row = lax.broadcasted_iota(jnp.int32, (C, 128), 0).astype(F32)
    diff = (lax.broadcasted_iota(jnp.int32, (C, C), 0) - lax.broadcasted_iota(jnp.int32, (C, C), 1)).astype(F32)
    dk_ref[...] = jnp.where(head_lanes, jnp.exp(lgf * (C - 1.0 - row)), jnp.exp(lgb * row))
    dq_ref[...] = jnp.where(head_lanes, jnp.exp(lgf * (row + 1.0)), jnp.exp(lgb * (C - row)))
    dbi_ref[...] = (jnp.where(diff >= 0, jnp.exp(lgf * jnp.maximum(diff, 0.0)), 0.0)
                    + jnp.where(diff <= 0, jnp.exp(lgb * jnp.maximum(-diff, 0.0)), 0.0))
    sub = lax.broadcasted_iota(jnp.int32, (8, 128), 0)
    cd_ref[...] = jnp.exp(jnp.where(sub == 0, lgf, lgb) * jnp.full((8, 128), float(C), F32))


def _ret_tables(lg_f, lg_b):
    C = RET_CHUNK
    smem = pl.BlockSpec(memory_space=pltpu.SMEM)
    per_head = lambda *shape: pl.BlockSpec((None,) + shape, lambda h: (h,) + (0,) * len(shape))
    return pl.pallas_call(
        _ret_tables_kernel,
        grid=(RET_HEADS,),
        in_specs=[smem, smem],
        out_specs=[per_head(C, 128), per_head(C, 128), per_head(C, C), per_head(8, 128)],
        out_shape=[jax.ShapeDtypeStruct((RET_HEADS, C, 128), F32), jax.ShapeDtypeStruct((RET_HEADS, C, 128), F32),
                   jax.ShapeDtypeStruct((RET_HEADS, C, C), F32), jax.ShapeDtypeStruct((RET_HEADS, 8, 128), F32)],
        compiler_params=_params("arbitrary"),
        name="ret_tables",
    )(lg_f, lg_b)


def _ret_kernel(q_ref, k_ref, v_ref, g_ref, kc_ref, vc_ref, dk_ref, dq_ref, dbi_ref, cd_ref,
                o_ref, a_ref, sb_ref, st_ref, *, n_chunks):
    C = RET_CHUNK
    lane = lax.broadcasted_iota(jnp.int32, (1, 128), 1)
    sub = lax.broadcasted_iota(jnp.int32, (128, 1), 0)

    for e in range(2):
        head_lanes = (lane // RET_QK_DIM) == e
        fwd_rows = (sub // RET_QK_DIM) == e
        vsl = slice(e * RET_V_DIM, (e + 1) * RET_V_DIM)
        dk = dk_ref[e]
        cdf = cd_ref[e, 0:1, :]
        cdb = cd_ref[e, 1:2, :]

        def local_state(k_bf16, v_bf16):
            kf32 = k_bf16.astype(F32)
            k_both = jnp.where(head_lanes, kf32, pltpu.roll(kf32, RET_QK_DIM, 1))
            return _dot((k_both * dk).T.astype(BF16), v_bf16)

        ctx_state = local_state(kc_ref[...], vc_ref[:, vsl])
        for i in range(n_chunks):
            rows = pl.ds(i * C, C)
            a_ref[i] = local_state(k_ref[rows, :], v_ref[rows, vsl])

        s_b = ctx_state
        for i in reversed(range(n_chunks)):
            sb_ref[i] = s_b
            s_b = s_b * cdb + a_ref[i]
        s_f = ctx_state
        for i in range(n_chunks):
            st_ref[e, i] = jnp.where(fwd_rows, s_f, sb_ref[i]).astype(BF16)
            s_f = s_f * cdf + a_ref[i]

    for i in range(n_chunks):
        rows = pl.ds(i * C, C)
        q = q_ref[rows, :].astype(F32)
        q_rolled = pltpu.roll(q, RET_QK_DIM, 1)
        kk = k_ref[rows, :]
        for e in range(2):
            head_lanes = (lane // RET_QK_DIM) == e
            vsl = slice(e * RET_V_DIM, (e + 1) * RET_V_DIM)
            q_own = jnp.where(head_lanes, q, 0.0).astype(BF16)
            q_both = (jnp.where(head_lanes, q, q_rolled) * dq_ref[e]).astype(BF16)
            inner = _dot_nt(q_own, kk) * dbi_ref[e]
            o = _dot(inner.astype(BF16), v_ref[rows, vsl]) + _dot(q_both, st_ref[e, i])
            mu = jnp.mean(o, axis=-1, keepdims=True)
            oc = o - mu
            var = jnp.mean(oc * oc, axis=-1, keepdims=True)
            gate = _silu(g_ref[rows, vsl].astype(F32))
            o_ref[rows, vsl] = (oc * lax.rsqrt(var + EPS) * gate).astype(BF16)


def _retention(p, pc, pc_tile0, lg_f, lg_b):
    _, b, l, _ = p.shape
    lc = pc.shape[2]
    assert lc == RET_CHUNK and l % RET_CHUNK == 0
    n_chunks = l // RET_CHUNK
    dk, dq, dbi, cd = _ret_tables(lg_f, lg_b)
    pair = lambda *shape: pl.BlockSpec((2,) + shape, lambda hp, bi: (hp,) + (0,) * len(shape))
    return pl.pallas_call(
        functools.partial(_ret_kernel, n_chunks=n_chunks),
        grid=(RET_HEADS // 2, b),
        in_specs=[
            _proj_block(p, 0, OFF_R_Q, 2 * RET_QK_DIM),
            _proj_block(p, 0, OFF_R_K, 2 * RET_QK_DIM),
            _proj_block(p, 0, OFF_R_V, 2 * RET_V_DIM),
            _proj_block(p, 0, OFF_R_G, 2 * RET_V_DIM),
            _proj_block(pc, pc_tile0, OFF_R_K, 2 * RET_QK_DIM),
            _proj_block(pc, pc_tile0, OFF_R_V, 2 * RET_V_DIM),
            pair(RET_CHUNK, 128), pair(RET_CHUNK, 128), pair(RET_CHUNK, RET_CHUNK), pair(8, 128),
        ],
        out_specs=pl.BlockSpec((None, l, 256), lambda hp, bi: (bi, 0, hp)),
        out_shape=jax.ShapeDtypeStruct((b, l, RET_V_WIDTH), BF16),
        scratch_shapes=[
            pltpu.VMEM((n_chunks, 128, RET_V_DIM), F32),
            pltpu.VMEM((n_chunks, 128, RET_V_DIM), F32),
            pltpu.VMEM((2, n_chunks, 128, RET_V_DIM), BF16),
        ],
        compiler_params=_params("parallel", "parallel"),
        name="ret",
    )(p, p, p, p, pc, pc, dk, dq, dbi, cd)


OUTPROJ_SUB_ROWS = 256


def _outproj_kernel(na_ref, ret_ref, wt_ref, wb_ref, x_ref, g1_ref, sh2_ref, sc2_ref, npost_ref, npre_ref,
                    xn_ref, h2_ref):
    gate_gain = g1_ref[...] * npost_ref[...]
    gain2 = npre_ref[...] * (1.0 + sc2_ref[...])
    shift2 = sh2_ref[...]
    for s in range(x_ref.shape[0] // OUTPROJ_SUB_ROWS):
        rows = pl.ds(s * OUTPROJ_SUB_ROWS, OUTPROJ_SUB_ROWS)
        mix = _dot(na_ref[rows, :], wt_ref[...]) + _dot(ret_ref[rows, :], wb_ref[...])
        ms = jnp.mean(mix * mix, axis=-1, keepdims=True)
        xn = x_ref[rows, :] + mix * lax.rsqrt(ms + EPS) * gate_gain
        xn_ref[rows, :] = xn
        ms2 = jnp.mean(xn * xn, axis=-1, keepdims=True)
        h2_ref[rows, :] = (xn * lax.rsqrt(ms2 + EPS) * gain2 + shift2).astype(BF16)


def _outproj(na, ret, w_out_bf16, x2, mod3, norm_post, norm_pre, *, rows_per_mod, tm):
    m, d = x2.shape
    kh = na.shape[1]
    mod_row = lambda i: (i * tm) // rows_per_mod
    mod_blk = lambda k: pl.BlockSpec((None, 1, d), lambda i: (mod_row(i), 0, k))
    return pl.pallas_call(
        _outproj_kernel,
        grid=(m // tm,),
        in_specs=[
            pl.BlockSpec((tm, kh), lambda i: (i, 0)),
            pl.BlockSpec((tm, kh), lambda i: (i, 0)),
            pl.BlockSpec((kh, d), lambda i: (0, 0)),
            pl.BlockSpec((kh, d), lambda i: (1, 0)),
            pl.BlockSpec((tm, d), lambda i: (i, 0)),
            mod_blk(2), mod_blk(3), mod_blk(4),
            pl.BlockSpec((1, d), lambda i: (0, 0)),
            pl.BlockSpec((1, d), lambda i: (0, 0)),
        ],
        out_specs=[pl.BlockSpec((tm, d), lambda i: (i, 0)), pl.BlockSpec((tm, d), lambda i: (i, 0))],
        out_shape=[jax.ShapeDtypeStruct((m, d), F32), jax.ShapeDtypeStruct((m, d), BF16)],
        compiler_params=_params("parallel"),
        name="outproj",
    )(na, ret, w_out_bf16, w_out_bf16, x2, mod3, mod3, mod3, norm_post, norm_pre)


FFN_SUB_ROWS = 256


def _ffn_kernel(h_ref, wg_ref, wu_ref, wd_ref, xn_hbm, g2_ref, npost_ref, o_ref, xn_buf, xn_sem):
    i = pl.program_id(0)
    f = pl.program_id(1)
    tm = h_ref.shape[0]

    def residual_copy():
        return pltpu.make_async_copy(xn_hbm.at[pl.ds(i * tm, tm), :], xn_buf, xn_sem)

    @pl.when(f == 0)
    def _():
        residual_copy().start()
        o_ref[...] = jnp.zeros_like(o_ref)

    h = h_ref[...]
    t = (_silu(_dot(h, wg_ref[...])) * _dot(h, wu_ref[...])).astype(BF16)
    o_ref[...] += _dot(t, wd_ref[...])

    @pl.when(f == pl.num_programs(1) - 1)
    def _():
        residual_copy().wait()
        gate_gain = g2_ref[...] * npost_ref[...]
        for s in range(tm // FFN_SUB_ROWS):
            rows = pl.ds(s * FFN_SUB_ROWS, FFN_SUB_ROWS)
            y = o_ref[rows, :]
            ms = jnp.mean(y * y, axis=-1, keepdims=True)
            o_ref[rows, :] = xn_buf[rows, :] + y * lax.rsqrt(ms + EPS) * gate_gain


def _ffn(h2, wg, wu, wd, xn, mod3, norm_post, *, rows_per_mod, tm, tf):
    m, d = h2.shape
    dff = wd.shape[0]
    mod_row = lambda i: (i * tm) // rows_per_mod
    return pl.pallas_call(
        _ffn_kernel,
        grid=(m // tm, dff // tf),
        in_specs=[
            pl.BlockSpec((tm, d), lambda i, f: (i, 0)),
            pl.BlockSpec((d, tf), lambda i, f: (0, f)),
            pl.BlockSpec((d, tf), lambda i, f: (0, f)),
            pl.BlockSpec((tf, d), lambda i, f: (f, 0)),
            pl.BlockSpec(memory_space=pl.ANY),
            pl.BlockSpec((None, 1, d), lambda i, f: (mod_row(i), 0, 5)),
            pl.BlockSpec((1, d), lambda i, f: (0, 0)),
        ],
        out_specs=pl.BlockSpec((tm, d), lambda i, f: (i, 0)),
        out_shape=jax.ShapeDtypeStruct((m, d), F32),
        scratch_shapes=[pltpu.VMEM((tm, d), F32), pltpu.SemaphoreType.DMA(())],
        compiler_params=_params("parallel", "arbitrary"),
        name="ffn",
    )(h2, wg, wu, wd, xn, mod3, norm_post)


def kernel(x, c, ctx, c_ctx, ada_w, ada_b, norm_pre_mix, norm_post_mix, norm_pre_ffn, norm_post_ffn, w_in,
           na_rpb, ret_log_gamma_fwd, ret_log_gamma_bwd, w_out, w_gate, w_up, w_down):
    b, l, d = x.shape
    lc = ctx.shape[1]
    depth = ada_w.shape[0]
    assert depth == 1, "the context-stream update between layers is not implemented"
    li = 0

    mod_rows = -(-(b + 1) // 8) * 8
    cc = jnp.concatenate([c, c_ctx[None, :], jnp.zeros((mod_rows - b - 1, d), F32)], axis=0)
    mod3 = _adaln(cc, ada_w[li], ada_b[li]).reshape(mod_rows, 1, 6 * d)

    n_tiles = IN_WIDTH // PROJ_TILE
    w_in_tiles = w_in[li].reshape(d, n_tiles, PROJ_TILE).transpose(1, 0, 2).astype(BF16)
    x2 = x.reshape(b * l, d)
    ctx2 = ctx.reshape(b * lc, d)
    gain_pre = norm_pre_mix[li].reshape(1, d)

    p = _inproj(x2, gain_pre, mod3, w_in_tiles, rows_per_mod=l, mod_row0=0, tile0=0, n_tiles=n_tiles, tm=1024,
                first_tile_scale=NA_HEAD_DIM ** -0.5 * LOG2_E, rope_tile=OFF_R_Q // PROJ_TILE,
                rope_tables=_rope_tables(l))
    ctx_tile0 = OFF_NA_K // PROJ_TILE
    pc = _inproj(ctx2, gain_pre, mod3, w_in_tiles, rows_per_mod=b * lc, mod_row0=b,
                 tile0=ctx_tile0, n_tiles=OFF_R_G // PROJ_TILE - ctx_tile0, tm=1024)
    p = p.reshape(n_tiles, b, l, PROJ_TILE)
    pc = pc.reshape(pc.shape[0], b, lc, PROJ_TILE)

    o_na = _neighbourhood_attention(p, pc, ctx_tile0, na_rpb[li])
    o_ret = _retention(p, pc, ctx_tile0, ret_log_gamma_fwd[li].astype(F32), ret_log_gamma_bwd[li].astype(F32))

    x_new, h2 = _outproj(o_na.reshape(b * l, NA_WIDTH), o_ret.reshape(b * l, RET_V_WIDTH), w_out[li].astype(BF16),
                         x2, mod3, norm_post_mix[li].reshape(1, d), norm_pre_ffn[li].reshape(1, d),
                         rows_per_mod=l, tm=512)
    out = _ffn(h2, w_gate[li].astype(BF16), w_up[li].astype(BF16), w_down[li].astype(BF16), x_new, mod3,
               norm_post_ffn[li].reshape(1, d), rows_per_mod=l, tm=1024, tf=512)
    return out.reshape(b, l, d)
```

```python
import functools

import numpy as np
import jax
import jax.numpy as jnp
from jax import lax
from jax.experimental import pallas as pl
from jax.experimental.pallas import tpu as pltpu

F32 = jnp.float32
BF16 = jnp.bfloat16

D_MODEL = 2048
GRID_W = 64
NA_HEADS = 8
NA_HEAD_DIM = 128
NA_WIN_H = 8
NA_WIN_W = 16
RET_HEADS = 8
RET_QK_DIM = 64
RET_V_DIM = 128
NA_WIDTH = NA_HEADS * NA_HEAD_DIM
RET_QK_WIDTH = RET_HEADS * RET_QK_DIM
RET_V_WIDTH = RET_HEADS * RET_V_DIM
IN_WIDTH = 3 * NA_WIDTH + 2 * RET_QK_WIDTH + 2 * RET_V_WIDTH
ROPE_BASE = 10000.0
EPS = 1e-6
MASK_VALUE = -1e30
LOG2_E = float(np.log2(np.e))

OFF_NA_Q = 0
OFF_NA_K = NA_WIDTH
OFF_NA_V = 2 * NA_WIDTH
OFF_R_Q = 3 * NA_WIDTH
OFF_R_K = OFF_R_Q + RET_QK_WIDTH
OFF_R_V = OFF_R_K + RET_QK_WIDTH
OFF_R_G = OFF_R_V + RET_V_WIDTH

VMEM_LIMIT_BYTES = 56 * 1024 * 1024

RET_CHUNK = 256


def _params(*semantics):
    return pltpu.CompilerParams(dimension_semantics=semantics, vmem_limit_bytes=VMEM_LIMIT_BYTES)


def _dot(a, b):
    return jnp.dot(a, b, preferred_element_type=F32)


def _dot_nt(a, b):
    return lax.dot_general(a, b, (((1,), (1,)), ((), ())), preferred_element_type=F32)


def _silu(x):
    return x * jax.nn.sigmoid(x)


def _adaln_kernel(c_ref, w_ref, b_ref, o_ref):
    s = _silu(c_ref[...]).astype(BF16)
    o_ref[...] = _dot(s, w_ref[...].astype(BF16)) + b_ref[...]


def _adaln(cc, ada_w, ada_b, tn=1024):
    rows, d = cc.shape
    n = ada_w.shape[1]
    return pl.pallas_call(
        _adaln_kernel,
        grid=(n // tn,),
        in_specs=[
            pl.BlockSpec((rows, d), lambda j: (0, 0)),
            pl.BlockSpec((d, tn), lambda j: (0, j)),
            pl.BlockSpec((1, tn), lambda j: (0, j)),
        ],
        out_specs=pl.BlockSpec((rows, tn), lambda j: (0, j)),
        out_shape=jax.ShapeDtypeStruct((rows, n), F32),
        compiler_params=_params("arbitrary"),
        name="adaln",
    )(cc, ada_w, ada_b.reshape(1, n))


INPROJ_SUB_ROWS = 256


def _rope_rotate(x, cos, sin_signed):
    lane = lax.broadcasted_iota(jnp.int32, (1, 128), 1)
    first_half = (lane % (RET_QK_DIM // 2)) < (RET_QK_DIM // 4)
    partner = jnp.where(first_half, pltpu.roll(x, 128 - RET_QK_DIM // 4, 1), pltpu.roll(x, RET_QK_DIM // 4, 1))
    return x * cos + partner * sin_signed


def _inproj_kernel(x_hbm, g_ref, sh_ref, sc_ref, w_ref, *refs, first_tile_scale, rope_tile):
    if rope_tile is None:
        o_ref, h_ref, x_buf, x_sem = refs
    else:
        cos_ref, sin_ref, o_ref, h_ref, x_buf, x_sem = refs
    i = pl.program_id(0)
    j = pl.program_id(1)
    tm = h_ref.shape[0]
    n_slices = tm // INPROJ_SUB_ROWS
    slot = lax.rem(i, 2)

    def x_copy(tile, dst_slot):
        return pltpu.make_async_copy(x_hbm.at[pl.ds(tile * tm, tm), :], x_buf.at[dst_slot], x_sem.at[dst_slot])

    @pl.when((i == 0) & (j == 0))
    def _():
        x_copy(0, 0).start()

    @pl.when((j == 1) & (i + 1 < pl.num_programs(0)))
    def _():
        x_copy(i + 1, 1 - slot).start()

    @pl.when(j == 0)
    def _():
        x_copy(i, slot).wait()
        gain = g_ref[...] * (1.0 + sc_ref[...])
        shift = sh_ref[...]
        for s in range(n_slices):
            rows = pl.ds(s * INPROJ_SUB_ROWS, INPROJ_SUB_ROWS)
            x = x_buf[slot, rows, :]
            ms = jnp.mean(x * x, axis=-1, keepdims=True)
            h = (x * lax.rsqrt(ms + EPS) * gain + shift).astype(BF16)
            h_ref[rows, :] = h
            res = _dot(h, w_ref[...])
            if first_tile_scale != 1.0:
                res = res * first_tile_scale
            o_ref[rows, :] = res.astype(BF16)

    plain = (j > 0) if rope_tile is None else ((j > 0) & (j != rope_tile))

    @pl.when(plain)
    def _():
        o_ref[...] = _dot(h_ref[...], w_ref[...]).astype(BF16)

    if rope_tile is not None:
        @pl.when(j == rope_tile)
        def _():
            q_scale = RET_QK_DIM ** -0.5
            for s in range(n_slices):
                rows = pl.ds(s * INPROJ_SUB_ROWS, INPROJ_SUB_ROWS)
                res = _dot(h_ref[rows, :], w_ref[...])
                cos, sin = cos_ref[rows, :], sin_ref[rows, :]
                cos_q, sin_q = cos * q_scale, sin * q_scale
                for cg in range(o_ref.shape[1] // 128):
                    lanes = slice(cg * 128, (cg + 1) * 128)
                    is_q = cg < RET_QK_WIDTH // 128
                    y = _rope_rotate(res[:, lanes], cos_q if is_q else cos, sin_q if is_q else sin)
                    o_ref[rows, lanes] = y.astype(BF16)


def _inproj(x2, gain, mod3, w_bf16, *, rows_per_mod, mod_row0, col_tiles, tm, tn,
            first_tile_scale=1.0, rope_tile=None, rope_tables=None):
    m, d = x2.shape
    n_out = len(col_tiles) * tn
    first = col_tiles[0]
    gap_at = next((i for i in range(1, len(col_tiles)) if col_tiles[i] != col_tiles[i - 1] + 1), None)
    if gap_at is None:
        w_map = lambda i, j: (0, j + first)
    else:
        gap = col_tiles[gap_at] - col_tiles[gap_at - 1] - 1
        assert all(col_tiles[i] == first + i + (gap if i >= gap_at else 0) for i in range(len(col_tiles)))
        w_map = lambda i, j: (0, j + first + jnp.where(j >= gap_at, gap, 0))
    mod_row = lambda i: mod_row0 + (i * tm) // rows_per_mod
    assert len(col_tiles) >= 2
    in_specs = [
        pl.BlockSpec(memory_space=pl.ANY),
        pl.BlockSpec((1, d), lambda i, j: (0, 0)),
        pl.BlockSpec((None, 1, d), lambda i, j: (mod_row(i), 0, 0)),
        pl.BlockSpec((None, 1, d), lambda i, j: (mod_row(i), 0, 1)),
        pl.BlockSpec((d, tn), w_map),
    ]
    operands = [x2, gain, mod3, mod3, w_bf16]
    if rope_tile is not None:
        assert tn == 2 * RET_QK_WIDTH and col_tiles[rope_tile] * tn == OFF_R_Q and rows_per_mod % tm == 0
        pos_blk = pl.BlockSpec((tm, 128), lambda i, j: (i % (rows_per_mod // tm), 0))
        in_specs += [pos_blk, pos_blk]
        operands += list(rope_tables)
    return pl.pallas_call(
        functools.partial(_inproj_kernel, first_tile_scale=first_tile_scale, rope_tile=rope_tile),
        grid=(m // tm, len(col_tiles)),
        in_specs=in_specs,
        out_specs=pl.BlockSpec((tm, tn), lambda i, j: (i, j)),
        out_shape=jax.ShapeDtypeStruct((m, n_out), BF16),
        scratch_shapes=[pltpu.VMEM((tm, d), BF16), pltpu.VMEM((2, tm, d), F32), pltpu.SemaphoreType.DMA((2,))],
        compiler_params=_params("arbitrary", "arbitrary"),
        name="inproj",
    )(*operands)


NA_GROUP_ROWS = 4
NA_BAND_ROWS = NA_GROUP_ROWS + NA_WIN_H
NA_Q = NA_GROUP_ROWS * GRID_W
NA_BAND = NA_BAND_ROWS * GRID_W


def _na_band_start(g, rows):
    lo = g * NA_GROUP_ROWS - NA_WIN_H // 2
    if isinstance(g, (int, np.integer)):
        return min(max(lo, 0), rows - NA_BAND_ROWS)
    return jnp.minimum(jnp.maximum(lo, 0), rows - NA_BAND_ROWS)


def _na_row_patterns(rows):
    groups = rows // NA_GROUP_ROWS
    assert rows % NA_GROUP_ROWS == 0 and rows >= NA_BAND_ROWS and groups >= 3

    def row_pattern(g):
        r = g * NA_GROUP_ROWS + np.arange(NA_GROUP_ROWS)[:, None]
        krow = _na_band_start(g, rows) + np.arange(NA_BAND_ROWS)[None, :]
        rs = np.clip(r - NA_WIN_H // 2, 0, rows - NA_WIN_H)
        ok = (krow >= rs) & (krow < rs + NA_WIN_H)
        return ok, np.clip(krow - r + NA_WIN_H - 1, 0, 2 * NA_WIN_H - 2)

    pats = [row_pattern(0), row_pattern(1), row_pattern(groups - 1)]
    for g in range(1, groups - 1):
        ok, dr = row_pattern(g)
        assert (ok == pats[1][0]).all() and (dr[ok] == pats[1][1][ok]).all()
    return pats


NA_SCORE_AHEAD = 4


def _na_kernel(rpb_ref, q_ref, k_ref, v_ref, kc_ref, vc_ref, o_ref, s_ref, bias_ref, va_ref, vca_ref, *, rows):
    groups = rows // NA_GROUP_ROWS
    hd = NA_HEAD_DIM

    @pl.when(pl.program_id(1) == 0)
    def _():
        n_dr, n_dc = 2 * NA_WIN_H - 1, 2 * NA_WIN_W - 1
        base = pl.program_id(0) * (n_dr * n_dc)
        c = lax.broadcasted_iota(jnp.int32, (GRID_W, GRID_W), 0)
        kc = lax.broadcasted_iota(jnp.int32, (GRID_W, GRID_W), 1)
        col_start = jnp.minimum(jnp.maximum(c - NA_WIN_W // 2, 0), GRID_W - NA_WIN_W)
        col_ok = (kc >= col_start) & (kc < col_start + NA_WIN_W)
        dc = kc - c + (NA_WIN_W - 1)
        masked = jnp.full((GRID_W, GRID_W), MASK_VALUE, F32)
        pats = _na_row_patterns(rows)
        used = sorted({int(dr[a, j]) for ok, dr in pats for a, j in zip(*np.nonzero(ok))})
        tiles = {}
        for d in used:
            t = jnp.zeros((GRID_W, GRID_W), F32)
            for x in range(n_dc):
                t = jnp.where(dc == x, rpb_ref[base + d * n_dc + x], t)
            tiles[d] = jnp.where(col_ok, t * LOG2_E, MASK_VALUE)
        for pi, (ok, dr) in enumerate(pats):
            for a in range(NA_GROUP_ROWS):
                for j in range(NA_BAND_ROWS):
                    tile = tiles[int(dr[a, j])] if ok[a, j] else masked
                    bias_ref[pi, a * GRID_W:(a + 1) * GRID_W, j * GRID_W:(j + 1) * GRID_W] = tile

    va_ref[:, :hd] = v_ref[...]
    va_ref[:, hd:] = jnp.ones((va_ref.shape[0], hd), BF16)
    vca_ref[:, :hd] = vc_ref[...]
    vca_ref[:, hd:] = jnp.ones((vca_ref.shape[0], hd), BF16)

    def scores(g, s_ref):
        q0 = g * NA_Q
        k0 = _na_band_start(g, rows) * GRID_W
        pattern = 0 if g == 0 else (2 if g == groups - 1 else 1)
        q = q_ref[pl.ds(q0, NA_Q), :]
        s_ref[:, :NA_BAND] = _dot_nt(q, k_ref[pl.ds(k0, NA_BAND), :]) + bias_ref[pattern]
        s_ref[:, NA_BAND:] = _dot_nt(q, kc_ref[...])

    def attend(g, s_ref):
        q0 = g * NA_Q
        k0 = _na_band_start(g, rows) * GRID_W
        s = s_ref[...]
        pb = jnp.exp2(s - jnp.max(s, axis=-1, keepdims=True)).astype(BF16)
        o = _dot(pb[:, :NA_BAND], va_ref[pl.ds(k0, NA_BAND), :]) + _dot(pb[:, NA_BAND:], vca_ref[...])
        o_ref[pl.ds(q0, NA_Q), :] = (o[:, :hd] / o[:, hd:hd + 1]).astype(BF16)

    for g in range(min(NA_SCORE_AHEAD, groups)):
        scores(g, s_ref.at[g])
    for g in range(groups):
        if g + NA_SCORE_AHEAD < groups:
            scores(g + NA_SCORE_AHEAD, s_ref.at[g + NA_SCORE_AHEAD])
        attend(g, s_ref.at[g])


def _neighbourhood_attention(p, pc, pc_off_k, pc_off_v, rpb):
    b, l, _ = p.shape
    lc = pc.shape[1]
    rows = l // GRID_W
    hd = NA_HEAD_DIM
    assert rpb.shape == (NA_HEADS, 2 * NA_WIN_H - 1, 2 * NA_WIN_W - 1)
    blk = lambda off: (lambda h, bi: (bi, 0, off // hd + h))
    return pl.pallas_call(
        functools.partial(_na_kernel, rows=rows),
        grid=(NA_HEADS, b),
        in_specs=[
            pl.BlockSpec(memory_space=pltpu.SMEM),
            pl.BlockSpec((None, l, hd), blk(OFF_NA_Q)),
            pl.BlockSpec((None, l, hd), blk(OFF_NA_K)),
            pl.BlockSpec((None, l, hd), blk(OFF_NA_V)),
            pl.BlockSpec((None, lc, hd), blk(pc_off_k)),
            pl.BlockSpec((None, lc, hd), blk(pc_off_v)),
        ],
        out_specs=pl.BlockSpec((None, l, hd), lambda h, bi: (bi, 0, h)),
        out_shape=jax.ShapeDtypeStruct((b, l, NA_WIDTH), BF16),
        scratch_shapes=[pltpu.VMEM((rows // NA_GROUP_ROWS, NA_Q, NA_BAND + lc), F32),
                        pltpu.VMEM((3, NA_Q, NA_BAND), F32),
                        pltpu.VMEM((l, 2 * hd), BF16),
                        pltpu.VMEM((lc, 2 * hd), BF16)],
        compiler_params=_params("parallel", "arbitrary"),
        name="na",
    )(rpb.astype(F32).reshape(-1), p, p, p, pc, pc)


def _rope_tables(l):
    t = np.arange(l)
    pos_row, pos_col = t // GRID_W, t % GRID_W
    quarter = RET_QK_DIM // 4
    inv_freq = ROPE_BASE ** (-np.arange(quarter, dtype=np.float64) / quarter)
    ang_r = pos_row[:, None] * inv_freq
    ang_c = pos_col[:, None] * inv_freq
    cos = np.concatenate([np.cos(ang_r), np.cos(ang_r), np.cos(ang_c), np.cos(ang_c)], axis=-1)
    sin = np.concatenate([-np.sin(ang_r), np.sin(ang_r), -np.sin(ang_c), np.sin(ang_c)], axis=-1)
    reps = 128 // RET_QK_DIM
    return (jnp.asarray(np.tile(cos, (1, reps)), F32), jnp.asarray(np.tile(sin, (1, reps)), F32))


def _ret_tables_kernel(lgf_ref, lgb_ref, dk_ref, dq_ref, dbi_ref, cd_ref):
    C = RET_CHUNK
    h = pl.program_id(0)
    lgf = lgf_ref[h]
    lgb = lgb_ref[h]
    lane = lax.broadcasted_iota(jnp.int32, (1, 128), 1)
    head_lanes = (lane // RET_QK_DIM) == (h % 2)
    row = lax.broadcasted_iota(jnp.int32, (C, 128), 0).astype(F32)
    diff = (lax.broadcasted_iota(jnp.int32, (C, C), 0) - lax.broadcasted_iota(jnp.int32, (C, C), 1)).astype(F32)
    dk_ref[...] = jnp.where(head_lanes, jnp.exp(lgf * (C - 1.0 - row)), jnp.exp(lgb * row))
    dq_ref[...] = jnp.where(head_lanes, jnp.exp(lgf * (row + 1.0)), jnp.exp(lgb * (C - row)))
    dbi_ref[...] = (jnp.where(diff >= 0, jnp.exp(lgf * jnp.maximum(diff, 0.0)), 0.0)
                    + jnp.where(diff <= 0, jnp.exp(lgb * jnp.maximum(-diff, 0.0)), 0.0))
    sub = lax.broadcasted_iota(jnp.int32, (8, 128), 0)
    cd_ref[...] = jnp.exp(jnp.where(sub == 0, lgf, lgb) * jnp.full((8, 128), float(C), F32))


def _ret_tables(lg_f, lg_b):
    C = RET_CHUNK
    smem = pl.BlockSpec(memory_space=pltpu.SMEM)
    per_head = lambda *shape: pl.BlockSpec((None,) + shape, lambda h: (h,) + (0,) * len(shape))
    return pl.pallas_call(
        _ret_tables_kernel,
        grid=(RET_HEADS,),
        in_specs=[smem, smem],
        out_specs=[per_head(C, 128), per_head(C, 128), per_head(C, C), per_head(8, 128)],
        out_shape=[jax.ShapeDtypeStruct((RET_HEADS, C, 128), F32), jax.ShapeDtypeStruct((RET_HEADS, C, 128), F32),
                   jax.ShapeDtypeStruct((RET_HEADS, C, C), F32), jax.ShapeDtypeStruct((RET_HEADS, 8, 128), F32)],
        compiler_params=_params("arbitrary"),
        name="ret_tables",
    )(lg_f, lg_b)


def _ret_kernel(q_ref, k_ref, v_ref, g_ref, kc_ref, vc_ref, dk_ref, dq_ref, dbi_ref, cd_ref,
                o_ref, a_ref, sb_ref, st_ref, *, n_chunks):
    C = RET_CHUNK
    lane = lax.broadcasted_iota(jnp.int32, (1, 128), 1)
    sub = lax.broadcasted_iota(jnp.int32, (128, 1), 0)

    for e in range(2):
        head_lanes = (lane // RET_QK_DIM) == e
        fwd_rows = (sub // RET_QK_DIM) == e
        vsl = slice(e * RET_V_DIM, (e + 1) * RET_V_DIM)
        dk = dk_ref[e]
        cdf = cd_ref[e, 0:1, :]
        cdb = cd_ref[e, 1:2, :]

        def local_state(k_bf16, v_bf16):
            kf32 = k_bf16.astype(F32)
            k_both = jnp.where(head_lanes, kf32, pltpu.roll(kf32, RET_QK_DIM, 1))
            return _dot((k_both * dk).T.astype(BF16), v_bf16)

        ctx_state = local_state(kc_ref[...], vc_ref[:, vsl])
        for i in range(n_chunks):
            rows = pl.ds(i * C, C)
            a_ref[i] = local_state(k_ref[rows, :], v_ref[rows, vsl])

        s_b = ctx_state
        for i in reversed(range(n_chunks)):
            sb_ref[i] = s_b
            s_b = s_b * cdb + a_ref[i]
        s_f = ctx_state
        for i in range(n_chunks):
            st_ref[e, i] = jnp.where(fwd_rows, s_f, sb_ref[i]).astype(BF16)
            s_f = s_f * cdf + a_ref[i]

    for i in range(n_chunks):
        rows = pl.ds(i * C, C)
        q = q_ref[rows, :].astype(F32)
        q_rolled = pltpu.roll(q, RET_QK_DIM, 1)
        kk = k_ref[rows, :]
        for e in range(2):
            head_lanes = (lane // RET_QK_DIM) == e
            vsl = slice(e * RET_V_DIM, (e + 1) * RET_V_DIM)
            q_own = jnp.where(head_lanes, q, 0.0).astype(BF16)
            q_both = (jnp.where(head_lanes, q, q_rolled) * dq_ref[e]).astype(BF16)
            inner = _dot_nt(q_own, kk) * dbi_ref[e]
            o = _dot(inner.astype(BF16), v_ref[rows, vsl]) + _dot(q_both, st_ref[e, i])
            mu = jnp.mean(o, axis=-1, keepdims=True)
            oc = o - mu
            var = jnp.mean(oc * oc, axis=-1, keepdims=True)
            gate = _silu(g_ref[rows, vsl].astype(F32))
            o_ref[rows, vsl] = (oc * lax.rsqrt(var + EPS) * gate).astype(BF16)


def _retention(p, pc, pc_off_k, pc_off_v, lg_f, lg_b):
    b, l, _ = p.shape
    lc = pc.shape[1]
    assert lc == RET_CHUNK and l % RET_CHUNK == 0
    n_chunks = l // RET_CHUNK
    dk, dq, dbi, cd = _ret_tables(lg_f, lg_b)
    qk_blk = lambda off: (lambda hp, bi: (bi, 0, off // 128 + hp))
    v_blk = lambda off: (lambda hp, bi: (bi, 0, off // 256 + hp))
    pair = lambda *shape: pl.BlockSpec((2,) + shape, lambda hp, bi: (hp,) + (0,) * len(shape))
    return pl.pallas_call(
        functools.partial(_ret_kernel, n_chunks=n_chunks),
        grid=(RET_HEADS // 2, b),
        in_specs=[
            pl.BlockSpec((None, l, 128), qk_blk(OFF_R_Q)),
            pl.BlockSpec((None, l, 128), qk_blk(OFF_R_K)),
            pl.BlockSpec((None, l, 256), v_blk(OFF_R_V)),
            pl.BlockSpec((None, l, 256), v_blk(OFF_R_G)),
            pl.BlockSpec((None, lc, 128), qk_blk(pc_off_k)),
            pl.BlockSpec((None, lc, 256), v_blk(pc_off_v)),
            pair(RET_CHUNK, 128), pair(RET_CHUNK, 128), pair(RET_CHUNK, RET_CHUNK), pair(8, 128),
        ],
        out_specs=pl.BlockSpec((None, l, 256), lambda hp, bi: (bi, 0, hp)),
        out_shape=jax.ShapeDtypeStruct((b, l, RET_V_WIDTH), BF16),
        scratch_shapes=[
            pltpu.VMEM((n_chunks, 128, RET_V_DIM), F32),
            pltpu.VMEM((n_chunks, 128, RET_V_DIM), F32),
            pltpu.VMEM((2, n_chunks, 128, RET_V_DIM), BF16),
        ],
        compiler_params=_params("parallel", "parallel"),
        name="ret",
    )(p, p, p, p, pc, pc, dk, dq, dbi, cd)


OUTPROJ_SUB_ROWS = 256


def _outproj_kernel(na_ref, ret_ref, wt_ref, wb_ref, x_ref, g1_ref, sh2_ref, sc2_ref, npost_ref, npre_ref,
                    xn_ref, h2_ref):
    gate_gain = g1_ref[...] * npost_ref[...]
    gain2 = npre_ref[...] * (1.0 + sc2_ref[...])
    shift2 = sh2_ref[...]
    for s in range(x_ref.shape[0] // OUTPROJ_SUB_ROWS):
        rows = pl.ds(s * OUTPROJ_SUB_ROWS, OUTPROJ_SUB_ROWS)
        mix = _dot(na_ref[rows, :], wt_ref[...]) + _dot(ret_ref[rows, :], wb_ref[...])
        ms = jnp.mean(mix * mix, axis=-1, keepdims=True)
        xn = x_ref[rows, :] + mix * lax.rsqrt(ms + EPS) * gate_gain
        xn_ref[rows, :] = xn
        ms2 = jnp.mean(xn * xn, axis=-1, keepdims=True)
        h2_ref[rows, :] = (xn * lax.rsqrt(ms2 + EPS) * gain2 + shift2).astype(BF16)


def _outproj(na, ret, w_out_bf16, x2, mod3, norm_post, norm_pre, *, rows_per_mod, tm):
    m, d = x2.shape
    kh = na.shape[1]
    mod_row = lambda i: (i * tm) // rows_per_mod
    mod_blk = lambda k: pl.BlockSpec((None, 1, d), lambda i: (mod_row(i), 0, k))
    return pl.pallas_call(
        _outproj_kernel,
        grid=(m // tm,),
        in_specs=[
            pl.BlockSpec((tm, kh), lambda i: (i, 0)),
            pl.BlockSpec((tm, kh), lambda i: (i, 0)),
            pl.BlockSpec((kh, d), lambda i: (0, 0)),
            pl.BlockSpec((kh, d), lambda i: (1, 0)),
            pl.BlockSpec((tm, d), lambda i: (i, 0)),
            mod_blk(2), mod_blk(3), mod_blk(4),
            pl.BlockSpec((1, d), lambda i: (0, 0)),
            pl.BlockSpec((1, d), lambda i: (0, 0)),
        ],
        out_specs=[pl.BlockSpec((tm, d), lambda i: (i, 0)), pl.BlockSpec((tm, d), lambda i: (i, 0))],
        out_shape=[jax.ShapeDtypeStruct((m, d), F32), jax.ShapeDtypeStruct((m, d), BF16)],
        compiler_params=_params("parallel"),
        name="outproj",
    )(na, ret, w_out_bf16, w_out_bf16, x2, mod3, mod3, mod3, norm_post, norm_pre)


FFN_SUB_ROWS = 256
FFN_RESIDUAL_CHUNKS = 4


def _ffn_kernel(h_ref, wg_ref, wu_ref, wd_ref, xn_hbm, g2_ref, npost_ref, o_ref, xn_buf, xn_sem):
    i = pl.program_id(0)
    f = pl.program_id(1)
    tm = h_ref.shape[0]
    chunk = tm // FFN_RESIDUAL_CHUNKS

    def residual_copy(k):
        return pltpu.make_async_copy(xn_hbm.at[pl.ds(i * tm + k * chunk, chunk), :],
                                     xn_buf.at[pl.ds(k * chunk, chunk), :], xn_sem.at[k])

    @pl.when(f == 0)
    def _():
        o_ref[...] = jnp.zeros_like(o_ref)

    for k in range(FFN_RESIDUAL_CHUNKS):
        @pl.when(f == k + 1)
        def _():
            residual_copy(k).start()

    h = h_ref[...]
    t = (_silu(_dot(h, wg_ref[...])) * _dot(h, wu_ref[...])).astype(BF16)
    o_ref[...] += _dot(t, wd_ref[...])

    @pl.when(f == pl.num_programs(1) - 1)
    def _():
        for k in range(FFN_RESIDUAL_CHUNKS):
            residual_copy(k).wait()
        gate_gain = g2_ref[...] * npost_ref[...]
        for s in range(tm // FFN_SUB_ROWS):
            rows = pl.ds(s * FFN_SUB_ROWS, FFN_SUB_ROWS)
            y = o_ref[rows, :]
            ms = jnp.mean(y * y, axis=-1, keepdims=True)
            o_ref[rows, :] = xn_buf[rows, :] + y * lax.rsqrt(ms + EPS) * gate_gain


def _ffn(h2, wg, wu, wd, xn, mod3, norm_post, *, rows_per_mod, tm, tf):
    m, d = h2.shape
    dff = wd.shape[0]
    mod_row = lambda i: (i * tm) // rows_per_mod
    assert dff // tf >= FFN_RESIDUAL_CHUNKS + 2 and tm % FFN_RESIDUAL_CHUNKS == 0
    return pl.pallas_call(
        _ffn_kernel,
        grid=(m // tm, dff // tf),
        in_specs=[
            pl.BlockSpec((tm, d), lambda i, f: (i, 0)),
            pl.BlockSpec((d, tf), lambda i, f: (0, f)),
            pl.BlockSpec((d, tf), lambda i, f: (0, f)),
            pl.BlockSpec((tf, d), lambda i, f: (f, 0)),
            pl.BlockSpec(memory_space=pl.ANY),
            pl.BlockSpec((None, 1, d), lambda i, f: (mod_row(i), 0, 5)),
            pl.BlockSpec((1, d), lambda i, f: (0, 0)),
        ],
        out_specs=pl.BlockSpec((tm, d), lambda i, f: (i, 0)),
        out_shape=jax.ShapeDtypeStruct((m, d), F32),
        scratch_shapes=[pltpu.VMEM((tm, d), F32), pltpu.SemaphoreType.DMA((FFN_RESIDUAL_CHUNKS,))],
        compiler_params=_params("parallel", "arbitrary"),
        name="ffn",
    )(h2, wg, wu, wd, xn, mod3, norm_post)


def kernel(x, c, ctx, c_ctx, ada_w, ada_b, norm_pre_mix, norm_post_mix, norm_pre_ffn, norm_post_ffn, w_in,
           na_rpb, ret_log_gamma_fwd, ret_log_gamma_bwd, w_out, w_gate, w_up, w_down):
    b, l, d = x.shape
    lc = ctx.shape[1]
    depth = ada_w.shape[0]
    assert depth == 1, "the context-stream update between layers is not implemented"
    li = 0

    mod_rows = -(-(b + 1) // 8) * 8
    cc = jnp.concatenate([c, c_ctx[None, :], jnp.zeros((mod_rows - b - 1, d), F32)], axis=0)
    mod3 = _adaln(cc, ada_w[li], ada_b[li]).reshape(mod_rows, 1, 6 * d)

    w_in_b = w_in[li].astype(BF16)
    x2 = x.reshape(b * l, d)
    ctx2 = ctx.reshape(b * lc, d)
    gain_pre = norm_pre_mix[li].reshape(1, d)

    p = _inproj(x2, gain_pre, mod3, w_in_b, rows_per_mod=l, mod_row0=0,
                col_tiles=tuple(range(IN_WIDTH // 1024)), tm=1024, tn=1024,
                first_tile_scale=NA_HEAD_DIM ** -0.5 * LOG2_E, rope_tile=OFF_R_Q // 1024, rope_tables=_rope_tables(l))
    tn = 512
    ctx_tiles = tuple(range(OFF_NA_K // tn, OFF_R_Q // tn)) + tuple(range(OFF_R_K // tn, OFF_R_G // tn))
    pc = _inproj(ctx2, gain_pre, mod3, w_in_b, rows_per_mod=b * lc, mod_row0=b,
                 col_tiles=ctx_tiles, tm=1024, tn=tn)
    p = p.reshape(b, l, IN_WIDTH)
    pc = pc.reshape(b, lc, len(ctx_tiles) * tn)
    pc_na_k, pc_na_v = 0, NA_WIDTH
    pc_r_k, pc_r_v = 2 * NA_WIDTH, 2 * NA_WIDTH + RET_QK_WIDTH

    o_na = _neighbourhood_attention(p, pc, pc_na_k, pc_na_v, na_rpb[li])
    o_ret = _retention(p, pc, pc_r_k, pc_r_v, ret_log_gamma_fwd[li].astype(F32), ret_log_gamma_bwd[li].astype(F32))

    x_new, h2 = _outproj(o_na.reshape(b * l, NA_WIDTH), o_ret.reshape(b * l, RET_V_WIDTH), w_out[li].astype(BF16),
                         x2, mod3, norm_post_mix[li].reshape(1, d), norm_pre_ffn[li].reshape(1, d),
                         rows_per_mod=l, tm=512)
    out = _ffn(h2, w_gate[li].astype(BF16), w_up[li].astype(BF16), w_down[li].astype(BF16), x_new, mod3,
               norm_post_ffn[li].reshape(1, d), rows_per_mod=l, tm=1024, tf=512)
    return out.reshape(b, l, d)
```

```python
import functools

import numpy as np
import jax
import jax.numpy as jnp
from jax import lax
from jax.experimental import pallas as pl
from jax.experimental.pallas import tpu as pltpu

F32 = jnp.float32
BF16 = jnp.bfloat16

D_MODEL = 2048
GRID_W = 64
NA_HEADS = 8
NA_HEAD_DIM = 128
NA_WIN_H = 8
NA_WIN_W = 16
RET_HEADS = 8
RET_QK_DIM = 64
RET_V_DIM = 128
NA_WIDTH = NA_HEADS * NA_HEAD_DIM
RET_QK_WIDTH = RET_HEADS * RET_QK_DIM
RET_V_WIDTH = RET_HEADS * RET_V_DIM
IN_WIDTH = 3 * NA_WIDTH + 2 * RET_QK_WIDTH + 2 * RET_V_WIDTH
ROPE_BASE = 10000.0
EPS = 1e-6
MASK_VALUE = -1e30
LOG2_E = float(np.log2(np.e))

OFF_NA_Q = 0
OFF_NA_K = NA_WIDTH
OFF_NA_V = 2 * NA_WIDTH
OFF_R_Q = 3 * NA_WIDTH
OFF_R_K = OFF_R_Q + RET_QK_WIDTH
OFF_R_V = OFF_R_K + RET_QK_WIDTH
OFF_R_G = OFF_R_V + RET_V_WIDTH

VMEM_LIMIT_BYTES = 56 * 1024 * 1024

RET_CHUNK = 256


def _params(*semantics):
    return pltpu.CompilerParams(dimension_semantics=semantics, vmem_limit_bytes=VMEM_LIMIT_BYTES)


def _dot(a, b):
    return jnp.dot(a, b, preferred_element_type=F32)


def _dot_nt(a, b):
    return lax.dot_general(a, b, (((1,), (1,)), ((), ())), preferred_element_type=F32)


def _silu(x):
    return x * jax.nn.sigmoid(x)


def _adaln_kernel(c_ref, w_ref, b_ref, o_ref):
    s = _silu(c_ref[...]).astype(BF16)
    o_ref[...] = _dot(s, w_ref[...].astype(BF16)) + b_ref[...]


def _adaln(cc, ada_w, ada_b, tn=1024):
    rows, d = cc.shape
    n = ada_w.shape[1]
    return pl.pallas_call(
        _adaln_kernel,
        grid=(n // tn,),
        in_specs=[
            pl.BlockSpec((rows, d), lambda j: (0, 0)),
            pl.BlockSpec((d, tn), lambda j: (0, j)),
            pl.BlockSpec((1, tn), lambda j: (0, j)),
        ],
        out_specs=pl.BlockSpec((rows, tn), lambda j: (0, j)),
        out_shape=jax.ShapeDtypeStruct((rows, n), F32),
        compiler_params=_params("arbitrary"),
        name="adaln",
    )(cc, ada_w, ada_b.reshape(1, n))


INPROJ_SUB_ROWS = 256


def _rope_rotate(x, cos, sin_signed):
    lane = lax.broadcasted_iota(jnp.int32, (1, 128), 1)
    first_half = (lane % (RET_QK_DIM // 2)) < (RET_QK_DIM // 4)
    partner = jnp.where(first_half, pltpu.roll(x, 128 - RET_QK_DIM // 4, 1), pltpu.roll(x, RET_QK_DIM // 4, 1))
    return x * cos + partner * sin_signed


def _inproj_kernel(x_hbm, g_ref, sh_ref, sc_ref, w_ref, *refs, first_tile_scale, rope_tile):
    if rope_tile is None:
        o_ref, h_ref, x_buf, x_sem = refs
    else:
        cos_ref, sin_ref, o_ref, h_ref, x_buf, x_sem = refs
    i = pl.program_id(0)
    j = pl.program_id(1)
    tm = h_ref.shape[0]
    n_slices = tm // INPROJ_SUB_ROWS
    slot = lax.rem(i, 2)

    def x_copy(tile, dst_slot):
        return pltpu.make_async_copy(x_hbm.at[pl.ds(tile * tm, tm), :], x_buf.at[dst_slot], x_sem.at[dst_slot])

    @pl.when((i == 0) & (j == 0))
    def _():
        x_copy(0, 0).start()

    @pl.when((j == 1) & (i + 1 < pl.num_programs(0)))
    def _():
        x_copy(i + 1, 1 - slot).start()

    @pl.when(j == 0)
    def _():
        x_copy(i, slot).wait()
        gain = g_ref[...] * (1.0 + sc_ref[...])
        shift = sh_ref[...]
        for s in range(n_slices):
            rows = pl.ds(s * INPROJ_SUB_ROWS, INPROJ_SUB_ROWS)
            x = x_buf[slot, rows, :]
            ms = jnp.mean(x * x, axis=-1, keepdims=True)
            h = (x * lax.rsqrt(ms + EPS) * gain + shift).astype(BF16)
            h_ref[rows, :] = h
            res = _dot(h, w_ref[...])
            if first_tile_scale != 1.0:
                res = res * first_tile_scale
            o_ref[rows, :] = res.astype(BF16)

    plain = (j > 0) if rope_tile is None else ((j > 0) & (j != rope_tile))

    @pl.when(plain)
    def _():
        o_ref[...] = _dot(h_ref[...], w_ref[...]).astype(BF16)

    if rope_tile is not None:
        @pl.when(j == rope_tile)
        def _():
            q_scale = RET_QK_DIM ** -0.5
            for s in range(n_slices):
                rows = pl.ds(s * INPROJ_SUB_ROWS, INPROJ_SUB_ROWS)
                res = _dot(h_ref[rows, :], w_ref[...])
                cos, sin = cos_ref[rows, :], sin_ref[rows, :]
                cos_q, sin_q = cos * q_scale, sin * q_scale
                for cg in range(o_ref.shape[1] // 128):
                    lanes = slice(cg * 128, (cg + 1) * 128)
                    is_q = cg < RET_QK_WIDTH // 128
                    y = _rope_rotate(res[:, lanes], cos_q if is_q else cos, sin_q if is_q else sin)
                    o_ref[rows, lanes] = y.astype(BF16)


def _inproj(x2, gain, mod3, w_bf16, *, rows_per_mod, mod_row0, col_tiles, tm, tn,
            first_tile_scale=1.0, rope_tile=None, rope_tables=None):
    m, d = x2.shape
    n_out = len(col_tiles) * tn
    first = col_tiles[0]
    gap_at = next((i for i in range(1, len(col_tiles)) if col_tiles[i] != col_tiles[i - 1] + 1), None)
    if gap_at is None:
        w_map = lambda i, j: (0, j + first)
    else:
        gap = col_tiles[gap_at] - col_tiles[gap_at - 1] - 1
        assert all(col_tiles[i] == first + i + (gap if i >= gap_at else 0) for i in range(len(col_tiles)))
        w_map = lambda i, j: (0, j + first + jnp.where(j >= gap_at, gap, 0))
    mod_row = lambda i: mod_row0 + (i * tm) // rows_per_mod
    assert len(col_tiles) >= 2
    in_specs = [
        pl.BlockSpec(memory_space=pl.ANY),
        pl.BlockSpec((1, d), lambda i, j: (0, 0)),
        pl.BlockSpec((None, 1, d), lambda i, j: (mod_row(i), 0, 0)),
        pl.BlockSpec((None, 1, d), lambda i, j: (mod_row(i), 0, 1)),
        pl.BlockSpec((d, tn), w_map),
    ]
    operands = [x2, gain, mod3, mod3, w_bf16]
    if rope_tile is not None:
        assert tn == 2 * RET_QK_WIDTH and col_tiles[rope_tile] * tn == OFF_R_Q and rows_per_mod % tm == 0
        pos_blk = pl.BlockSpec((tm, 128), lambda i, j: (i % (rows_per_mod // tm), 0))
        in_specs += [pos_blk, pos_blk]
        operands += list(rope_tables)
    return pl.pallas_call(
        functools.partial(_inproj_kernel, first_tile_scale=first_tile_scale, rope_tile=rope_tile),
        grid=(m // tm, len(col_tiles)),
        in_specs=in_specs,
        out_specs=pl.BlockSpec((tm, tn), lambda i, j: (i, j)),
        out_shape=jax.ShapeDtypeStruct((m, n_out), BF16),
        scratch_shapes=[pltpu.VMEM((tm, d), BF16), pltpu.VMEM((2, tm, d), F32), pltpu.SemaphoreType.DMA((2,))],
        compiler_params=_params("arbitrary", "arbitrary"),
        name="inproj",
    )(*operands)


NA_GROUP_ROWS = 4
NA_BAND_ROWS = NA_GROUP_ROWS + NA_WIN_H
NA_Q = NA_GROUP_ROWS * GRID_W
NA_BAND = NA_BAND_ROWS * GRID_W


def _na_band_start(g, rows):
    lo = g * NA_GROUP_ROWS - NA_WIN_H // 2
    if isinstance(g, (int, np.integer)):
        return min(max(lo, 0), rows - NA_BAND_ROWS)
    return jnp.minimum(jnp.maximum(lo, 0), rows - NA_BAND_ROWS)


def _na_row_patterns(rows):
    groups = rows // NA_GROUP_ROWS
    assert rows % NA_GROUP_ROWS == 0 and rows >= NA_BAND_ROWS and groups >= 3

    def row_pattern(g):
        r = g * NA_GROUP_ROWS + np.arange(NA_GROUP_ROWS)[:, None]
        krow = _na_band_start(g, rows) + np.arange(NA_BAND_ROWS)[None, :]
        rs = np.clip(r - NA_WIN_H // 2, 0, rows - NA_WIN_H)
        ok = (krow >= rs) & (krow < rs + NA_WIN_H)
        return ok, np.clip(krow - r + NA_WIN_H - 1, 0, 2 * NA_WIN_H - 2)

    pats = [row_pattern(0), row_pattern(1), row_pattern(groups - 1)]
    for g in range(1, groups - 1):
        ok, dr = row_pattern(g)
        assert (ok == pats[1][0]).all() and (dr[ok] == pats[1][1][ok]).all()
    return pats


NA_SCORE_AHEAD = 4


def _na_kernel(rpb_ref, q_ref, k_ref, v_ref, kc_ref, vc_ref, o_ref, s_ref, bias_ref, va_ref, vca_ref, *, rows):
    groups = rows // NA_GROUP_ROWS
    hd = NA_HEAD_DIM

    @pl.when(pl.program_id(1) == 0)
    def _():
        n_dr, n_dc = 2 * NA_WIN_H - 1, 2 * NA_WIN_W - 1
        base = pl.program_id(0) * (n_dr * n_dc)
        c = lax.broadcasted_iota(jnp.int32, (GRID_W, GRID_W), 0)
        kc = lax.broadcasted_iota(jnp.int32, (GRID_W, GRID_W), 1)
        col_start = jnp.minimum(jnp.maximum(c - NA_WIN_W // 2, 0), GRID_W - NA_WIN_W)
        col_ok = (kc >= col_start) & (kc < col_start + NA_WIN_W)
        dc = kc - c + (NA_WIN_W - 1)
        masked = jnp.full((GRID_W, GRID_W), MASK_VALUE, F32)
        pats = _na_row_patterns(rows)
        used = sorted({int(dr[a, j]) for ok, dr in pats for a, j in zip(*np.nonzero(ok))})
        tiles = {}
        for d in used:
            t = jnp.zeros((GRID_W, GRID_W), F32)
            for x in range(n_dc):
                t = jnp.where(dc == x, rpb_ref[base + d * n_dc + x], t)
            tiles[d] = jnp.where(col_ok, t * LOG2_E, MASK_VALUE)
        for pi, (ok, dr) in enumerate(pats):
            for a in range(NA_GROUP_ROWS):
                for j in range(NA_BAND_ROWS):
                    tile = tiles[int(dr[a, j])] if ok[a, j] else masked
                    bias_ref[pi, a * GRID_W:(a + 1) * GRID_W, j * GRID_W:(j + 1) * GRID_W] = tile

    va_ref[:, :hd] = v_ref[...]
    va_ref[:, hd:] = jnp.ones((va_ref.shape[0], hd), BF16)
    vca_ref[:, :hd] = vc_ref[...]
    vca_ref[:, hd:] = jnp.ones((vca_ref.shape[0], hd), BF16)

    def scores(g, s_ref):
        q0 = g * NA_Q
        k0 = _na_band_start(g, rows) * GRID_W
        pattern = 0 if g == 0 else (2 if g == groups - 1 else 1)
        q = q_ref[pl.ds(q0, NA_Q), :]
        s_ref[:, :NA_BAND] = _dot_nt(q, k_ref[pl.ds(k0, NA_BAND), :]) + bias_ref[pattern]
        s_ref[:, NA_BAND:] = _dot_nt(q, kc_ref[...])

    def attend(g, s_ref):
        q0 = g * NA_Q
        k0 = _na_band_start(g, rows) * GRID_W
        s = s_ref[...]
        pb = jnp.exp2(s - jnp.max(s, axis=-1, keepdims=True)).astype(BF16)
        o = _dot(pb[:, :NA_BAND], va_ref[pl.ds(k0, NA_BAND), :]) + _dot(pb[:, NA_BAND:], vca_ref[...])
        o_ref[pl.ds(q0, NA_Q), :] = (o[:, :hd] / o[:, hd:hd + 1]).astype(BF16)

    for g in range(min(NA_SCORE_AHEAD, groups)):
        scores(g, s_ref.at[g])
    for g in range(groups):
        if g + NA_SCORE_AHEAD < groups:
            scores(g + NA_SCORE_AHEAD, s_ref.at[g + NA_SCORE_AHEAD])
        attend(g, s_ref.at[g])


def _neighbourhood_attention(p, pc, pc_off_k, pc_off_v, rpb):
    b, l, _ = p.shape
    lc = pc.shape[1]
    rows = l // GRID_W
    hd = NA_HEAD_DIM
    assert rpb.shape == (NA_HEADS, 2 * NA_WIN_H - 1, 2 * NA_WIN_W - 1)
    blk = lambda off: (lambda h, bi: (bi, 0, off // hd + h))
    return pl.pallas_call(
        functools.partial(_na_kernel, rows=rows),
        grid=(NA_HEADS, b),
        in_specs=[
            pl.BlockSpec(memory_space=pltpu.SMEM),
            pl.BlockSpec((None, l, hd), blk(OFF_NA_Q)),
            pl.BlockSpec((None, l, hd), blk(OFF_NA_K)),
            pl.BlockSpec((None, l, hd), blk(OFF_NA_V)),
            pl.BlockSpec((None, lc, hd), blk(pc_off_k)),
            pl.BlockSpec((None, lc, hd), blk(pc_off_v)),
        ],
        out_specs=pl.BlockSpec((None, l, hd), lambda h, bi: (bi, 0, h)),
        out_shape=jax.ShapeDtypeStruct((b, l, NA_WIDTH), BF16),
        scratch_shapes=[pltpu.VMEM((rows // NA_GROUP_ROWS, NA_Q, NA_BAND + lc), F32),
                        pltpu.VMEM((3, NA_Q, NA_BAND), F32),
                        pltpu.VMEM((l, 2 * hd), BF16),
                        pltpu.VMEM((lc, 2 * hd), BF16)],
        compiler_params=_params("parallel", "arbitrary"),
        name="na",
    )(rpb.astype(F32).reshape(-1), p, p, p, pc, pc)


def _rope_tables(l):
    t = np.arange(l)
    pos_row, pos_col = t // GRID_W, t % GRID_W
    quarter = RET_QK_DIM // 4
    inv_freq = ROPE_BASE ** (-np.arange(quarter, dtype=np.float64) / quarter)
    ang_r = pos_row[:, None] * inv_freq
    ang_c = pos_col[:, None] * inv_freq
    cos = np.concatenate([np.cos(ang_r), np.cos(ang_r), np.cos(ang_c), np.cos(ang_c)], axis=-1)
    sin = np.concatenate([-np.sin(ang_r), np.sin(ang_r), -np.sin(ang_c), np.sin(ang_c)], axis=-1)
    reps = 128 // RET_QK_DIM
    return (jnp.asarray(np.tile(cos, (1, reps)), F32), jnp.asarray(np.tile(sin, (1, reps)), F32))


def _ret_tables_kernel(lgf_ref, lgb_ref, dk_ref, dq_ref, dbi_ref, cd_ref):
    C = RET_CHUNK
    h = pl.program_id(0)
    lgf = lgf_ref[h]
    lgb = lgb_ref[h]
    lane = lax.broadcasted_iota(jnp.int32, (1, 128), 1)
    head_lanes = (lane // RET_QK_DIM) == (h % 2)
    row = lax.broadcasted_iota(jnp.int32, (C, 128), 0).astype(F32)
    diff = (lax.broadcasted_iota(jnp.int32, (C, C), 0) - lax.broadcasted_iota(jnp.int32, (C, C), 1)).astype(F32)
    dk_ref[...] = jnp.where(head_lanes, jnp.exp(lgf * (C - 1.0 - row)), jnp.exp(lgb * row))
    dq_ref[...] = jnp.where(head_lanes, jnp.exp(lgf * (row + 1.0)), jnp.exp(lgb * (C - row)))
    dbi_ref[...] = (jnp.where(diff >= 0, jnp.exp(lgf * jnp.maximum(diff, 0.0)), 0.0)
                    + jnp.where(diff <= 0, jnp.exp(lgb * jnp.maximum(-diff, 0.0)), 0.0))
    sub = lax.broadcasted_iota(jnp.int32, (8, 128), 0)
    cd_ref[...] = jnp.exp(jnp.where(sub == 0, lgf, lgb) * jnp.full((8, 128), float(C), F32))


def _ret_tables(lg_f, lg_b):
    C = RET_CHUNK
    smem = pl.BlockSpec(memory_space=pltpu.SMEM)
    per_head = lambda *shape: pl.BlockSpec((None,) + shape, lambda h: (h,) + (0,) * len(shape))
    return pl.pallas_call(
        _ret_tables_kernel,
        grid=(RET_HEADS,),
        in_specs=[smem, smem],
        out_specs=[per_head(C, 128), per_head(C, 128), per_head(C, C), per_head(8, 128)],
        out_shape=[jax.ShapeDtypeStruct((RET_HEADS, C, 128), F32), jax.ShapeDtypeStruct((RET_HEADS, C, 128), F32),
                   jax.ShapeDtypeStruct((RET_HEADS, C, C), F32), jax.ShapeDtypeStruct((RET_HEADS, 8, 128), F32)],
        compiler_params=_params("arbitrary"),
        name="ret_tables",
    )(lg_f, lg_b)


def _ret_kernel(q_ref, k_ref, v_ref, g_ref, kc_ref, vc_ref, dk_ref, dq_ref, dbi_ref, cd_ref,
                o_ref, a_ref, sb_ref, st_ref, *, n_chunks):
    C = RET_CHUNK
    lane = lax.broadcasted_iota(jnp.int32, (1, 128), 1)
    sub = lax.broadcasted_iota(jnp.int32, (128, 1), 0)

    for e in range(2):
        head_lanes = (lane // RET_QK_DIM) == e
        fwd_rows = (sub // RET_QK_DIM) == e
        vsl = slice(e * RET_V_DIM, (e + 1) * RET_V_DIM)
        dk = dk_ref[e]
        cdf = cd_ref[e, 0:1, :]
        cdb = cd_ref[e, 1:2, :]

        def local_state(k_bf16, v_bf16):
            kf32 = k_bf16.astype(F32)
            k_both = jnp.where(head_lanes, kf32, pltpu.roll(kf32, RET_QK_DIM, 1))
            return _dot((k_both * dk).T.astype(BF16), v_bf16)

        ctx_state = local_state(kc_ref[...], vc_ref[:, vsl])
        for i in range(n_chunks):
            rows = pl.ds(i * C, C)
            a_ref[i] = local_state(k_ref[rows, :], v_ref[rows, vsl])

        s_b = ctx_state
        for i in reversed(range(n_chunks)):
            sb_ref[i] = s_b
            s_b = s_b * cdb + a_ref[i]
        s_f = ctx_state
        for i in range(n_chunks):
            st_ref[e, i] = jnp.where(fwd_rows, s_f, sb_ref[i]).astype(BF16)
            s_f = s_f * cdf + a_ref[i]

    for i in range(n_chunks):
        rows = pl.ds(i * C, C)
        q = q_ref[rows, :].astype(F32)
        q_rolled = pltpu.roll(q, RET_QK_DIM, 1)
        kk = k_ref[rows, :]
        for e in range(2):
            head_lanes = (lane // RET_QK_DIM) == e
            vsl = slice(e * RET_V_DIM, (e + 1) * RET_V_DIM)
            q_own = jnp.where(head_lanes, q, 0.0).astype(BF16)
            q_both = (jnp.where(head_lanes, q, q_rolled) * dq_ref[e]).astype(BF16)
            inner = _dot_nt(q_own, kk) * dbi_ref[e]
            o = _dot(inner.astype(BF16), v_ref[rows, vsl]) + _dot(q_both, st_ref[e, i])
            mu = jnp.mean(o, axis=-1, keepdims=True)
            oc = o - mu
            var = jnp.mean(oc * oc, axis=-1, keepdims=True)
            gate = _silu(g_ref[rows, vsl].astype(F32))
            o_ref[rows, vsl] = (oc * lax.rsqrt(var + EPS) * gate).astype(BF16)


def _retention(p, pc, pc_off_k, pc_off_v, lg_f, lg_b):
    b, l, _ = p.shape
    lc = pc.shape[1]
    assert lc == RET_CHUNK and l % RET_CHUNK == 0
    n_chunks = l // RET_CHUNK
    dk, dq, dbi, cd = _ret_tables(lg_f, lg_b)
    qk_blk = lambda off: (lambda hp, bi: (bi, 0, off // 128 + hp))
    v_blk = lambda off: (lambda hp, bi: (bi, 0, off // 256 + hp))
    pair = lambda *shape: pl.BlockSpec((2,) + shape, lambda hp, bi: (hp,) + (0,) * len(shape))
    return pl.pallas_call(
        functools.partial(_ret_kernel, n_chunks=n_chunks),
        grid=(RET_HEADS // 2, b),
        in_specs=[
            pl.BlockSpec((None, l, 128), qk_blk(OFF_R_Q)),
            pl.BlockSpec((None, l, 128), qk_blk(OFF_R_K)),
            pl.BlockSpec((None, l, 256), v_blk(OFF_R_V)),
            pl.BlockSpec((None, l, 256), v_blk(OFF_R_G)),
            pl.BlockSpec((None, lc, 128), qk_blk(pc_off_k)),
            pl.BlockSpec((None, lc, 256), v_blk(pc_off_v)),
            pair(RET_CHUNK, 128), pair(RET_CHUNK, 128), pair(RET_CHUNK, RET_CHUNK), pair(8, 128),
        ],
        out_specs=pl.BlockSpec((None, l, 256), lambda hp, bi: (bi, 0, hp)),
        out_shape=jax.ShapeDtypeStruct((b, l, RET_V_WIDTH), BF16),
        scratch_shapes=[
            pltpu.VMEM((n_chunks, 128, RET_V_DIM), F32),
            pltpu.VMEM((n_chunks, 128, RET_V_DIM), F32),
            pltpu.VMEM((2, n_chunks, 128, RET_V_DIM), BF16),
        ],
        compiler_params=_params("parallel", "parallel"),
        name="ret",
    )(p, p, p, p, pc, pc, dk, dq, dbi, cd)


OUTPROJ_SUB_ROWS = 256


def _outproj_kernel(na_ref, ret_ref, wt_ref, wb_ref, x_ref, g1_ref, sh2_ref, sc2_ref, npost_ref, npre_ref,
                    xn_ref, h2_ref):
    gate_gain = g1_ref[...] * npost_ref[...]
    gain2 = npre_ref[...] * (1.0 + sc2_ref[...])
    shift2 = sh2_ref[...]
    for s in range(x_ref.shape[0] // OUTPROJ_SUB_ROWS):
        rows = pl.ds(s * OUTPROJ_SUB_ROWS, OUTPROJ_SUB_ROWS)
        mix = _dot(na_ref[rows, :], wt_ref[...]) + _dot(ret_ref[rows, :], wb_ref[...])
        ms = jnp.mean(mix * mix, axis=-1, keepdims=True)
        xn = x_ref[rows, :] + mix * lax.rsqrt(ms + EPS) * gate_gain
        xn_ref[rows, :] = xn
        ms2 = jnp.mean(xn * xn, axis=-1, keepdims=True)
        h2_ref[rows, :] = (xn * lax.rsqrt(ms2 + EPS) * gain2 + shift2).astype(BF16)


def _outproj(na, ret, w_out_bf16, x2, mod3, norm_post, norm_pre, *, rows_per_mod, tm):
    m, d = x2.shape
    kh = na.shape[1]
    mod_row = lambda i: (i * tm) // rows_per_mod
    mod_blk = lambda k: pl.BlockSpec((None, 1, d), lambda i: (mod_row(i), 0, k))
    return pl.pallas_call(
        _outproj_kernel,
        grid=(m // tm,),
        in_specs=[
            pl.BlockSpec((tm, kh), lambda i: (i, 0)),
            pl.BlockSpec((tm, kh), lambda i: (i, 0)),
            pl.BlockSpec((kh, d), lambda i: (0, 0)),
            pl.BlockSpec((kh, d), lambda i: (1, 0)),
            pl.BlockSpec((tm, d), lambda i: (i, 0)),
            mod_blk(2), mod_blk(3), mod_blk(4),
            pl.BlockSpec((1, d), lambda i: (0, 0)),
            pl.BlockSpec((1, d), lambda i: (0, 0)),
        ],
        out_specs=[pl.BlockSpec((tm, d), lambda i: (i, 0)), pl.BlockSpec((tm, d), lambda i: (i, 0))],
        out_shape=[jax.ShapeDtypeStruct((m, d), F32), jax.ShapeDtypeStruct((m, d), BF16)],
        compiler_params=_params("parallel"),
        name="outproj",
    )(na, ret, w_out_bf16, w_out_bf16, x2, mod3, mod3, mod3, norm_post, norm_pre)


FFN_SUB_ROWS = 256
FFN_RESIDUAL_CHUNKS = 4


def _ffn_kernel(h_ref, wg_ref, wu_ref, wd_ref, xn_hbm, g2_ref, npost_ref, o_ref, xn_buf, xn_sem):
    i = pl.program_id(0)
    f = pl.program_id(1)
    tm = h_ref.shape[0]
    chunk = tm // FFN_RESIDUAL_CHUNKS

    def residual_copy(k):
        return pltpu.make_async_copy(xn_hbm.at[pl.ds(i * tm + k * chunk, chunk), :],
                                     xn_buf.at[pl.ds(k * chunk, chunk), :], xn_sem.at[k])

    for k in range(FFN_RESIDUAL_CHUNKS):
        @pl.when(f == k + 1)
        def _():
            residual_copy(k).start()

    def gated():
        h = h_ref[...]
        return (_silu(_dot(h, wg_ref[...])) * _dot(h, wu_ref[...])).astype(BF16)

    last = pl.num_programs(1) - 1

    @pl.when(f == 0)
    def _():
        o_ref[...] = _dot(gated(), wd_ref[...])

    @pl.when((f > 0) & (f < last))
    def _():
        o_ref[...] += _dot(gated(), wd_ref[...])

    @pl.when(f == last)
    def _():
        t = gated()
        for k in range(FFN_RESIDUAL_CHUNKS):
            residual_copy(k).wait()
        gate_gain = g2_ref[...] * npost_ref[...]
        for s in range(tm // FFN_SUB_ROWS):
            rows = slice(s * FFN_SUB_ROWS, (s + 1) * FFN_SUB_ROWS)
            y = o_ref[rows, :] + _dot(t[rows, :], wd_ref[...])
            ms = jnp.mean(y * y, axis=-1, keepdims=True)
            o_ref[rows, :] = xn_buf[rows, :] + y * lax.rsqrt(ms + EPS) * gate_gain


def _ffn(h2, wg, wu, wd, xn, mod3, norm_post, *, rows_per_mod, tm, tf):
    m, d = h2.shape
    dff = wd.shape[0]
    mod_row = lambda i: (i * tm) // rows_per_mod
    assert dff // tf >= FFN_RESIDUAL_CHUNKS + 2 and tm % FFN_RESIDUAL_CHUNKS == 0
    return pl.pallas_call(
        _ffn_kernel,
        grid=(m // tm, dff // tf),
        in_specs=[
            pl.BlockSpec((tm, d), lambda i, f: (i, 0)),
            pl.BlockSpec((d, tf), lambda i, f: (0, f)),
            pl.BlockSpec((d, tf), lambda i, f: (0, f)),
            pl.BlockSpec((tf, d), lambda i, f: (f, 0)),
            pl.BlockSpec(memory_space=pl.ANY),
            pl.BlockSpec((None, 1, d), lambda i, f: (mod_row(i), 0, 5)),
            pl.BlockSpec((1, d), lambda i, f: (0, 0)),
        ],
        out_specs=pl.BlockSpec((tm, d), lambda i, f: (i, 0)),
        out_shape=jax.ShapeDtypeStruct((m, d), F32),
        scratch_shapes=[pltpu.VMEM((tm, d), F32), pltpu.SemaphoreType.DMA((FFN_RESIDUAL_CHUNKS,))],
        compiler_params=_params("parallel", "arbitrary"),
        name="ffn",
    )(h2, wg, wu, wd, xn, mod3, norm_post)


def kernel(x, c, ctx, c_ctx, ada_w, ada_b, norm_pre_mix, norm_post_mix, norm_pre_ffn, norm_post_ffn, w_in,
           na_rpb, ret_log_gamma_fwd, ret_log_gamma_bwd, w_out, w_gate, w_up, w_down):
    b, l, d = x.shape
    lc = ctx.shape[1]
    depth = ada_w.shape[0]
    assert depth == 1, "the context-stream update between layers is not implemented"
    li = 0

    mod_rows = -(-(b + 1) // 8) * 8
    cc = jnp.concatenate([c, c_ctx[None, :], jnp.zeros((mod_rows - b - 1, d), F32)], axis=0)
    mod3 = _adaln(cc, ada_w[li], ada_b[li]).reshape(mod_rows, 1, 6 * d)

    w_in_b = w_in[li].astype(BF16)
    x2 = x.reshape(b * l, d)
    ctx2 = ctx.reshape(b * lc, d)
    gain_pre = norm_pre_mix[li].reshape(1, d)

    p = _inproj(x2, gain_pre, mod3, w_in_b, rows_per_mod=l, mod_row0=0,
                col_tiles=tuple(range(IN_WIDTH // 1024)), tm=1024, tn=1024,
                first_tile_scale=NA_HEAD_DIM ** -0.5 * LOG2_E, rope_tile=OFF_R_Q // 1024, rope_tables=_rope_tables(l))
    tn = 512
    ctx_tiles = tuple(range(OFF_NA_K // tn, OFF_R_Q // tn)) + tuple(range(OFF_R_K // tn, OFF_R_G // tn))
    pc = _inproj(ctx2, gain_pre, mod3, w_in_b, rows_per_mod=b * lc, mod_row0=b,
                 col_tiles=ctx_tiles, tm=1024, tn=tn)
    p = p.reshape(b, l, IN_WIDTH)
    pc = pc.reshape(b, lc, len(ctx_tiles) * tn)
    pc_na_k, pc_na_v = 0, NA_WIDTH
    pc_r_k, pc_r_v = 2 * NA_WIDTH, 2 * NA_WIDTH + RET_QK_WIDTH

    o_na = _neighbourhood_attention(p, pc, pc_na_k, pc_na_v, na_rpb[li])
    o_ret = _retention(p, pc, pc_r_k, pc_r_v, ret_log_gamma_fwd[li].astype(F32), ret_log_gamma_bwd[li].astype(F32))

    x_new, h2 = _outproj(o_na.reshape(b * l, NA_WIDTH), o_ret.reshape(b * l, RET_V_WIDTH), w_out[li].astype(BF16),
                         x2, mod3, norm_post_mix[li].reshape(1, d), norm_pre_ffn[li].reshape(1, d),
                         rows_per_mod=l, tm=512)
    out = _ffn(h2, w_gate[li].astype(BF16), w_up[li].astype(BF16), w_down[li].astype(BF16), x_new, mod3,
               norm_post_ffn[li].reshape(1, d), rows_per_mod=l, tm=1024, tf=512)
    return out.reshape(b, l, d)
```

```python
import functools

import numpy as np
import jax
import jax.numpy as jnp
from jax import lax
from jax.experimental import pallas as pl
from jax.experimental.pallas import tpu as pltpu

F32 = jnp.float32
BF16 = jnp.bfloat16

D_MODEL = 2048
GRID_W = 64
NA_HEADS = 8
NA_HEAD_DIM = 128
NA_WIN_H = 8
NA_WIN_W = 16
RET_HEADS = 8
RET_QK_DIM = 64
RET_V_DIM = 128
NA_WIDTH = NA_HEADS * NA_HEAD_DIM
RET_QK_WIDTH = RET_HEADS * RET_QK_DIM
RET_V_WIDTH = RET_HEADS * RET_V_DIM
IN_WIDTH = 3 * NA_WIDTH + 2 * RET_QK_WIDTH + 2 * RET_V_WIDTH
ROPE_BASE = 10000.0
EPS = 1e-6
MASK_VALUE = -1e30
LOG2_E = float(np.log2(np.e))

OFF_NA_Q = 0
OFF_NA_K = NA_WIDTH
OFF_NA_V = 2 * NA_WIDTH
OFF_R_Q = 3 * NA_WIDTH
OFF_R_K = OFF_R_Q + RET_QK_WIDTH
OFF_R_V = OFF_R_K + RET_QK_WIDTH
OFF_R_G = OFF_R_V + RET_V_WIDTH

VMEM_LIMIT_BYTES = 56 * 1024 * 1024

RET_CHUNK = 256


def _params(*semantics):
    return pltpu.CompilerParams(dimension_semantics=semantics, vmem_limit_bytes=VMEM_LIMIT_BYTES)


def _dot(a, b):
    return jnp.dot(a, b, preferred_element_type=F32)


def _dot_nt(a, b):
    return lax.dot_general(a, b, (((1,), (1,)), ((), ())), preferred_element_type=F32)


def _silu(x):
    return x * jax.nn.sigmoid(x)


def _adaln_kernel(c_ref, w_ref, b_ref, o_ref):
    s = _silu(c_ref[...]).astype(BF16)
    o_ref[...] = _dot(s, w_ref[...].astype(BF16)) + b_ref[...]


def _adaln(cc, ada_w, ada_b, tn=1024):
    rows, d = cc.shape
    n = ada_w.shape[1]
    return pl.pallas_call(
        _adaln_kernel,
        grid=(n // tn,),
        in_specs=[
            pl.BlockSpec((rows, d), lambda j: (0, 0)),
            pl.BlockSpec((d, tn), lambda j: (0, j)),
            pl.BlockSpec((1, tn), lambda j: (0, j)),
        ],
        out_specs=pl.BlockSpec((rows, tn), lambda j: (0, j)),
        out_shape=jax.ShapeDtypeStruct((rows, n), F32),
        compiler_params=_params("arbitrary"),
        name="adaln",
    )(cc, ada_w, ada_b.reshape(1, n))


INPROJ_SUB_ROWS = 256


def _rope_rotate(x, cos, sin_signed):
    lane = lax.broadcasted_iota(jnp.int32, (1, 128), 1)
    first_half = (lane % (RET_QK_DIM // 2)) < (RET_QK_DIM // 4)
    partner = jnp.where(first_half, pltpu.roll(x, 128 - RET_QK_DIM // 4, 1), pltpu.roll(x, RET_QK_DIM // 4, 1))
    return x * cos + partner * sin_signed


def _inproj_kernel(x_hbm, g_ref, sh_ref, sc_ref, w_ref, *refs, first_tile_scale, rope_tile):
    if rope_tile is None:
        o_ref, h_ref, x_buf, x_sem = refs
    else:
        cos_ref, sin_ref, o_ref, h_ref, x_buf, x_sem = refs
    i = pl.program_id(0)
    j = pl.program_id(1)
    tm = h_ref.shape[0]
    n_slices = tm // INPROJ_SUB_ROWS
    slot = lax.rem(i, 2)

    def x_copy(tile, dst_slot):
        return pltpu.make_async_copy(x_hbm.at[pl.ds(tile * tm, tm), :], x_buf.at[dst_slot], x_sem.at[dst_slot])

    @pl.when((i == 0) & (j == 0))
    def _():
        x_copy(0, 0).start()

    @pl.when((j == 1) & (i + 1 < pl.num_programs(0)))
    def _():
        x_copy(i + 1, 1 - slot).start()

    @pl.when(j == 0)
    def _():
        x_copy(i, slot).wait()
        gain = g_ref[...] * (1.0 + sc_ref[...])
        shift = sh_ref[...]
        for s in range(n_slices):
            rows = pl.ds(s * INPROJ_SUB_ROWS, INPROJ_SUB_ROWS)
            x = x_buf[slot, rows, :]
            ms = jnp.mean(x * x, axis=-1, keepdims=True)
            h = (x * lax.rsqrt(ms + EPS) * gain + shift).astype(BF16)
            h_ref[rows, :] = h
            res = _dot(h, w_ref[...])
            if first_tile_scale != 1.0:
                res = res * first_tile_scale
            o_ref[rows, :] = res.astype(BF16)

    plain = (j > 0) if rope_tile is None else ((j > 0) & (j != rope_tile))

    @pl.when(plain)
    def _():
        o_ref[...] = _dot(h_ref[...], w_ref[...]).astype(BF16)

    if rope_tile is not None:
        @pl.when(j == rope_tile)
        def _():
            q_scale = RET_QK_DIM ** -0.5
            for s in range(n_slices):
                rows = pl.ds(s * INPROJ_SUB_ROWS, INPROJ_SUB_ROWS)
                res = _dot(h_ref[rows, :], w_ref[...])
                cos, sin = cos_ref[rows, :], sin_ref[rows, :]
                cos_q, sin_q = cos * q_scale, sin * q_scale
                for cg in range(o_ref.shape[1] // 128):
                    lanes = slice(cg * 128, (cg + 1) * 128)
                    is_q = cg < RET_QK_WIDTH // 128
                    y = _rope_rotate(res[:, lanes], cos_q if is_q else cos, sin_q if is_q else sin)
                    o_ref[rows, lanes] = y.astype(BF16)


def _inproj(x2, gain, mod3, w_bf16, *, rows_per_mod, mod_row0, col_tiles, tm, tn,
            first_tile_scale=1.0, rope_tile=None, rope_tables=None):
    m, d = x2.shape
    n_out = len(col_tiles) * tn
    first = col_tiles[0]
    gap_at = next((i for i in range(1, len(col_tiles)) if col_tiles[i] != col_tiles[i - 1] + 1), None)
    if gap_at is None:
        w_map = lambda i, j: (0, j + first)
    else:
        gap = col_tiles[gap_at] - col_tiles[gap_at - 1] - 1
        assert all(col_tiles[i] == first + i + (gap if i >= gap_at else 0) for i in range(len(col_tiles)))
        w_map = lambda i, j: (0, j + first + jnp.where(j >= gap_at, gap, 0))
    mod_row = lambda i: mod_row0 + (i * tm) // rows_per_mod
    assert len(col_tiles) >= 2
    in_specs = [
        pl.BlockSpec(memory_space=pl.ANY),
        pl.BlockSpec((1, d), lambda i, j: (0, 0)),
        pl.BlockSpec((None, 1, d), lambda i, j: (mod_row(i), 0, 0)),
        pl.BlockSpec((None, 1, d), lambda i, j: (mod_row(i), 0, 1)),
        pl.BlockSpec((d, tn), w_map),
    ]
    operands = [x2, gain, mod3, mod3, w_bf16]
    if rope_tile is not None:
        assert tn == 2 * RET_QK_WIDTH and col_tiles[rope_tile] * tn == OFF_R_Q and rows_per_mod % tm == 0
        pos_blk = pl.BlockSpec((tm, 128), lambda i, j: (i % (rows_per_mod // tm), 0))
        in_specs += [pos_blk, pos_blk]
        operands += list(rope_tables)
    return pl.pallas_call(
        functools.partial(_inproj_kernel, first_tile_scale=first_tile_scale, rope_tile=rope_tile),
        grid=(m // tm, len(col_tiles)),
        in_specs=in_specs,
        out_specs=pl.BlockSpec((tm, tn), lambda i, j: (i, j)),
        out_shape=jax.ShapeDtypeStruct((m, n_out), BF16),
        scratch_shapes=[pltpu.VMEM((tm, d), BF16), pltpu.VMEM((2, tm, d), F32), pltpu.SemaphoreType.DMA((2,))],
        compiler_params=_params("arbitrary", "arbitrary"),
        name="inproj",
    )(*operands)


NA_GROUP_ROWS = 4
NA_BAND_ROWS = NA_GROUP_ROWS + NA_WIN_H
NA_Q = NA_GROUP_ROWS * GRID_W
NA_BAND = NA_BAND_ROWS * GRID_W


def _na_band_start(g, rows):
    lo = g * NA_GROUP_ROWS - NA_WIN_H // 2
    if isinstance(g, (int, np.integer)):
        return min(max(lo, 0), rows - NA_BAND_ROWS)
    return jnp.minimum(jnp.maximum(lo, 0), rows - NA_BAND_ROWS)


def _na_row_patterns(rows):
    groups = rows // NA_GROUP_ROWS
    assert rows % NA_GROUP_ROWS == 0 and rows >= NA_BAND_ROWS and groups >= 3

    def row_pattern(g):
        r = g * NA_GROUP_ROWS + np.arange(NA_GROUP_ROWS)[:, None]
        krow = _na_band_start(g, rows) + np.arange(NA_BAND_ROWS)[None, :]
        rs = np.clip(r - NA_WIN_H // 2, 0, rows - NA_WIN_H)
        ok = (krow >= rs) & (krow < rs + NA_WIN_H)
        return ok, np.clip(krow - r + NA_WIN_H - 1, 0, 2 * NA_WIN_H - 2)

    pats = [row_pattern(0), row_pattern(1), row_pattern(groups - 1)]
    for g in range(1, groups - 1):
        ok, dr = row_pattern(g)
        assert (ok == pats[1][0]).all() and (dr[ok] == pats[1][1][ok]).all()
    return pats


NA_SCORE_AHEAD = 8
NA_HEADS_PER_STEP = 2


def _na_kernel(rpb_ref, q_ref, k_ref, v_ref, kc_ref, vc_ref, o_ref, s_ref, bias_ref, va_ref, vca_ref, *, rows):
    groups = rows // NA_GROUP_ROWS
    hd = NA_HEAD_DIM
    head_lanes = [slice(hh * hd, (hh + 1) * hd) for hh in range(NA_HEADS_PER_STEP)]

    @pl.when(pl.program_id(1) == 0)
    def _():
        n_dr, n_dc = 2 * NA_WIN_H - 1, 2 * NA_WIN_W - 1
        c = lax.broadcasted_iota(jnp.int32, (GRID_W, GRID_W), 0)
        kc = lax.broadcasted_iota(jnp.int32, (GRID_W, GRID_W), 1)
        col_start = jnp.minimum(jnp.maximum(c - NA_WIN_W // 2, 0), GRID_W - NA_WIN_W)
        col_ok = (kc >= col_start) & (kc < col_start + NA_WIN_W)
        dc = kc - c + (NA_WIN_W - 1)
        masked = jnp.full((GRID_W, GRID_W), MASK_VALUE, F32)
        pats = _na_row_patterns(rows)
        used = sorted({int(dr[a, j]) for ok, dr in pats for a, j in zip(*np.nonzero(ok))})
        for hh in range(NA_HEADS_PER_STEP):
            base = (pl.program_id(0) * NA_HEADS_PER_STEP + hh) * (n_dr * n_dc)
            tiles = {}
            for d in used:
                t = jnp.zeros((GRID_W, GRID_W), F32)
                for x in range(n_dc):
                    t = jnp.where(dc == x, rpb_ref[base + d * n_dc + x], t)
                tiles[d] = jnp.where(col_ok, t * LOG2_E, MASK_VALUE)
            for pi, (ok, dr) in enumerate(pats):
                for a in range(NA_GROUP_ROWS):
                    for j in range(NA_BAND_ROWS):
                        tile = tiles[int(dr[a, j])] if ok[a, j] else masked
                        bias_ref[hh, pi, a * GRID_W:(a + 1) * GRID_W, j * GRID_W:(j + 1) * GRID_W] = tile

    for hh, lanes in enumerate(head_lanes):
        va_ref[hh, :, :hd] = v_ref[:, lanes]
        va_ref[hh, :, hd:] = jnp.ones((va_ref.shape[1], hd), BF16)
        vca_ref[hh, :, :hd] = vc_ref[:, lanes]
        vca_ref[hh, :, hd:] = jnp.ones((vca_ref.shape[1], hd), BF16)

    def scores(hh, g, s_ref):
        lanes = head_lanes[hh]
        q0 = g * NA_Q
        k0 = _na_band_start(g, rows) * GRID_W
        pattern = 0 if g == 0 else (2 if g == groups - 1 else 1)
        q = q_ref[pl.ds(q0, NA_Q), lanes]
        s_ref[:, :NA_BAND] = _dot_nt(q, k_ref[pl.ds(k0, NA_BAND), lanes]) + bias_ref[hh, pattern]
        s_ref[:, NA_BAND:] = _dot_nt(q, kc_ref[:, lanes])

    def attend(hh, g, s_ref):
        q0 = g * NA_Q
        k0 = _na_band_start(g, rows) * GRID_W
        s = s_ref[...]
        pb = jnp.exp2(s - jnp.max(s, axis=-1, keepdims=True)).astype(BF16)
        o = _dot(pb[:, :NA_BAND], va_ref[hh, pl.ds(k0, NA_BAND), :]) + _dot(pb[:, NA_BAND:], vca_ref[hh])
        o_ref[pl.ds(q0, NA_Q), head_lanes[hh]] = (o[:, :hd] / o[:, hd:hd + 1]).astype(BF16)

    items = [(hh, g) for hh in range(NA_HEADS_PER_STEP) for g in range(groups)]
    for n in range(min(NA_SCORE_AHEAD, len(items))):
        scores(*items[n], s_ref.at[n])
    for n in range(len(items)):
        if n + NA_SCORE_AHEAD < len(items):
            scores(*items[n + NA_SCORE_AHEAD], s_ref.at[n + NA_SCORE_AHEAD])
        attend(*items[n], s_ref.at[n])


def _neighbourhood_attention(p, pc, pc_off_k, pc_off_v, rpb):
    b, l, _ = p.shape
    lc = pc.shape[1]
    rows = l // GRID_W
    hd = NA_HEAD_DIM
    assert rpb.shape == (NA_HEADS, 2 * NA_WIN_H - 1, 2 * NA_WIN_W - 1)
    hps = NA_HEADS_PER_STEP
    wide = hps * hd
    blk = lambda off: (lambda h, bi: (bi, 0, off // wide + h))
    return pl.pallas_call(
        functools.partial(_na_kernel, rows=rows),
        grid=(NA_HEADS // hps, b),
        in_specs=[
            pl.BlockSpec(memory_space=pltpu.SMEM),
            pl.BlockSpec((None, l, wide), blk(OFF_NA_Q)),
            pl.BlockSpec((None, l, wide), blk(OFF_NA_K)),
            pl.BlockSpec((None, l, wide), blk(OFF_NA_V)),
            pl.BlockSpec((None, lc, wide), blk(pc_off_k)),
            pl.BlockSpec((None, lc, wide), blk(pc_off_v)),
        ],
        out_specs=pl.BlockSpec((None, l, wide), lambda h, bi: (bi, 0, h)),
        out_shape=jax.ShapeDtypeStruct((b, l, NA_WIDTH), BF16),
        scratch_shapes=[pltpu.VMEM((hps * (rows // NA_GROUP_ROWS), NA_Q, NA_BAND + lc), F32),
                        pltpu.VMEM((hps, 3, NA_Q, NA_BAND), F32),
                        pltpu.VMEM((hps, l, 2 * hd), BF16),
                        pltpu.VMEM((hps, lc, 2 * hd), BF16)],
        compiler_params=_params("parallel", "arbitrary"),
        name="na",
    )(rpb.astype(F32).reshape(-1), p, p, p, pc, pc)


def _rope_tables(l):
    t = np.arange(l)
    pos_row, pos_col = t // GRID_W, t % GRID_W
    quarter = RET_QK_DIM // 4
    inv_freq = ROPE_BASE ** (-np.arange(quarter, dtype=np.float64) / quarter)
    ang_r = pos_row[:, None] * inv_freq
    ang_c = pos_col[:, None] * inv_freq
    cos = np.concatenate([np.cos(ang_r), np.cos(ang_r), np.cos(ang_c), np.cos(ang_c)], axis=-1)
    sin = np.concatenate([-np.sin(ang_r), np.sin(ang_r), -np.sin(ang_c), np.sin(ang_c)], axis=-1)
    reps = 128 // RET_QK_DIM
    return (jnp.asarray(np.tile(cos, (1, reps)), F32), jnp.asarray(np.tile(sin, (1, reps)), F32))


def _ret_tables_kernel(lgf_ref, lgb_ref, dk_ref, dq_ref, dbi_ref, cd_ref):
    C = RET_CHUNK
    h = pl.program_id(0)
    lgf = lgf_ref[h]
    lgb = lgb_ref[h]
    lane = lax.broadcasted_iota(jnp.int32, (1, 128), 1)
    head_lanes = (lane // RET_QK_DIM) == (h % 2)
    row = lax.broadcasted_iota(jnp.int32, (C, 128), 0).astype(F32)
    diff = (lax.broadcasted_iota(jnp.int32, (C, C), 0) - lax.broadcasted_iota(jnp.int32, (C, C), 1)).astype(F32)
    dk_ref[...] = jnp.where(head_lanes, jnp.exp(lgf * (C - 1.0 - row)), jnp.exp(lgb * row))
    dq_ref[...] = jnp.where(head_lanes, jnp.exp(lgf * (row + 1.0)), jnp.exp(lgb * (C - row)))
    dbi_ref[...] = (jnp.where(diff >= 0, jnp.exp(lgf * jnp.maximum(diff, 0.0)), 0.0)
                    + jnp.where(diff <= 0, jnp.exp(lgb * jnp.maximum(-diff, 0.0)), 0.0))
    sub = lax.broadcasted_iota(jnp.int32, (8, 128), 0)
    cd_ref[...] = jnp.exp(jnp.where(sub == 0, lgf, lgb) * jnp.full((8, 128), float(C), F32))


def _ret_tables(lg_f, lg_b):
    C = RET_CHUNK
    smem = pl.BlockSpec(memory_space=pltpu.SMEM)
    per_head = lambda *shape: pl.BlockSpec((None,) + shape, lambda h: (h,) + (0,) * len(shape))
    return pl.pallas_call(
        _ret_tables_kernel,
        grid=(RET_HEADS,),
        in_specs=[smem, smem],
        out_specs=[per_head(C, 128), per_head(C, 128), per_head(C, C), per_head(8, 128)],
        out_shape=[jax.ShapeDtypeStruct((RET_HEADS, C, 128), F32), jax.ShapeDtypeStruct((RET_HEADS, C, 128), F32),
                   jax.ShapeDtypeStruct((RET_HEADS, C, C), F32), jax.ShapeDtypeStruct((RET_HEADS, 8, 128), F32)],
        compiler_params=_params("arbitrary"),
        name="ret_tables",
    )(lg_f, lg_b)


def _ret_kernel(q_ref, k_ref, v_ref, g_ref, kc_ref, vc_ref, dk_ref, dq_ref, dbi_ref, cd_ref,
                o_ref, a_ref, sb_ref, st_ref, *, n_chunks):
    C = RET_CHUNK
    lane = lax.broadcasted_iota(jnp.int32, (1, 128), 1)
    sub = lax.broadcasted_iota(jnp.int32, (128, 1), 0)

    for e in range(2):
        head_lanes = (lane // RET_QK_DIM) == e
        fwd_rows = (sub // RET_QK_DIM) == e
        vsl = slice(e * RET_V_DIM, (e + 1) * RET_V_DIM)
        dk = dk_ref[e]
        cdf = cd_ref[e, 0:1, :]
        cdb = cd_ref[e, 1:2, :]

        def local_state(k_bf16, v_bf16):
            kf32 = k_bf16.astype(F32)
            k_both = jnp.where(head_lanes, kf32, pltpu.roll(kf32, RET_QK_DIM, 1))
            return _dot((k_both * dk).T.astype(BF16), v_bf16)

        ctx_state = local_state(kc_ref[...], vc_ref[:, vsl])
        for i in range(n_chunks):
            rows = pl.ds(i * C, C)
            a_ref[i] = local_state(k_ref[rows, :], v_ref[rows, vsl])

        s_b = ctx_state
        for i in reversed(range(n_chunks)):
            sb_ref[i] = s_b
            s_b = s_b * cdb + a_ref[i]
        s_f = ctx_state
        for i in range(n_chunks):
            st_ref[e, i] = jnp.where(fwd_rows, s_f, sb_ref[i]).astype(BF16)
            s_f = s_f * cdf + a_ref[i]

    for i in range(n_chunks):
        rows = pl.ds(i * C, C)
        q = q_ref[rows, :].astype(F32)
        q_rolled = pltpu.roll(q, RET_QK_DIM, 1)
        kk = k_ref[rows, :]
        for e in range(2):
            head_lanes = (lane // RET_QK_DIM) == e
            vsl = slice(e * RET_V_DIM, (e + 1) * RET_V_DIM)
            q_own = jnp.where(head_lanes, q, 0.0).astype(BF16)
            q_both = (jnp.where(head_lanes, q, q_rolled) * dq_ref[e]).astype(BF16)
            inner = _dot_nt(q_own, kk) * dbi_ref[e]
            o = _dot(inner.astype(BF16), v_ref[rows, vsl]) + _dot(q_both, st_ref[e, i])
            mu = jnp.mean(o, axis=-1, keepdims=True)
            oc = o - mu
            var = jnp.mean(oc * oc, axis=-1, keepdims=True)
            gate = _silu(g_ref[rows, vsl].astype(F32))
            o_ref[rows, vsl] = (oc * lax.rsqrt(var + EPS) * gate).astype(BF16)


def _retention(p, pc, pc_off_k, pc_off_v, lg_f, lg_b):
    b, l, _ = p.shape
    lc = pc.shape[1]
    assert lc == RET_CHUNK and l % RET_CHUNK == 0
    n_chunks = l // RET_CHUNK
    dk, dq, dbi, cd = _ret_tables(lg_f, lg_b)
    qk_blk = lambda off: (lambda hp, bi: (bi, 0, off // 128 + hp))
    v_blk = lambda off: (lambda hp, bi: (bi, 0, off // 256 + hp))
    pair = lambda *shape: pl.BlockSpec((2,) + shape, lambda hp, bi: (hp,) + (0,) * len(shape))
    return pl.pallas_call(
        functools.partial(_ret_kernel, n_chunks=n_chunks),
        grid=(RET_HEADS // 2, b),
        in_specs=[
            pl.BlockSpec((None, l, 128), qk_blk(OFF_R_Q)),
            pl.BlockSpec((None, l, 128), qk_blk(OFF_R_K)),
            pl.BlockSpec((None, l, 256), v_blk(OFF_R_V)),
            pl.BlockSpec((None, l, 256), v_blk(OFF_R_G)),
            pl.BlockSpec((None, lc, 128), qk_blk(pc_off_k)),
            pl.BlockSpec((None, lc, 256), v_blk(pc_off_v)),
            pair(RET_CHUNK, 128), pair(RET_CHUNK, 128), pair(RET_CHUNK, RET_CHUNK), pair(8, 128),
        ],
        out_specs=pl.BlockSpec((None, l, 256), lambda hp, bi: (bi, 0, hp)),
        out_shape=jax.ShapeDtypeStruct((b, l, RET_V_WIDTH), BF16),
        scratch_shapes=[
            pltpu.VMEM((n_chunks, 128, RET_V_DIM), F32),
            pltpu.VMEM((n_chunks, 128, RET_V_DIM), F32),
            pltpu.VMEM((2, n_chunks, 128, RET_V_DIM), BF16),
        ],
        compiler_params=_params("parallel", "parallel"),
        name="ret",
    )(p, p, p, p, pc, pc, dk, dq, dbi, cd)


OUTPROJ_SUB_ROWS = 256


def _outproj_kernel(na_ref, ret_ref, wt_ref, wb_ref, x_ref, g1_ref, sh2_ref, sc2_ref, npost_ref, npre_ref,
                    xn_ref, h2_ref):
    gate_gain = g1_ref[...] * npost_ref[...]
    gain2 = npre_ref[...] * (1.0 + sc2_ref[...])
    shift2 = sh2_ref[...]
    for s in range(x_ref.shape[0] // OUTPROJ_SUB_ROWS):
        rows = pl.ds(s * OUTPROJ_SUB_ROWS, OUTPROJ_SUB_ROWS)
        mix = _dot(na_ref[rows, :], wt_ref[...]) + _dot(ret_ref[rows, :], wb_ref[...])
        ms = jnp.mean(mix * mix, axis=-1, keepdims=True)
        xn = x_ref[rows, :] + mix * lax.rsqrt(ms + EPS) * gate_gain
        xn_ref[rows, :] = xn
        ms2 = jnp.mean(xn * xn, axis=-1, keepdims=True)
        h2_ref[rows, :] = (xn * lax.rsqrt(ms2 + EPS) * gain2 + shift2).astype(BF16)


def _outproj(na, ret, w_out_bf16, x2, mod3, norm_post, norm_pre, *, rows_per_mod, tm):
    m, d = x2.shape
    kh = na.shape[1]
    mod_row = lambda i: (i * tm) // rows_per_mod
    mod_blk = lambda k: pl.BlockSpec((None, 1, d), lambda i: (mod_row(i), 0, k))
    return pl.pallas_call(
        _outproj_kernel,
        grid=(m // tm,),
        in_specs=[
            pl.BlockSpec((tm, kh), lambda i: (i, 0)),
            pl.BlockSpec((tm, kh), lambda i: (i, 0)),
            pl.BlockSpec((kh, d), lambda i: (0, 0)),
            pl.BlockSpec((kh, d), lambda i: (1, 0)),
            pl.BlockSpec((tm, d), lambda i: (i, 0)),
            mod_blk(2), mod_blk(3), mod_blk(4),
            pl.BlockSpec((1, d), lambda i: (0, 0)),
            pl.BlockSpec((1, d), lambda i: (0, 0)),
        ],
        out_specs=[pl.BlockSpec((tm, d), lambda i: (i, 0)), pl.BlockSpec((tm, d), lambda i: (i, 0))],
        out_shape=[jax.ShapeDtypeStruct((m, d), F32), jax.ShapeDtypeStruct((m, d), BF16)],
        compiler_params=_params("parallel"),
        name="outproj",
    )(na, ret, w_out_bf16, w_out_bf16, x2, mod3, mod3, mod3, norm_post, norm_pre)


FFN_SUB_ROWS = 256
FFN_RESIDUAL_CHUNKS = 4


def _ffn_kernel(h_ref, wg_ref, wu_ref, wd_ref, xn_hbm, g2_ref, npost_ref, o_ref, xn_buf, xn_sem):
    i = pl.program_id(0)
    f = pl.program_id(1)
    tm = h_ref.shape[0]
    chunk = tm // FFN_RESIDUAL_CHUNKS

    def residual_copy(k):
        return pltpu.make_async_copy(xn_hbm.at[pl.ds(i * tm + k * chunk, chunk), :],
                                     xn_buf.at[pl.ds(k * chunk, chunk), :], xn_sem.at[k])

    for k in range(FFN_RESIDUAL_CHUNKS):
        @pl.when(f == k + 1)
        def _():
            residual_copy(k).start()

    def gated():
        h = h_ref[...]
        return (_silu(_dot(h, wg_ref[...])) * _dot(h, wu_ref[...])).astype(BF16)

    last = pl.num_programs(1) - 1

    @pl.when(f == 0)
    def _():
        o_ref[...] = _dot(gated(), wd_ref[...])

    @pl.when((f > 0) & (f < last))
    def _():
        o_ref[...] += _dot(gated(), wd_ref[...])

    @pl.when(f == last)
    def _():
        t = gated()
        for k in range(FFN_RESIDUAL_CHUNKS):
            residual_copy(k).wait()
        gate_gain = g2_ref[...] * npost_ref[...]
        for s in range(tm // FFN_SUB_ROWS):
            rows = slice(s * FFN_SUB_ROWS, (s + 1) * FFN_SUB_ROWS)
            y = o_ref[rows, :] + _dot(t[rows, :], wd_ref[...])
            ms = jnp.mean(y * y, axis=-1, keepdims=True)
            o_ref[rows, :] = xn_buf[rows, :] + y * lax.rsqrt(ms + EPS) * gate_gain


def _ffn(h2, wg, wu, wd, xn, mod3, norm_post, *, rows_per_mod, tm, tf):
    m, d = h2.shape
    dff = wd.shape[0]
    mod_row = lambda i: (i * tm) // rows_per_mod
    assert dff // tf >= FFN_RESIDUAL_CHUNKS + 2 and tm % FFN_RESIDUAL_CHUNKS == 0
    return pl.pallas_call(
        _ffn_kernel,
        grid=(m // tm, dff // tf),
        in_specs=[
            pl.BlockSpec((tm, d), lambda i, f: (i, 0)),
            pl.BlockSpec((d, tf), lambda i, f: (0, f)),
            pl.BlockSpec((d, tf), lambda i, f: (0, f)),
            pl.BlockSpec((tf, d), lambda i, f: (f, 0)),
            pl.BlockSpec(memory_space=pl.ANY),
            pl.BlockSpec((None, 1, d), lambda i, f: (mod_row(i), 0, 5)),
            pl.BlockSpec((1, d), lambda i, f: (0, 0)),
        ],
        out_specs=pl.BlockSpec((tm, d), lambda i, f: (i, 0)),
        out_shape=jax.ShapeDtypeStruct((m, d), F32),
        scratch_shapes=[pltpu.VMEM((tm, d), F32), pltpu.SemaphoreType.DMA((FFN_RESIDUAL_CHUNKS,))],
        compiler_params=_params("parallel", "arbitrary"),
        name="ffn",
    )(h2, wg, wu, wd, xn, mod3, norm_post)


def kernel(x, c, ctx, c_ctx, ada_w, ada_b, norm_pre_mix, norm_post_mix, norm_pre_ffn, norm_post_ffn, w_in,
           na_rpb, ret_log_gamma_fwd, ret_log_gamma_bwd, w_out, w_gate, w_up, w_down):
    b, l, d = x.shape
    lc = ctx.shape[1]
    depth = ada_w.shape[0]
    assert depth == 1, "the context-stream update between layers is not implemented"
    li = 0

    mod_rows = -(-(b + 1) // 8) * 8
    cc = jnp.concatenate([c, c_ctx[None, :], jnp.zeros((mod_rows - b - 1, d), F32)], axis=0)
    mod3 = _adaln(cc, ada_w[li], ada_b[li]).reshape(mod_rows, 1, 6 * d)

    w_in_b = w_in[li].astype(BF16)
    x2 = x.reshape(b * l, d)
    ctx2 = ctx.reshape(b * lc, d)
    gain_pre = norm_pre_mix[li].reshape(1, d)

    p = _inproj(x2, gain_pre, mod3, w_in_b, rows_per_mod=l, mod_row0=0,
                col_tiles=tuple(range(IN_WIDTH // 1024)), tm=1024, tn=1024,
                first_tile_scale=NA_HEAD_DIM ** -0.5 * LOG2_E, rope_tile=OFF_R_Q // 1024, rope_tables=_rope_tables(l))
    tn = 512
    ctx_tiles = tuple(range(OFF_NA_K // tn, OFF_R_Q // tn)) + tuple(range(OFF_R_K // tn, OFF_R_G // tn))
    pc = _inproj(ctx2, gain_pre, mod3, w_in_b, rows_per_mod=b * lc, mod_row0=b,
                 col_tiles=ctx_tiles, tm=1024, tn=tn)
    p = p.reshape(b, l, IN_WIDTH)
    pc = pc.reshape(b, lc, len(ctx_tiles) * tn)
    pc_na_k, pc_na_v = 0, NA_WIDTH
    pc_r_k, pc_r_v = 2 * NA_WIDTH, 2 * NA_WIDTH + RET_QK_WIDTH

    o_na = _neighbourhood_attention(p, pc, pc_na_k, pc_na_v, na_rpb[li])
    o_ret = _retention(p, pc, pc_r_k, pc_r_v, ret_log_gamma_fwd[li].astype(F32), ret_log_gamma_bwd[li].astype(F32))

    x_new, h2 = _outproj(o_na.reshape(b * l, NA_WIDTH), o_ret.reshape(b * l, RET_V_WIDTH), w_out[li].astype(BF16),
                         x2, mod3, norm_post_mix[li].reshape(1, d), norm_pre_ffn[li].reshape(1, d),
                         rows_per_mod=l, tm=512)
    out = _ffn(h2, w_gate[li].astype(BF16), w_up[li].astype(BF16), w_down[li].astype(BF16), x_new, mod3,
               norm_post_ffn[li].reshape(1, d), rows_per_mod=l, tm=1024, tf=512)
    return out.reshape(b, l, d)
```

```python
import functools

import numpy as np
import jax
import jax.numpy as jnp
from jax import lax
from jax.experimental import pallas as pl
from jax.experimental.pallas import tpu as pltpu

F32 = jnp.float32
BF16 = jnp.bfloat16

D_MODEL = 2048
GRID_W = 64
NA_HEADS = 8
NA_HEAD_DIM = 128
NA_WIN_H = 8
NA_WIN_W = 16
RET_HEADS = 8
RET_QK_DIM = 64
RET_V_DIM = 128
NA_WIDTH = NA_HEADS * NA_HEAD_DIM
RET_QK_WIDTH = RET_HEADS * RET_QK_DIM
RET_V_WIDTH = RET_HEADS * RET_V_DIM
IN_WIDTH = 3 * NA_WIDTH + 2 * RET_QK_WIDTH + 2 * RET_V_WIDTH
ROPE_BASE = 10000.0
EPS = 1e-6
MASK_VALUE = -1e30
LOG2_E = float(np.log2(np.e))

OFF_NA_Q = 0
OFF_NA_K = NA_WIDTH
OFF_NA_V = 2 * NA_WIDTH
OFF_R_Q = 3 * NA_WIDTH
OFF_R_K = OFF_R_Q + RET_QK_WIDTH
OFF_R_V = OFF_R_K + RET_QK_WIDTH
OFF_R_G = OFF_R_V + RET_V_WIDTH

VMEM_LIMIT_BYTES = 56 * 1024 * 1024

RET_CHUNK = 256


def _params(*semantics):
    return pltpu.CompilerParams(dimension_semantics=semantics, vmem_limit_bytes=VMEM_LIMIT_BYTES)


def _dot(a, b):
    return jnp.dot(a, b, preferred_element_type=F32)


def _dot_nt(a, b):
    return lax.dot_general(a, b, (((1,), (1,)), ((), ())), preferred_element_type=F32)


def _silu(x):
    return x * jax.nn.sigmoid(x)


def _adaln_kernel(c_ref, w_ref, b_ref, o_ref):
    s = _silu(c_ref[...]).astype(BF16)
    o_ref[...] = _dot(s, w_ref[...].astype(BF16)) + b_ref[...]


def _adaln(cc, ada_w, ada_b, tn=1024):
    rows, d = cc.shape
    n = ada_w.shape[1]
    return pl.pallas_call(
        _adaln_kernel,
        grid=(n // tn,),
        in_specs=[
            pl.BlockSpec((rows, d), lambda j: (0, 0)),
            pl.BlockSpec((d, tn), lambda j: (0, j)),
            pl.BlockSpec((1, tn), lambda j: (0, j)),
        ],
        out_specs=pl.BlockSpec((rows, tn), lambda j: (0, j)),
        out_shape=jax.ShapeDtypeStruct((rows, n), F32),
        compiler_params=_params("arbitrary"),
        name="adaln",
    )(cc, ada_w, ada_b.reshape(1, n))


INPROJ_SUB_ROWS = 256


def _rope_rotate(x, cos, sin_signed):
    lane = lax.broadcasted_iota(jnp.int32, (1, 128), 1)
    first_half = (lane % (RET_QK_DIM // 2)) < (RET_QK_DIM // 4)
    partner = jnp.where(first_half, pltpu.roll(x, 128 - RET_QK_DIM // 4, 1), pltpu.roll(x, RET_QK_DIM // 4, 1))
    return x * cos + partner * sin_signed


def _inproj_kernel(x_hbm, g_ref, sh_ref, sc_ref, w_ref, *refs, first_tile_scale, rope_tile, cast_steps):
    refs = list(refs)
    n_cast = len(cast_steps)
    if rope_tile is not None:
        cos_ref, sin_ref = refs[:2]
        refs = refs[2:]
    cast_src, refs = refs[:n_cast], refs[n_cast:]
    o_ref, refs = refs[0], refs[1:]
    cast_dst, refs = refs[:n_cast], refs[n_cast:]
    h_ref, x_buf, x_sem = refs
    i = pl.program_id(0)
    j = pl.program_id(1)
    tm = h_ref.shape[0]
    n_slices = tm // INPROJ_SUB_ROWS
    slot = lax.rem(i, 2)

    def emit_casts(step):
        for s, src, dst in zip(cast_steps, cast_src, cast_dst):
            if s == step:
                dst[...] = src[...].astype(BF16)

    def x_copy(tile, dst_slot):
        return pltpu.make_async_copy(x_hbm.at[pl.ds(tile * tm, tm), :], x_buf.at[dst_slot], x_sem.at[dst_slot])

    @pl.when((i == 0) & (j == 0))
    def _():
        x_copy(0, 0).start()

    @pl.when((j == 1) & (i + 1 < pl.num_programs(0)))
    def _():
        x_copy(i + 1, 1 - slot).start()

    @pl.when(j == 0)
    def _():
        x_copy(i, slot).wait()
        gain = g_ref[...] * (1.0 + sc_ref[...])
        shift = sh_ref[...]
        for s in range(n_slices):
            rows = pl.ds(s * INPROJ_SUB_ROWS, INPROJ_SUB_ROWS)
            x = x_buf[slot, rows, :]
            ms = jnp.mean(x * x, axis=-1, keepdims=True)
            h = (x * lax.rsqrt(ms + EPS) * gain + shift).astype(BF16)
            h_ref[rows, :] = h
            res = _dot(h, w_ref[...])
            if first_tile_scale != 1.0:
                res = res * first_tile_scale
            o_ref[rows, :] = res.astype(BF16)

    plain = (j > 0) if rope_tile is None else ((j > 0) & (j != rope_tile))
    for step in sorted(set(cast_steps)):
        plain = plain & (j != step)

        @pl.when(j == step)
        def _(step=step):
            emit_casts(step)
            o_ref[...] = _dot(h_ref[...], w_ref[...]).astype(BF16)

    @pl.when(plain)
    def _():
        o_ref[...] = _dot(h_ref[...], w_ref[...]).astype(BF16)

    if rope_tile is not None:
        @pl.when(j == rope_tile)
        def _():
            q_scale = RET_QK_DIM ** -0.5
            for s in range(n_slices):
                rows = pl.ds(s * INPROJ_SUB_ROWS, INPROJ_SUB_ROWS)
                res = _dot(h_ref[rows, :], w_ref[...])
                cos, sin = cos_ref[rows, :], sin_ref[rows, :]
                cos_q, sin_q = cos * q_scale, sin * q_scale
                for cg in range(o_ref.shape[1] // 128):
                    lanes = slice(cg * 128, (cg + 1) * 128)
                    is_q = cg < RET_QK_WIDTH // 128
                    y = _rope_rotate(res[:, lanes], cos_q if is_q else cos, sin_q if is_q else sin)
                    o_ref[rows, lanes] = y.astype(BF16)


def _inproj(x2, gain, mod3, w_bf16, *, rows_per_mod, mod_row0, col_tiles, tm, tn,
            first_tile_scale=1.0, rope_tile=None, rope_tables=None, cast_weights=()):
    m, d = x2.shape
    n_out = len(col_tiles) * tn
    first = col_tiles[0]
    gap_at = next((i for i in range(1, len(col_tiles)) if col_tiles[i] != col_tiles[i - 1] + 1), None)
    if gap_at is None:
        w_map = lambda i, j: (0, j + first)
    else:
        gap = col_tiles[gap_at] - col_tiles[gap_at - 1] - 1
        assert all(col_tiles[i] == first + i + (gap if i >= gap_at else 0) for i in range(len(col_tiles)))
        w_map = lambda i, j: (0, j + first + jnp.where(j >= gap_at, gap, 0))
    mod_row = lambda i: mod_row0 + (i * tm) // rows_per_mod
    assert len(col_tiles) >= 2
    in_specs = [
        pl.BlockSpec(memory_space=pl.ANY),
        pl.BlockSpec((1, d), lambda i, j: (0, 0)),
        pl.BlockSpec((None, 1, d), lambda i, j: (mod_row(i), 0, 0)),
        pl.BlockSpec((None, 1, d), lambda i, j: (mod_row(i), 0, 1)),
        pl.BlockSpec((d, tn), w_map),
    ]
    operands = [x2, gain, mod3, mod3, w_bf16]
    if rope_tile is not None:
        assert tn == 2 * RET_QK_WIDTH and col_tiles[rope_tile] * tn == OFF_R_Q and rows_per_mod % tm == 0
        pos_blk = pl.BlockSpec((tm, 128), lambda i, j: (i % (rows_per_mod // tm), 0))
        in_specs += [pos_blk, pos_blk]
        operands += list(rope_tables)
    n_row_tiles = m // tm
    out_specs = [pl.BlockSpec((tm, tn), lambda i, j: (i, j))]
    out_shape = [jax.ShapeDtypeStruct((m, n_out), BF16)]
    cast_steps = []
    for k, wk in enumerate(cast_weights):
        rk, ck = wk.shape
        assert rk % (n_row_tiles * 16) == 0
        slab = pl.BlockSpec((rk // n_row_tiles, ck), lambda i, j: (i, 0))
        in_specs.append(slab)
        operands.append(wk)
        out_specs.append(slab)
        out_shape.append(jax.ShapeDtypeStruct((rk, ck), BF16))
        free_steps = [s for s in range(1, len(col_tiles)) if s != rope_tile]
        cast_steps.append(free_steps[k % len(free_steps)])
    outs = pl.pallas_call(
        functools.partial(_inproj_kernel, first_tile_scale=first_tile_scale, rope_tile=rope_tile,
                          cast_steps=tuple(cast_steps)),
        grid=(n_row_tiles, len(col_tiles)),
        in_specs=in_specs,
        out_specs=out_specs,
        out_shape=out_shape,
        scratch_shapes=[pltpu.VMEM((tm, d), BF16), pltpu.VMEM((2, tm, d), F32), pltpu.SemaphoreType.DMA((2,))],
        compiler_params=_params("arbitrary", "arbitrary"),
        name="inproj",
    )(*operands)
    return outs[0], tuple(outs[1:])


NA_GROUP_ROWS = 4
NA_BAND_ROWS = NA_GROUP_ROWS + NA_WIN_H
NA_Q = NA_GROUP_ROWS * GRID_W
NA_BAND = NA_BAND_ROWS * GRID_W


def _na_band_start(g, rows):
    lo = g * NA_GROUP_ROWS - NA_WIN_H // 2
    if isinstance(g, (int, np.integer)):
        return min(max(lo, 0), rows - NA_BAND_ROWS)
    return jnp.minimum(jnp.maximum(lo, 0), rows - NA_BAND_ROWS)


def _na_row_patterns(rows):
    groups = rows // NA_GROUP_ROWS
    assert rows % NA_GROUP_ROWS == 0 and rows >= NA_BAND_ROWS and groups >= 3

    def row_pattern(g):
        r = g * NA_GROUP_ROWS + np.arange(NA_GROUP_ROWS)[:, None]
        krow = _na_band_start(g, rows) + np.arange(NA_BAND_ROWS)[None, :]
        rs = np.clip(r - NA_WIN_H // 2, 0, rows - NA_WIN_H)
        ok = (krow >= rs) & (krow < rs + NA_WIN_H)
        return ok, np.clip(krow - r + NA_WIN_H - 1, 0, 2 * NA_WIN_H - 2)

    pats = [row_pattern(0), row_pattern(1), row_pattern(groups - 1)]
    for g in range(1, groups - 1):
        ok, dr = row_pattern(g)
        assert (ok == pats[1][0]).all() and (dr[ok] == pats[1][1][ok]).all()
    return pats


NA_SCORE_AHEAD = 8
NA_HEADS_PER_STEP = 2


def _na_kernel(rpb_ref, q_ref, k_ref, v_ref, kc_ref, vc_ref, o_ref, s_ref, bias_ref, va_ref, vca_ref, *, rows):
    groups = rows // NA_GROUP_ROWS
    hd = NA_HEAD_DIM
    head_lanes = [slice(hh * hd, (hh + 1) * hd) for hh in range(NA_HEADS_PER_STEP)]

    @pl.when(pl.program_id(1) == 0)
    def _():
        n_dr, n_dc = 2 * NA_WIN_H - 1, 2 * NA_WIN_W - 1
        c = lax.broadcasted_iota(jnp.int32, (GRID_W, GRID_W), 0)
        kc = lax.broadcasted_iota(jnp.int32, (GRID_W, GRID_W), 1)
        col_start = jnp.minimum(jnp.maximum(c - NA_WIN_W // 2, 0), GRID_W - NA_WIN_W)
        col_ok = (kc >= col_start) & (kc < col_start + NA_WIN_W)
        dc = kc - c + (NA_WIN_W - 1)
        masked = jnp.full((GRID_W, GRID_W), MASK_VALUE, F32)
        pats = _na_row_patterns(rows)
        used = sorted({int(dr[a, j]) for ok, dr in pats for a, j in zip(*np.nonzero(ok))})
        for hh in range(NA_HEADS_PER_STEP):
            base = (pl.program_id(0) * NA_HEADS_PER_STEP + hh) * (n_dr * n_dc)
            tiles = {}
            for d in used:
                t = jnp.zeros((GRID_W, GRID_W), F32)
                for x in range(n_dc):
                    t = jnp.where(dc == x, rpb_ref[base + d * n_dc + x], t)
                tiles[d] = jnp.where(col_ok, t * LOG2_E, MASK_VALUE)
            for pi, (ok, dr) in enumerate(pats):
                for a in range(NA_GROUP_ROWS):
                    for j in range(NA_BAND_ROWS):
                        tile = tiles[int(dr[a, j])] if ok[a, j] else masked
                        bias_ref[hh, pi, a * GRID_W:(a + 1) * GRID_W, j * GRID_W:(j + 1) * GRID_W] = tile

    for hh, lanes in enumerate(head_lanes):
        va_ref[hh, :, :hd] = v_ref[:, lanes]
        va_ref[hh, :, hd:] = jnp.ones((va_ref.shape[1], hd), BF16)
        vca_ref[hh, :, :hd] = vc_ref[:, lanes]
        vca_ref[hh, :, hd:] = jnp.ones((vca_ref.shape[1], hd), BF16)

    def scores(hh, g, s_ref):
        lanes = head_lanes[hh]
        q0 = g * NA_Q
        k0 = _na_band_start(g, rows) * GRID_W
        pattern = 0 if g == 0 else (2 if g == groups - 1 else 1)
        q = q_ref[pl.ds(q0, NA_Q), lanes]
        s_ref[:, :NA_BAND] = _dot_nt(q, k_ref[pl.ds(k0, NA_BAND), lanes]) + bias_ref[hh, pattern]
        s_ref[:, NA_BAND:] = _dot_nt(q, kc_ref[:, lanes])

    def attend(hh, g, s_ref):
        q0 = g * NA_Q
        k0 = _na_band_start(g, rows) * GRID_W
        s = s_ref[...]
        pb = jnp.exp2(s - jnp.max(s, axis=-1, keepdims=True)).astype(BF16)
        o = _dot(pb[:, :NA_BAND], va_ref[hh, pl.ds(k0, NA_BAND), :]) + _dot(pb[:, NA_BAND:], vca_ref[hh])
        o_ref[pl.ds(q0, NA_Q), head_lanes[hh]] = (o[:, :hd] / o[:, hd:hd + 1]).astype(BF16)

    items = [(hh, g) for hh in range(NA_HEADS_PER_STEP) for g in range(groups)]
    for n in range(min(NA_SCORE_AHEAD, len(items))):
        scores(*items[n], s_ref.at[n])
    for n in range(len(items)):
        if n + NA_SCORE_AHEAD < len(items):
            scores(*items[n + NA_SCORE_AHEAD], s_ref.at[n + NA_SCORE_AHEAD])
        attend(*items[n], s_ref.at[n])


def _neighbourhood_attention(p, pc, pc_off_k, pc_off_v, rpb):
    b, l, _ = p.shape
    lc = pc.shape[1]
    rows = l // GRID_W
    hd = NA_HEAD_DIM
    assert rpb.shape == (NA_HEADS, 2 * NA_WIN_H - 1, 2 * NA_WIN_W - 1)
    hps = NA_HEADS_PER_STEP
    wide = hps * hd
    blk = lambda off: (lambda h, bi: (bi, 0, off // wide + h))
    return pl.pallas_call(
        functools.partial(_na_kernel, rows=rows),
        grid=(NA_HEADS // hps, b),
        in_specs=[
            pl.BlockSpec(memory_space=pltpu.SMEM),
            pl.BlockSpec((None, l, wide), blk(OFF_NA_Q)),
            pl.BlockSpec((None, l, wide), blk(OFF_NA_K)),
            pl.BlockSpec((None, l, wide), blk(OFF_NA_V)),
            pl.BlockSpec((None, lc, wide), blk(pc_off_k)),
            pl.BlockSpec((None, lc, wide), blk(pc_off_v)),
        ],
        out_specs=pl.BlockSpec((None, l, wide), lambda h, bi: (bi, 0, h)),
        out_shape=jax.ShapeDtypeStruct((b, l, NA_WIDTH), BF16),
        scratch_shapes=[pltpu.VMEM((hps * (rows // NA_GROUP_ROWS), NA_Q, NA_BAND + lc), F32),
                        pltpu.VMEM((hps, 3, NA_Q, NA_BAND), F32),
                        pltpu.VMEM((hps, l, 2 * hd), BF16),
                        pltpu.VMEM((hps, lc, 2 * hd), BF16)],
        compiler_params=_params("parallel", "arbitrary"),
        name="na",
    )(rpb.astype(F32).reshape(-1), p, p, p, pc, pc)


def _rope_tables(l):
    t = np.arange(l)
    pos_row, pos_col = t // GRID_W, t % GRID_W
    quarter = RET_QK_DIM // 4
    inv_freq = ROPE_BASE ** (-np.arange(quarter, dtype=np.float64) / quarter)
    ang_r = pos_row[:, None] * inv_freq
    ang_c = pos_col[:, None] * inv_freq
    cos = np.concatenate([np.cos(ang_r), np.cos(ang_r), np.cos(ang_c), np.cos(ang_c)], axis=-1)
    sin = np.concatenate([-np.sin(ang_r), np.sin(ang_r), -np.sin(ang_c), np.sin(ang_c)], axis=-1)
    reps = 128 // RET_QK_DIM
    return (jnp.asarray(np.tile(cos, (1, reps)), F32), jnp.asarray(np.tile(sin, (1, reps)), F32))


def _ret_tables_kernel(lgf_ref, lgb_ref, dk_ref, dq_ref, dbi_ref, cd_ref):
    C = RET_CHUNK
    h = pl.program_id(0)
    lgf = lgf_ref[h]
    lgb = lgb_ref[h]
    lane = lax.broadcasted_iota(jnp.int32, (1, 128), 1)
    head_lanes = (lane // RET_QK_DIM) == (h % 2)
    row = lax.broadcasted_iota(jnp.int32, (C, 128), 0).astype(F32)
    diff = (lax.broadcasted_iota(jnp.int32, (C, C), 0) - lax.broadcasted_iota(jnp.int32, (C, C), 1)).astype(F32)
    dk_ref[...] = jnp.where(head_lanes, jnp.exp(lgf * (C - 1.0 - row)), jnp.exp(lgb * row))
    dq_ref[...] = jnp.where(head_lanes, jnp.exp(lgf * (row + 1.0)), jnp.exp(lgb * (C - row)))
    dbi_ref[...] = (jnp.where(diff >= 0, jnp.exp(lgf * jnp.maximum(diff, 0.0)), 0.0)
                    + jnp.where(diff <= 0, jnp.exp(lgb * jnp.maximum(-diff, 0.0)), 0.0))
    sub = lax.broadcasted_iota(jnp.int32, (8, 128), 0)
    cd_ref[...] = jnp.exp(jnp.where(sub == 0, lgf, lgb) * jnp.full((8, 128), float(C), F32))


def _ret_tables(lg_f, lg_b):
    C = RET_CHUNK
    smem = pl.BlockSpec(memory_space=pltpu.SMEM)
    per_head = lambda *shape: pl.BlockSpec((None,) + shape, lambda h: (h,) + (0,) * len(shape))
    return pl.pallas_call(
        _ret_tables_kernel,
        grid=(RET_HEADS,),
        in_specs=[smem, smem],
        out_specs=[per_head(C, 128), per_head(C, 128), per_head(C, C), per_head(8, 128)],
        out_shape=[jax.ShapeDtypeStruct((RET_HEADS, C, 128), F32), jax.ShapeDtypeStruct((RET_HEADS, C, 128), F32),
                   jax.ShapeDtypeStruct((RET_HEADS, C, C), F32), jax.ShapeDtypeStruct((RET_HEADS, 8, 128), F32)],
        compiler_params=_params("arbitrary"),
        name="ret_tables",
    )(lg_f, lg_b)


def _ret_kernel(q_ref, k_ref, v_ref, g_ref, kc_ref, vc_ref, dk_ref, dq_ref, dbi_ref, cd_ref,
                o_ref, a_ref, sb_ref, st_ref, *, n_chunks):
    C = RET_CHUNK
    lane = lax.broadcasted_iota(jnp.int32, (1, 128), 1)
    sub = lax.broadcasted_iota(jnp.int32, (128, 1), 0)

    for e in range(2):
        head_lanes = (lane // RET_QK_DIM) == e
        fwd_rows = (sub // RET_QK_DIM) == e
        vsl = slice(e * RET_V_DIM, (e + 1) * RET_V_DIM)
        dk = dk_ref[e]
        cdf = cd_ref[e, 0:1, :]
        cdb = cd_ref[e, 1:2, :]

        def local_state(k_bf16, v_bf16):
            kf32 = k_bf16.astype(F32)
            k_both = jnp.where(head_lanes, kf32, pltpu.roll(kf32, RET_QK_DIM, 1))
            return _dot((k_both * dk).T.astype(BF16), v_bf16)

        ctx_state = local_state(kc_ref[...], vc_ref[:, vsl])
        for i in range(n_chunks):
            rows = pl.ds(i * C, C)
            a_ref[i] = local_state(k_ref[rows, :], v_ref[rows, vsl])

        s_b = ctx_state
        for i in reversed(range(n_chunks)):
            sb_ref[i] = s_b
            s_b = s_b * cdb + a_ref[i]
        s_f = ctx_state
        for i in range(n_chunks):
            st_ref[e, i] = jnp.where(fwd_rows, s_f, sb_ref[i]).astype(BF16)
            s_f = s_f * cdf + a_ref[i]

    for i in range(n_chunks):
        rows = pl.ds(i * C, C)
        q = q_ref[rows, :].astype(F32)
        q_rolled = pltpu.roll(q, RET_QK_DIM, 1)
        kk = k_ref[rows, :]
        for e in range(2):
            head_lanes = (lane // RET_QK_DIM) == e
            vsl = slice(e * RET_V_DIM, (e + 1) * RET_V_DIM)
            q_own = jnp.where(head_lanes, q, 0.0).astype(BF16)
            q_both = (jnp.where(head_lanes, q, q_rolled) * dq_ref[e]).astype(BF16)
            inner = _dot_nt(q_own, kk) * dbi_ref[e]
            o = _dot(inner.astype(BF16), v_ref[rows, vsl]) + _dot(q_both, st_ref[e, i])
            mu = jnp.mean(o, axis=-1, keepdims=True)
            oc = o - mu
            var = jnp.mean(oc * oc, axis=-1, keepdims=True)
            gate = _silu(g_ref[rows, vsl].astype(F32))
            o_ref[rows, vsl] = (oc * lax.rsqrt(var + EPS) * gate).astype(BF16)


def _retention(p, pc, pc_off_k, pc_off_v, lg_f, lg_b):
    b, l, _ = p.shape
    lc = pc.shape[1]
    assert lc == RET_CHUNK and l % RET_CHUNK == 0
    n_chunks = l // RET_CHUNK
    dk, dq, dbi, cd = _ret_tables(lg_f, lg_b)
    qk_blk = lambda off: (lambda hp, bi: (bi, 0, off // 128 + hp))
    v_blk = lambda off: (lambda hp, bi: (bi, 0, off // 256 + hp))
    pair = lambda *shape: pl.BlockSpec((2,) + shape, lambda hp, bi: (hp,) + (0,) * len(shape))
    return pl.pallas_call(
        functools.partial(_ret_kernel, n_chunks=n_chunks),
        grid=(RET_HEADS // 2, b),
        in_specs=[
            pl.BlockSpec((None, l, 128), qk_blk(OFF_R_Q)),
            pl.BlockSpec((None, l, 128), qk_blk(OFF_R_K)),
            pl.BlockSpec((None, l, 256), v_blk(OFF_R_V)),
            pl.BlockSpec((None, l, 256), v_blk(OFF_R_G)),
            pl.BlockSpec((None, lc, 128), qk_blk(pc_off_k)),
            pl.BlockSpec((None, lc, 256), v_blk(pc_off_v)),
            pair(RET_CHUNK, 128), pair(RET_CHUNK, 128), pair(RET_CHUNK, RET_CHUNK), pair(8, 128),
        ],
        out_specs=pl.BlockSpec((None, l, 256), lambda hp, bi: (bi, 0, hp)),
        out_shape=jax.ShapeDtypeStruct((b, l, RET_V_WIDTH), BF16),
        scratch_shapes=[
            pltpu.VMEM((n_chunks, 128, RET_V_DIM), F32),
            pltpu.VMEM((n_chunks, 128, RET_V_DIM), F32),
            pltpu.VMEM((2, n_chunks, 128, RET_V_DIM), BF16),
        ],
        compiler_params=_params("parallel", "parallel"),
        name="ret",
    )(p, p, p, p, pc, pc, dk, dq, dbi, cd)


OUTPROJ_SUB_ROWS = 256


def _outproj_kernel(na_ref, ret_ref, wt_ref, wb_ref, x_ref, g1_ref, sh2_ref, sc2_ref, npost_ref, npre_ref,
                    xn_ref, h2_ref):
    gate_gain = g1_ref[...] * npost_ref[...]
    gain2 = npre_ref[...] * (1.0 + sc2_ref[...])
    shift2 = sh2_ref[...]
    for s in range(x_ref.shape[0] // OUTPROJ_SUB_ROWS):
        rows = pl.ds(s * OUTPROJ_SUB_ROWS, OUTPROJ_SUB_ROWS)
        mix = _dot(na_ref[rows, :], wt_ref[...]) + _dot(ret_ref[rows, :], wb_ref[...])
        ms = jnp.mean(mix * mix, axis=-1, keepdims=True)
        xn = x_ref[rows, :] + mix * lax.rsqrt(ms + EPS) * gate_gain
        xn_ref[rows, :] = xn
        ms2 = jnp.mean(xn * xn, axis=-1, keepdims=True)
        h2_ref[rows, :] = (xn * lax.rsqrt(ms2 + EPS) * gain2 + shift2).astype(BF16)


def _outproj(na, ret, w_out_bf16, x2, mod3, norm_post, norm_pre, *, rows_per_mod, tm):
    m, d = x2.shape
    kh = na.shape[1]
    mod_row = lambda i: (i * tm) // rows_per_mod
    mod_blk = lambda k: pl.BlockSpec((None, 1, d), lambda i: (mod_row(i), 0, k))
    return pl.pallas_call(
        _outproj_kernel,
        grid=(m // tm,),
        in_specs=[
            pl.BlockSpec((tm, kh), lambda i: (i, 0)),
            pl.BlockSpec((tm, kh), lambda i: (i, 0)),
            pl.BlockSpec((kh, d), lambda i: (0, 0)),
            pl.BlockSpec((kh, d), lambda i: (1, 0)),
            pl.BlockSpec((tm, d), lambda i: (i, 0)),
            mod_blk(2), mod_blk(3), mod_blk(4),
            pl.BlockSpec((1, d), lambda i: (0, 0)),
            pl.BlockSpec((1, d), lambda i: (0, 0)),
        ],
        out_specs=[pl.BlockSpec((tm, d), lambda i: (i, 0)), pl.BlockSpec((tm, d), lambda i: (i, 0))],
        out_shape=[jax.ShapeDtypeStruct((m, d), F32), jax.ShapeDtypeStruct((m, d), BF16)],
        compiler_params=_params("parallel"),
        name="outproj",
    )(na, ret, w_out_bf16, w_out_bf16, x2, mod3, mod3, mod3, norm_post, norm_pre)


FFN_SUB_ROWS = 256
FFN_RESIDUAL_CHUNKS = 4


def _ffn_kernel(h_ref, wg_ref, wu_ref, wd_ref, xn_hbm, g2_ref, npost_ref, o_ref, xn_buf, xn_sem):
    i = pl.program_id(0)
    f = pl.program_id(1)
    tm = h_ref.shape[0]
    chunk = tm // FFN_RESIDUAL_CHUNKS

    def residual_copy(k):
        return pltpu.make_async_copy(xn_hbm.at[pl.ds(i * tm + k * chunk, chunk), :],
                                     xn_buf.at[pl.ds(k * chunk, chunk), :], xn_sem.at[k])

    for k in range(FFN_RESIDUAL_CHUNKS):
        @pl.when(f == k + 1)
        def _():
            residual_copy(k).start()

    def gated():
        h = h_ref[...]
        return (_silu(_dot(h, wg_ref[...])) * _dot(h, wu_ref[...])).astype(BF16)

    last = pl.num_programs(1) - 1

    @pl.when(f == 0)
    def _():
        o_ref[...] = _dot(gated(), wd_ref[...])

    @pl.when((f > 0) & (f < last))
    def _():
        o_ref[...] += _dot(gated(), wd_ref[...])

    @pl.when(f == last)
    def _():
        t = gated()
        for k in range(FFN_RESIDUAL_CHUNKS):
            residual_copy(k).wait()
        gate_gain = g2_ref[...] * npost_ref[...]
        for s in range(tm // FFN_SUB_ROWS):
            rows = slice(s * FFN_SUB_ROWS, (s + 1) * FFN_SUB_ROWS)
            y = o_ref[rows, :] + _dot(t[rows, :], wd_ref[...])
            ms = jnp.mean(y * y, axis=-1, keepdims=True)
            o_ref[rows, :] = xn_buf[rows, :] + y * lax.rsqrt(ms + EPS) * gate_gain


def _ffn(h2, wg, wu, wd, xn, mod3, norm_post, *, rows_per_mod, tm, tf):
    m, d = h2.shape
    dff = wd.shape[0]
    mod_row = lambda i: (i * tm) // rows_per_mod
    assert dff // tf >= FFN_RESIDUAL_CHUNKS + 2 and tm % FFN_RESIDUAL_CHUNKS == 0
    return pl.pallas_call(
        _ffn_kernel,
        grid=(m // tm, dff // tf),
        in_specs=[
            pl.BlockSpec((tm, d), lambda i, f: (i, 0)),
            pl.BlockSpec((d, tf), lambda i, f: (0, f)),
            pl.BlockSpec((d, tf), lambda i, f: (0, f)),
            pl.BlockSpec((tf, d), lambda i, f: (f, 0)),
            pl.BlockSpec(memory_space=pl.ANY),
            pl.BlockSpec((None, 1, d), lambda i, f: (mod_row(i), 0, 5)),
            pl.BlockSpec((1, d), lambda i, f: (0, 0)),
        ],
        out_specs=pl.BlockSpec((tm, d), lambda i, f: (i, 0)),
        out_shape=jax.ShapeDtypeStruct((m, d), F32),
        scratch_shapes=[pltpu.VMEM((tm, d), F32), pltpu.SemaphoreType.DMA((FFN_RESIDUAL_CHUNKS,))],
        compiler_params=_params("parallel", "arbitrary"),
        name="ffn",
    )(h2, wg, wu, wd, xn, mod3, norm_post)


def kernel(x, c, ctx, c_ctx, ada_w, ada_b, norm_pre_mix, norm_post_mix, norm_pre_ffn, norm_post_ffn, w_in,
           na_rpb, ret_log_gamma_fwd, ret_log_gamma_bwd, w_out, w_gate, w_up, w_down):
    b, l, d = x.shape
    lc = ctx.shape[1]
    depth = ada_w.shape[0]
    assert depth == 1, "the context-stream update between layers is not implemented"
    li = 0

    mod_rows = -(-(b + 1) // 8) * 8
    cc = jnp.concatenate([c, c_ctx[None, :], jnp.zeros((mod_rows - b - 1, d), F32)], axis=0)
    mod3 = _adaln(cc, ada_w[li], ada_b[li]).reshape(mod_rows, 1, 6 * d)

    w_in_b = w_in[li].astype(BF16)
    x2 = x.reshape(b * l, d)
    ctx2 = ctx.reshape(b * lc, d)
    gain_pre = norm_pre_mix[li].reshape(1, d)

    p, (w_out_b, w_gate_b, w_up_b, w_down_b) = _inproj(
        x2, gain_pre, mod3, w_in_b, rows_per_mod=l, mod_row0=0,
        col_tiles=tuple(range(IN_WIDTH // 1024)), tm=1024, tn=1024,
        first_tile_scale=NA_HEAD_DIM ** -0.5 * LOG2_E, rope_tile=OFF_R_Q // 1024, rope_tables=_rope_tables(l),
        cast_weights=(w_out[li], w_gate[li], w_up[li], w_down[li]))
    tn = 512
    ctx_tiles = tuple(range(OFF_NA_K // tn, OFF_R_Q // tn)) + tuple(range(OFF_R_K // tn, OFF_R_G // tn))
    pc, _ = _inproj(ctx2, gain_pre, mod3, w_in_b, rows_per_mod=b * lc, mod_row0=b,
                    col_tiles=ctx_tiles, tm=1024, tn=tn)
    p = p.reshape(b, l, IN_WIDTH)
    pc = pc.reshape(b, lc, len(ctx_tiles) * tn)
    pc_na_k, pc_na_v = 0, NA_WIDTH
    pc_r_k, pc_r_v = 2 * NA_WIDTH, 2 * NA_WIDTH + RET_QK_WIDTH

    o_na = _neighbourhood_attention(p, pc, pc_na_k, pc_na_v, na_rpb[li])
    o_ret = _retention(p, pc, pc_r_k, pc_r_v, ret_log_gamma_fwd[li].astype(F32), ret_log_gamma_bwd[li].astype(F32))

    x_new, h2 = _outproj(o_na.reshape(b * l, NA_WIDTH), o_ret.reshape(b * l, RET_V_WIDTH), w_out_b,
                         x2, mod3, norm_post_mix[li].reshape(1, d), norm_pre_ffn[li].reshape(1, d),
                         rows_per_mod=l, tm=512)
    out = _ffn(h2, w_gate_b, w_up_b, w_down_b, x_new, mod3,
               norm_post_ffn[li].reshape(1, d), rows_per_mod=l, tm=1024, tf=512)
    return out.reshape(b, l, d)
```

```python
import functools

import numpy as np
import jax
import jax.numpy as jnp
from jax import lax
from jax.experimental import pallas as pl
from jax.experimental.pallas import tpu as pltpu

F32 = jnp.float32
BF16 = jnp.bfloat16

D_MODEL = 2048
GRID_W = 64
NA_HEADS = 8
NA_HEAD_DIM = 128
NA_WIN_H = 8
NA_WIN_W = 16
RET_HEADS = 8
RET_QK_DIM = 64
RET_V_DIM = 128
NA_WIDTH = NA_HEADS * NA_HEAD_DIM
RET_QK_WIDTH = RET_HEADS * RET_QK_DIM
RET_V_WIDTH = RET_HEADS * RET_V_DIM
IN_WIDTH = 3 * NA_WIDTH + 2 * RET_QK_WIDTH + 2 * RET_V_WIDTH
ROPE_BASE = 10000.0
EPS = 1e-6
MASK_VALUE = -1e30
LOG2_E = float(np.log2(np.e))

OFF_NA_Q = 0
OFF_NA_K = NA_WIDTH
OFF_NA_V = 2 * NA_WIDTH
OFF_R_Q = 3 * NA_WIDTH
OFF_R_K = OFF_R_Q + RET_QK_WIDTH
OFF_R_V = OFF_R_K + RET_QK_WIDTH
OFF_R_G = OFF_R_V + RET_V_WIDTH

VMEM_LIMIT_BYTES = 56 * 1024 * 1024

RET_CHUNK = 256


def _params(*semantics):
    return pltpu.CompilerParams(dimension_semantics=semantics, vmem_limit_bytes=VMEM_LIMIT_BYTES)


def _dot(a, b):
    return jnp.dot(a, b, preferred_element_type=F32)


def _dot_nt(a, b):
    return lax.dot_general(a, b, (((1,), (1,)), ((), ())), preferred_element_type=F32)


def _silu(x):
    return x * jax.nn.sigmoid(x)


def _adaln_kernel(c_ref, w_ref, b_ref, o_ref):
    s = _silu(c_ref[...]).astype(BF16)
    o_ref[...] = _dot(s, w_ref[...].astype(BF16)) + b_ref[...]


def _adaln(cc, ada_w, ada_b, tn=1024):
    rows, d = cc.shape
    n = ada_w.shape[1]
    return pl.pallas_call(
        _adaln_kernel,
        grid=(n // tn,),
        in_specs=[
            pl.BlockSpec((rows, d), lambda j: (0, 0)),
            pl.BlockSpec((d, tn), lambda j: (0, j)),
            pl.BlockSpec((1, tn), lambda j: (0, j)),
        ],
        out_specs=pl.BlockSpec((rows, tn), lambda j: (0, j)),
        out_shape=jax.ShapeDtypeStruct((rows, n), F32),
        compiler_params=_params("arbitrary"),
        name="adaln",
    )(cc, ada_w, ada_b.reshape(1, n))


INPROJ_SUB_ROWS = 256


def _rope_rotate(x, cos, sin_signed):
    lane = lax.broadcasted_iota(jnp.int32, (1, 128), 1)
    first_half = (lane % (RET_QK_DIM // 2)) < (RET_QK_DIM // 4)
    partner = jnp.where(first_half, pltpu.roll(x, 128 - RET_QK_DIM // 4, 1), pltpu.roll(x, RET_QK_DIM // 4, 1))
    return x * cos + partner * sin_signed


def _inproj_kernel(x_hbm, g_ref, sh_ref, sc_ref, w_ref, *refs, first_tile_scale, rope_tile, cast_steps):
    refs = list(refs)
    n_cast = len(cast_steps)
    if rope_tile is not None:
        cos_ref, sin_ref = refs[:2]
        refs = refs[2:]
    cast_src, refs = refs[:n_cast], refs[n_cast:]
    o_ref, refs = refs[0], refs[1:]
    cast_dst, refs = refs[:n_cast], refs[n_cast:]
    h_ref, x_buf, x_sem = refs
    i = pl.program_id(0)
    j = pl.program_id(1)
    tm = h_ref.shape[0]
    n_slices = tm // INPROJ_SUB_ROWS
    slot = lax.rem(i, 2)

    def emit_casts(step):
        for s, src, dst in zip(cast_steps, cast_src, cast_dst):
            if s == step:
                dst[...] = src[...].astype(BF16)

    def x_copy(tile, dst_slot):
        return pltpu.make_async_copy(x_hbm.at[pl.ds(tile * tm, tm), :], x_buf.at[dst_slot], x_sem.at[dst_slot])

    @pl.when((i == 0) & (j == 0))
    def _():
        x_copy(0, 0).start()

    @pl.when((j == 1) & (i + 1 < pl.num_programs(0)))
    def _():
        x_copy(i + 1, 1 - slot).start()

    @pl.when(j == 0)
    def _():
        x_copy(i, slot).wait()
        gain = g_ref[...] * (1.0 + sc_ref[...])
        shift = sh_ref[...]
        for s in range(n_slices):
            rows = pl.ds(s * INPROJ_SUB_ROWS, INPROJ_SUB_ROWS)
            x = x_buf[slot, rows, :]
            ms = jnp.mean(x * x, axis=-1, keepdims=True)
            h = (x * lax.rsqrt(ms + EPS) * gain + shift).astype(BF16)
            h_ref[rows, :] = h
            res = _dot(h, w_ref[...])
            if first_tile_scale != 1.0:
                res = res * first_tile_scale
            o_ref[rows, :] = res.astype(BF16)

    plain = (j > 0) if rope_tile is None else ((j > 0) & (j != rope_tile))
    for step in sorted(set(cast_steps)):
        plain = plain & (j != step)

        @pl.when(j == step)
        def _(step=step):
            emit_casts(step)
            o_ref[...] = _dot(h_ref[...], w_ref[...]).astype(BF16)

    @pl.when(plain)
    def _():
        o_ref[...] = _dot(h_ref[...], w_ref[...]).astype(BF16)

    if rope_tile is not None:
        @pl.when(j == rope_tile)
        def _():
            q_scale = RET_QK_DIM ** -0.5
            for s in range(n_slices):
                rows = pl.ds(s * INPROJ_SUB_ROWS, INPROJ_SUB_ROWS)
                res = _dot(h_ref[rows, :], w_ref[...])
                cos, sin = cos_ref[rows, :], sin_ref[rows, :]
                cos_q, sin_q = cos * q_scale, sin * q_scale
                for cg in range(o_ref.shape[1] // 128):
                    lanes = slice(cg * 128, (cg + 1) * 128)
                    is_q = cg < RET_QK_WIDTH // 128
                    y = _rope_rotate(res[:, lanes], cos_q if is_q else cos, sin_q if is_q else sin)
                    o_ref[rows, lanes] = y.astype(BF16)


def _inproj(x2, gain, mod3, w_bf16, *, rows_per_mod, mod_row0, col_tiles, tm, tn,
            first_tile_scale=1.0, rope_tile=None, rope_tables=None, cast_weights=()):
    m, d = x2.shape
    n_out = len(col_tiles) * tn
    first = col_tiles[0]
    gap_at = next((i for i in range(1, len(col_tiles)) if col_tiles[i] != col_tiles[i - 1] + 1), None)
    if gap_at is None:
        w_map = lambda i, j: (0, j + first)
    else:
        gap = col_tiles[gap_at] - col_tiles[gap_at - 1] - 1
        assert all(col_tiles[i] == first + i + (gap if i >= gap_at else 0) for i in range(len(col_tiles)))
        w_map = lambda i, j: (0, j + first + jnp.where(j >= gap_at, gap, 0))
    mod_row = lambda i: mod_row0 + (i * tm) // rows_per_mod
    assert len(col_tiles) >= 2
    in_specs = [
        pl.BlockSpec(memory_space=pl.ANY),
        pl.BlockSpec((1, d), lambda i, j: (0, 0)),
        pl.BlockSpec((None, 1, d), lambda i, j: (mod_row(i), 0, 0)),
        pl.BlockSpec((None, 1, d), lambda i, j: (mod_row(i), 0, 1)),
        pl.BlockSpec((d, tn), w_map),
    ]
    operands = [x2, gain, mod3, mod3, w_bf16]
    if rope_tile is not None:
        assert tn == 2 * RET_QK_WIDTH and col_tiles[rope_tile] * tn == OFF_R_Q and rows_per_mod % tm == 0
        pos_blk = pl.BlockSpec((tm, 128), lambda i, j: (i % (rows_per_mod // tm), 0))
        in_specs += [pos_blk, pos_blk]
        operands += list(rope_tables)
    n_row_tiles = m // tm
    out_specs = [pl.BlockSpec((tm, tn), lambda i, j: (i, j))]
    out_shape = [jax.ShapeDtypeStruct((m, n_out), BF16)]
    cast_steps = []
    for k, wk in enumerate(cast_weights):
        rk, ck = wk.shape
        assert rk % (n_row_tiles * 16) == 0
        slab = pl.BlockSpec((rk // n_row_tiles, ck), lambda i, j: (i, 0))
        in_specs.append(slab)
        operands.append(wk)
        out_specs.append(slab)
        out_shape.append(jax.ShapeDtypeStruct((rk, ck), BF16))
        free_steps = [s for s in range(1, len(col_tiles)) if s != rope_tile]
        cast_steps.append(free_steps[k % len(free_steps)])
    outs = pl.pallas_call(
        functools.partial(_inproj_kernel, first_tile_scale=first_tile_scale, rope_tile=rope_tile,
                          cast_steps=tuple(cast_steps)),
        grid=(n_row_tiles, len(col_tiles)),
        in_specs=in_specs,
        out_specs=out_specs,
        out_shape=out_shape,
        scratch_shapes=[pltpu.VMEM((tm, d), BF16), pltpu.VMEM((2, tm, d), F32), pltpu.SemaphoreType.DMA((2,))],
        compiler_params=_params("arbitrary", "arbitrary"),
        name="inproj",
    )(*operands)
    return outs[0], tuple(outs[1:])


NA_GROUP_ROWS = 4
NA_BAND_ROWS = NA_GROUP_ROWS + NA_WIN_H
NA_Q = NA_GROUP_ROWS * GRID_W
NA_BAND = NA_BAND_ROWS * GRID_W


def _na_band_start(g, rows):
    lo = g * NA_GROUP_ROWS - NA_WIN_H // 2
    if isinstance(g, (int, np.integer)):
        return min(max(lo, 0), rows - NA_BAND_ROWS)
    return jnp.minimum(jnp.maximum(lo, 0), rows - NA_BAND_ROWS)


def _na_row_patterns(rows):
    groups = rows // NA_GROUP_ROWS
    assert rows % NA_GROUP_ROWS == 0 and rows >= NA_BAND_ROWS and groups >= 3

    def row_pattern(g):
        r = g * NA_GROUP_ROWS + np.arange(NA_GROUP_ROWS)[:, None]
        krow = _na_band_start(g, rows) + np.arange(NA_BAND_ROWS)[None, :]
        rs = np.clip(r - NA_WIN_H // 2, 0, rows - NA_WIN_H)
        ok = (krow >= rs) & (krow < rs + NA_WIN_H)
        return ok, np.clip(krow - r + NA_WIN_H - 1, 0, 2 * NA_WIN_H - 2)

    pats = [row_pattern(0), row_pattern(1), row_pattern(groups - 1)]
    for g in range(1, groups - 1):
        ok, dr = row_pattern(g)
        assert (ok == pats[1][0]).all() and (dr[ok] == pats[1][1][ok]).all()
    return pats


NA_SCORE_AHEAD = 8
NA_HEADS_PER_STEP = 2


def _na_kernel(rpb_ref, q_ref, k_ref, v_ref, kc_ref, vc_ref, o_ref, s_ref, bias_ref, va_ref, vca_ref, *, rows):
    groups = rows // NA_GROUP_ROWS
    hd = NA_HEAD_DIM
    head_lanes = [slice(hh * hd, (hh + 1) * hd) for hh in range(NA_HEADS_PER_STEP)]

    @pl.when(pl.program_id(1) == 0)
    def _():
        n_dr, n_dc = 2 * NA_WIN_H - 1, 2 * NA_WIN_W - 1
        c = lax.broadcasted_iota(jnp.int32, (GRID_W, GRID_W), 0)
        kc = lax.broadcasted_iota(jnp.int32, (GRID_W, GRID_W), 1)
        col_start = jnp.minimum(jnp.maximum(c - NA_WIN_W // 2, 0), GRID_W - NA_WIN_W)
        col_ok = (kc >= col_start) & (kc < col_start + NA_WIN_W)
        dc = kc - c + (NA_WIN_W - 1)
        masked = jnp.full((GRID_W, GRID_W), MASK_VALUE, F32)
        pats = _na_row_patterns(rows)
        used = sorted({int(dr[a, j]) for ok, dr in pats for a, j in zip(*np.nonzero(ok))})
        for hh in range(NA_HEADS_PER_STEP):
            base = (pl.program_id(0) * NA_HEADS_PER_STEP + hh) * (n_dr * n_dc)
            tiles = {}
            for d in used:
                t = jnp.zeros((GRID_W, GRID_W), F32)
                for x in range(n_dc):
                    t = jnp.where(dc == x, rpb_ref[base + d * n_dc + x], t)
                tiles[d] = jnp.where(col_ok, t * LOG2_E, MASK_VALUE)
            for pi, (ok, dr) in enumerate(pats):
                for a in range(NA_GROUP_ROWS):
                    for j in range(NA_BAND_ROWS):
                        tile = tiles[int(dr[a, j])] if ok[a, j] else masked
                        bias_ref[hh, pi, a * GRID_W:(a + 1) * GRID_W, j * GRID_W:(j + 1) * GRID_W] = tile

    for hh, lanes in enumerate(head_lanes):
        va_ref[hh, :, :hd] = v_ref[:, lanes]
        va_ref[hh, :, hd:] = jnp.ones((va_ref.shape[1], hd), BF16)
        vca_ref[hh, :, :hd] = vc_ref[:, lanes]
        vca_ref[hh, :, hd:] = jnp.ones((vca_ref.shape[1], hd), BF16)

    def scores(hh, g, s_ref):
        lanes = head_lanes[hh]
        q0 = g * NA_Q
        k0 = _na_band_start(g, rows) * GRID_W
        pattern = 0 if g == 0 else (2 if g == groups - 1 else 1)
        q = q_ref[pl.ds(q0, NA_Q), lanes]
        s_ref[:, :NA_BAND] = _dot_nt(q, k_ref[pl.ds(k0, NA_BAND), lanes]) + bias_ref[hh, pattern]
        s_ref[:, NA_BAND:] = _dot_nt(q, kc_ref[:, lanes])

    def attend(hh, g, s_ref):
        q0 = g * NA_Q
        k0 = _na_band_start(g, rows) * GRID_W
        s = s_ref[...]
        pb = jnp.exp2(s - jnp.max(s, axis=-1, keepdims=True)).astype(BF16)
        o = _dot(pb[:, :NA_BAND], va_ref[hh, pl.ds(k0, NA_BAND), :]) + _dot(pb[:, NA_BAND:], vca_ref[hh])
        o_ref[pl.ds(q0, NA_Q), head_lanes[hh]] = (o[:, :hd] / o[:, hd:hd + 1]).astype(BF16)

    items = [(hh, g) for hh in range(NA_HEADS_PER_STEP) for g in range(groups)]
    for n in range(min(NA_SCORE_AHEAD, len(items))):
        scores(*items[n], s_ref.at[n])
    for n in range(len(items)):
        if n + NA_SCORE_AHEAD < len(items):
            scores(*items[n + NA_SCORE_AHEAD], s_ref.at[n + NA_SCORE_AHEAD])
        attend(*items[n], s_ref.at[n])


def _neighbourhood_attention(p, pc, pc_off_k, pc_off_v, rpb):
    b, l, _ = p.shape
    lc = pc.shape[1]
    rows = l // GRID_W
    hd = NA_HEAD_DIM
    assert rpb.shape == (NA_HEADS, 2 * NA_WIN_H - 1, 2 * NA_WIN_W - 1)
    hps = NA_HEADS_PER_STEP
    wide = hps * hd
    blk = lambda off: (lambda h, bi: (bi, 0, off // wide + h))
    return pl.pallas_call(
        functools.partial(_na_kernel, rows=rows),
        grid=(NA_HEADS // hps, b),
        in_specs=[
            pl.BlockSpec(memory_space=pltpu.SMEM),
            pl.BlockSpec((None, l, wide), blk(OFF_NA_Q)),
            pl.BlockSpec((None, l, wide), blk(OFF_NA_K)),
            pl.BlockSpec((None, l, wide), blk(OFF_NA_V)),
            pl.BlockSpec((None, lc, wide), blk(pc_off_k)),
            pl.BlockSpec((None, lc, wide), blk(pc_off_v)),
        ],
        out_specs=pl.BlockSpec((None, l, wide), lambda h, bi: (bi, 0, h)),
        out_shape=jax.ShapeDtypeStruct((b, l, NA_WIDTH), BF16),
        scratch_shapes=[pltpu.VMEM((hps * (rows // NA_GROUP_ROWS), NA_Q, NA_BAND + lc), F32),
                        pltpu.VMEM((hps, 3, NA_Q, NA_BAND), F32),
                        pltpu.VMEM((hps, l, 2 * hd), BF16),
                        pltpu.VMEM((hps, lc, 2 * hd), BF16)],
        compiler_params=_params("parallel", "arbitrary"),
        name="na",
    )(rpb.astype(F32).reshape(-1), p, p, p, pc, pc)


def _rope_tables(l):
    t = np.arange(l)
    pos_row, pos_col = t // GRID_W, t % GRID_W
    quarter = RET_QK_DIM // 4
    inv_freq = ROPE_BASE ** (-np.arange(quarter, dtype=np.float64) / quarter)
    ang_r = pos_row[:, None] * inv_freq
    ang_c = pos_col[:, None] * inv_freq
    cos = np.concatenate([np.cos(ang_r), np.cos(ang_r), np.cos(ang_c), np.cos(ang_c)], axis=-1)
    sin = np.concatenate([-np.sin(ang_r), np.sin(ang_r), -np.sin(ang_c), np.sin(ang_c)], axis=-1)
    reps = 128 // RET_QK_DIM
    return (jnp.asarray(np.tile(cos, (1, reps)), F32), jnp.asarray(np.tile(sin, (1, reps)), F32))


def _ret_tables_kernel(lgf_ref, lgb_ref, dk_ref, dq_ref, dbi_ref, cd_ref):
    C = RET_CHUNK
    h = pl.program_id(0)
    lgf = lgf_ref[h]
    lgb = lgb_ref[h]
    lane = lax.broadcasted_iota(jnp.int32, (1, 128), 1)
    head_lanes = (lane // RET_QK_DIM) == (h % 2)
    row = lax.broadcasted_iota(jnp.int32, (C, 128), 0).astype(F32)
    diff = (lax.broadcasted_iota(jnp.int32, (C, C), 0) - lax.broadcasted_iota(jnp.int32, (C, C), 1)).astype(F32)
    dk_ref[...] = jnp.where(head_lanes, jnp.exp(lgf * (C - 1.0 - row)), jnp.exp(lgb * row))
    dq_ref[...] = jnp.where(head_lanes, jnp.exp(lgf * (row + 1.0)), jnp.exp(lgb * (C - row)))
    dbi_ref[...] = (jnp.where(diff >= 0, jnp.exp(lgf * jnp.maximum(diff, 0.0)), 0.0)
                    + jnp.where(diff <= 0, jnp.exp(lgb * jnp.maximum(-diff, 0.0)), 0.0))
    sub = lax.broadcasted_iota(jnp.int32, (8, 128), 0)
    cd_ref[...] = jnp.exp(jnp.where(sub == 0, lgf, lgb) * jnp.full((8, 128), float(C), F32))


def _ret_tables(lg_f, lg_b):
    C = RET_CHUNK
    smem = pl.BlockSpec(memory_space=pltpu.SMEM)
    per_head = lambda *shape: pl.BlockSpec((None,) + shape, lambda h: (h,) + (0,) * len(shape))
    return pl.pallas_call(
        _ret_tables_kernel,
        grid=(RET_HEADS,),
        in_specs=[smem, smem],
        out_specs=[per_head(C, 128), per_head(C, 128), per_head(C, C), per_head(8, 128)],
        out_shape=[jax.ShapeDtypeStruct((RET_HEADS, C, 128), F32), jax.ShapeDtypeStruct((RET_HEADS, C, 128), F32),
                   jax.ShapeDtypeStruct((RET_HEADS, C, C), F32), jax.ShapeDtypeStruct((RET_HEADS, 8, 128), F32)],
        compiler_params=_params("arbitrary"),
        name="ret_tables",
    )(lg_f, lg_b)


def _ret_kernel(q_ref, k_ref, v_ref, g_ref, kc_ref, vc_ref, dk_ref, dq_ref, dbi_ref, cd_ref,
                o_ref, a_ref, sb_ref, st_ref, *, n_chunks):
    C = RET_CHUNK
    lane = lax.broadcasted_iota(jnp.int32, (1, 128), 1)
    sub = lax.broadcasted_iota(jnp.int32, (128, 1), 0)

    for e in range(2):
        head_lanes = (lane // RET_QK_DIM) == e
        fwd_rows = (sub // RET_QK_DIM) == e
        vsl = slice(e * RET_V_DIM, (e + 1) * RET_V_DIM)
        dk = dk_ref[e]
        cdf = cd_ref[e, 0:1, :]
        cdb = cd_ref[e, 1:2, :]

        def local_state(k_bf16, v_bf16):
            kf32 = k_bf16.astype(F32)
            k_both = jnp.where(head_lanes, kf32, pltpu.roll(kf32, RET_QK_DIM, 1))
            return _dot((k_both * dk).T.astype(BF16), v_bf16)

        ctx_state = local_state(kc_ref[...], vc_ref[:, vsl])
        for i in range(n_chunks):
            rows = pl.ds(i * C, C)
            a_ref[i] = local_state(k_ref[rows, :], v_ref[rows, vsl])

        s_b = ctx_state
        for i in reversed(range(n_chunks)):
            sb_ref[i] = s_b
            s_b = s_b * cdb + a_ref[i]
        s_f = ctx_state
        for i in range(n_chunks):
            st_ref[e, i] = jnp.where(fwd_rows, s_f, sb_ref[i]).astype(BF16)
            s_f = s_f * cdf + a_ref[i]

    for i in range(n_chunks):
        rows = pl.ds(i * C, C)
        q = q_ref[rows, :].astype(F32)
        q_rolled = pltpu.roll(q, RET_QK_DIM, 1)
        kk = k_ref[rows, :]
        for e in range(2):
            head_lanes = (lane // RET_QK_DIM) == e
            vsl = slice(e * RET_V_DIM, (e + 1) * RET_V_DIM)
            q_own = jnp.where(head_lanes, q, 0.0).astype(BF16)
            q_both = (jnp.where(head_lanes, q, q_rolled) * dq_ref[e]).astype(BF16)
            inner = _dot_nt(q_own, kk) * dbi_ref[e]
            o = _dot(inner.astype(BF16), v_ref[rows, vsl]) + _dot(q_both, st_ref[e, i])
            mu = jnp.mean(o, axis=-1, keepdims=True)
            oc = o - mu
            var = jnp.mean(oc * oc, axis=-1, keepdims=True)
            gate = _silu(g_ref[rows, vsl].astype(F32))
            o_ref[rows, vsl] = (oc * lax.rsqrt(var + EPS) * gate).astype(BF16)


def _retention(p, pc, pc_off_k, pc_off_v, lg_f, lg_b):
    b, l, _ = p.shape
    lc = pc.shape[1]
    assert lc == RET_CHUNK and l % RET_CHUNK == 0
    n_chunks = l // RET_CHUNK
    dk, dq, dbi, cd = _ret_tables(lg_f, lg_b)
    qk_blk = lambda off: (lambda hp, bi: (bi, 0, off // 128 + hp))
    v_blk = lambda off: (lambda hp, bi: (bi, 0, off // 256 + hp))
    pair = lambda *shape: pl.BlockSpec((2,) + shape, lambda hp, bi: (hp,) + (0,) * len(shape))
    return pl.pallas_call(
        functools.partial(_ret_kernel, n_chunks=n_chunks),
        grid=(RET_HEADS // 2, b),
        in_specs=[
            pl.BlockSpec((None, l, 128), qk_blk(OFF_R_Q)),
            pl.BlockSpec((None, l, 128), qk_blk(OFF_R_K)),
            pl.BlockSpec((None, l, 256), v_blk(OFF_R_V)),
            pl.BlockSpec((None, l, 256), v_blk(OFF_R_G)),
            pl.BlockSpec((None, lc, 128), qk_blk(pc_off_k)),
            pl.BlockSpec((None, lc, 256), v_blk(pc_off_v)),
            pair(RET_CHUNK, 128), pair(RET_CHUNK, 128), pair(RET_CHUNK, RET_CHUNK), pair(8, 128),
        ],
        out_specs=pl.BlockSpec((None, l, 256), lambda hp, bi: (bi, 0, hp)),
        out_shape=jax.ShapeDtypeStruct((b, l, RET_V_WIDTH), BF16),
        scratch_shapes=[
            pltpu.VMEM((n_chunks, 128, RET_V_DIM), F32),
            pltpu.VMEM((n_chunks, 128, RET_V_DIM), F32),
            pltpu.VMEM((2, n_chunks, 128, RET_V_DIM), BF16),
        ],
        compiler_params=_params("parallel", "parallel"),
        name="ret",
    )(p, p, p, p, pc, pc, dk, dq, dbi, cd)


OUTPROJ_SUB_ROWS = 128


def _outproj_kernel(na_ref, ret_ref, wt_ref, wb_ref, x_ref, g1_ref, sh2_ref, sc2_ref, npost_ref, npre_ref,
                    xn_ref, h2_ref):
    gate_gain = g1_ref[...] * npost_ref[...]
    gain2 = npre_ref[...] * (1.0 + sc2_ref[...])
    shift2 = sh2_ref[...]
    for s in range(x_ref.shape[0] // OUTPROJ_SUB_ROWS):
        rows = pl.ds(s * OUTPROJ_SUB_ROWS, OUTPROJ_SUB_ROWS)
        mix = _dot(na_ref[rows, :], wt_ref[...]) + _dot(ret_ref[rows, :], wb_ref[...])
        ms = jnp.mean(mix * mix, axis=-1, keepdims=True)
        xn = x_ref[rows, :] + mix * lax.rsqrt(ms + EPS) * gate_gain
        xn_ref[rows, :] = xn
        ms2 = jnp.mean(xn * xn, axis=-1, keepdims=True)
        h2_ref[rows, :] = (xn * lax.rsqrt(ms2 + EPS) * gain2 + shift2).astype(BF16)


def _outproj(na, ret, w_out_bf16, x2, mod3, norm_post, norm_pre, *, rows_per_mod, tm):
    m, d = x2.shape
    kh = na.shape[1]
    mod_row = lambda i: (i * tm) // rows_per_mod
    mod_blk = lambda k: pl.BlockSpec((None, 1, d), lambda i: (mod_row(i), 0, k))
    return pl.pallas_call(
        _outproj_kernel,
        grid=(m // tm,),
        in_specs=[
            pl.BlockSpec((tm, kh), lambda i: (i, 0)),
            pl.BlockSpec((tm, kh), lambda i: (i, 0)),
            pl.BlockSpec((kh, d), lambda i: (0, 0)),
            pl.BlockSpec((kh, d), lambda i: (1, 0)),
            pl.BlockSpec((tm, d), lambda i: (i, 0)),
            mod_blk(2), mod_blk(3), mod_blk(4),
            pl.BlockSpec((1, d), lambda i: (0, 0)),
            pl.BlockSpec((1, d), lambda i: (0, 0)),
        ],
        out_specs=[pl.BlockSpec((tm, d), lambda i: (i, 0)), pl.BlockSpec((tm, d), lambda i: (i, 0))],
        out_shape=[jax.ShapeDtypeStruct((m, d), F32), jax.ShapeDtypeStruct((m, d), BF16)],
        compiler_params=_params("parallel"),
        name="outproj",
    )(na, ret, w_out_bf16, w_out_bf16, x2, mod3, mod3, mod3, norm_post, norm_pre)


FFN_SUB_ROWS = 256
FFN_RESIDUAL_CHUNKS = 4


def _ffn_kernel(h_ref, wg_ref, wu_ref, wd_ref, xn_hbm, g2_ref, npost_ref, o_ref, xn_buf, xn_sem):
    i = pl.program_id(0)
    f = pl.program_id(1)
    tm = h_ref.shape[0]
    chunk = tm // FFN_RESIDUAL_CHUNKS

    def residual_copy(k):
        return pltpu.make_async_copy(xn_hbm.at[pl.ds(i * tm + k * chunk, chunk), :],
                                     xn_buf.at[pl.ds(k * chunk, chunk), :], xn_sem.at[k])

    for k in range(FFN_RESIDUAL_CHUNKS):
        @pl.when(f == k + 1)
        def _():
            residual_copy(k).start()

    def gated():
        h = h_ref[...]
        return (_silu(_dot(h, wg_ref[...])) * _dot(h, wu_ref[...])).astype(BF16)

    last = pl.num_programs(1) - 1

    @pl.when(f == 0)
    def _():
        o_ref[...] = _dot(gated(), wd_ref[...])

    @pl.when((f > 0) & (f < last))
    def _():
        o_ref[...] += _dot(gated(), wd_ref[...])

    @pl.when(f == last)
    def _():
        t = gated()
        for k in range(FFN_RESIDUAL_CHUNKS):
            residual_copy(k).wait()
        gate_gain = g2_ref[...] * npost_ref[...]
        for s in range(tm // FFN_SUB_ROWS):
            rows = slice(s * FFN_SUB_ROWS, (s + 1) * FFN_SUB_ROWS)
            y = o_ref[rows, :] + _dot(t[rows, :], wd_ref[...])
            ms = jnp.mean(y * y, axis=-1, keepdims=True)
            o_ref[rows, :] = xn_buf[rows, :] + y * lax.rsqrt(ms + EPS) * gate_gain


def _ffn(h2, wg, wu, wd, xn, mod3, norm_post, *, rows_per_mod, tm, tf):
    m, d = h2.shape
    dff = wd.shape[0]
    mod_row = lambda i: (i * tm) // rows_per_mod
    assert dff // tf >= FFN_RESIDUAL_CHUNKS + 2 and tm % FFN_RESIDUAL_CHUNKS == 0
    return pl.pallas_call(
        _ffn_kernel,
        grid=(m // tm, dff // tf),
        in_specs=[
            pl.BlockSpec((tm, d), lambda i, f: (i, 0)),
            pl.BlockSpec((d, tf), lambda i, f: (0, f)),
            pl.BlockSpec((d, tf), lambda i, f: (0, f)),
            pl.BlockSpec((tf, d), lambda i, f: (f, 0)),
            pl.BlockSpec(memory_space=pl.ANY),
            pl.BlockSpec((None, 1, d), lambda i, f: (mod_row(i), 0, 5)),
            pl.BlockSpec((1, d), lambda i, f: (0, 0)),
        ],
        out_specs=pl.BlockSpec((tm, d), lambda i, f: (i, 0)),
        out_shape=jax.ShapeDtypeStruct((m, d), F32),
        scratch_shapes=[pltpu.VMEM((tm, d), F32), pltpu.SemaphoreType.DMA((FFN_RESIDUAL_CHUNKS,))],
        compiler_params=_params("parallel", "arbitrary"),
        name="ffn",
    )(h2, wg, wu, wd, xn, mod3, norm_post)


def kernel(x, c, ctx, c_ctx, ada_w, ada_b, norm_pre_mix, norm_post_mix, norm_pre_ffn, norm_post_ffn, w_in,
           na_rpb, ret_log_gamma_fwd, ret_log_gamma_bwd, w_out, w_gate, w_up, w_down):
    b, l, d = x.shape
    lc = ctx.shape[1]
    depth = ada_w.shape[0]
    assert depth == 1, "the context-stream update between layers is not implemented"
    li = 0

    mod_rows = -(-(b + 1) // 8) * 8
    cc = jnp.concatenate([c, c_ctx[None, :], jnp.zeros((mod_rows - b - 1, d), F32)], axis=0)
    mod3 = _adaln(cc, ada_w[li], ada_b[li]).reshape(mod_rows, 1, 6 * d)

    w_in_b = w_in[li].astype(BF16)
    x2 = x.reshape(b * l, d)
    ctx2 = ctx.reshape(b * lc, d)
    gain_pre = norm_pre_mix[li].reshape(1, d)

    p, (w_out_b, w_gate_b, w_up_b, w_down_b) = _inproj(
        x2, gain_pre, mod3, w_in_b, rows_per_mod=l, mod_row0=0,
        col_tiles=tuple(range(IN_WIDTH // 1024)), tm=1024, tn=1024,
        first_tile_scale=NA_HEAD_DIM ** -0.5 * LOG2_E, rope_tile=OFF_R_Q // 1024, rope_tables=_rope_tables(l),
        cast_weights=(w_out[li], w_gate[li], w_up[li], w_down[li]))
    tn = 512
    ctx_tiles = tuple(range(OFF_NA_K // tn, OFF_R_Q // tn)) + tuple(range(OFF_R_K // tn, OFF_R_G // tn))
    pc, _ = _inproj(ctx2, gain_pre, mod3, w_in_b, rows_per_mod=b * lc, mod_row0=b,
                    col_tiles=ctx_tiles, tm=1024, tn=tn)
    p = p.reshape(b, l, IN_WIDTH)
    pc = pc.reshape(b, lc, len(ctx_tiles) * tn)
    pc_na_k, pc_na_v = 0, NA_WIDTH
    pc_r_k, pc_r_v = 2 * NA_WIDTH, 2 * NA_WIDTH + RET_QK_WIDTH

    o_na = _neighbourhood_attention(p, pc, pc_na_k, pc_na_v, na_rpb[li])
    o_ret = _retention(p, pc, pc_r_k, pc_r_v, ret_log_gamma_fwd[li].astype(F32), ret_log_gamma_bwd[li].astype(F32))

    x_new, h2 = _outproj(o_na.reshape(b * l, NA_WIDTH), o_ret.reshape(b * l, RET_V_WIDTH), w_out_b,
                         x2, mod3, norm_post_mix[li].reshape(1, d), norm_pre_ffn[li].reshape(1, d),
                         rows_per_mod=l, tm=512)
    out = _ffn(h2, w_gate_b, w_up_b, w_down_b, x_new, mod3,
               norm_post_ffn[li].reshape(1, d), rows_per_mod=l, tm=1024, tf=512)
    return out.reshape(b, l, d)
```
